```python
import math
import jax, jax.numpy as jnp
from jax import lax
import numpy as np

D_MODEL = 1024
BATCH = 8
SEQ = 4096
DEPTH = 1

GRID_W = 64
CTX_LEN = 256
NA_HEADS = 8
NA_HEAD_DIM = 64
NA_WIN_ROWS = 8
NA_WIN_COLS = 16
GDN_HEADS = 4
GDN_HEAD_DIM = 128
GDN_CONV = 5
GDN_CHUNK = 64
D_FF = 2816
N_MOD = 9
ROPE_THETA = 10000.0
LN_EPS = 1e-6
NORM_EPS = 1e-6
DEEPNORM_ALPHA = (2 * DEPTH) ** 0.25
DEEPNORM_BETA = (8 * DEPTH) ** -0.25
NA_W = NA_HEADS * NA_HEAD_DIM
GDN_W = GDN_HEADS * GDN_HEAD_DIM
IN_SIZES = (NA_W, NA_W, NA_W, 3 * GDN_W, GDN_W, 2 * GDN_HEADS, 2 * GDN_HEADS, D_MODEL, D_MODEL)

kernel_name = 'hybrid_natten_gdn_macaron_dit_layer'


def _ln(h):
    hf = h.astype(jnp.float32)
    mu = jnp.mean(hf, axis=-1, keepdims=True)
    var = jnp.mean(jnp.square(hf - mu), axis=-1, keepdims=True)
    return (hf - mu) * lax.rsqrt(var + LN_EPS)


def post_ln(h, g, b):
    return (_ln(h) * g + b).astype(h.dtype)


def modulate(h, shift, scale):
    return (_ln(h) * (1.0 + scale) + shift).astype(h.dtype)


def swiglu(u, w_up, w_down):
    a, b = jnp.split(u @ w_up, 2, axis=-1)
    return (jax.nn.silu(a) * b) @ w_down


def to_heads(t, n_heads):
    b, n, w = t.shape
    return t.reshape(b, n, n_heads, w // n_heads).transpose(0, 2, 1, 3)


def from_heads(t):
    b, h, n, d = t.shape
    return t.transpose(0, 2, 1, 3).reshape(b, n, h * d)


def split_proj(p):
    return jnp.split(p, np.cumsum(IN_SIZES)[:-1].tolist(), axis=-1)


def l2norm(t):
    return t * lax.rsqrt(jnp.sum(jnp.square(t), axis=-1, keepdims=True) + NORM_EPS)


def axial_rope(n_tok, head_dim):
    n_freq = head_dim // 4
    freqs = ROPE_THETA ** (-jnp.arange(n_freq, dtype=jnp.float32) / n_freq)
    t = jnp.arange(n_tok)
    pos = jnp.stack([t // GRID_W, t % GRID_W], axis=-1).astype(jnp.float32)
    ang = pos[:, :, None] * freqs
    return jnp.cos(ang), jnp.sin(ang)


def apply_rope(x, cos, sin):
    shp = x.shape
    x = x.reshape(shp[:-1] + (2, 2, shp[-1] // 4))
    x1, x2 = x[..., 0, :], x[..., 1, :]
    return jnp.stack([x1 * cos - x2 * sin, x2 * cos + x1 * sin], axis=-2).reshape(shp)


def short_conv(h, w):
    ch = h.shape[-1]
    return lax.conv_general_dilated(h, w[:, None, :].astype(h.dtype), window_strides=(1,),
                                    padding=[(GDN_CONV // 2, GDN_CONV // 2)],
                                    dimension_numbers=('NWC', 'WIO', 'NWC'), feature_group_count=ch)


def na_latent(q, k, v, k_ctx, v_ctx, rpb):
    bsz, n_heads, rows, width, hd = q.shape
    kr = min(NA_WIN_ROWS, rows)
    n_loc = kr * NA_WIN_COLS
    col = np.arange(width)
    col_idx = np.clip(col - NA_WIN_COLS // 2, 0, width - NA_WIN_COLS)[:, None] + np.arange(NA_WIN_COLS)[None, :]
    bias_cols = rpb[:, :, col_idx - col[:, None] + NA_WIN_COLS - 1]
    scale = hd ** -0.5

    def one_row(r):
        r0 = jnp.clip(r - kr // 2, 0, rows - kr)
        k_win = lax.dynamic_slice_in_dim(k, r0, kr, axis=2)[:, :, :, col_idx]
        v_win = lax.dynamic_slice_in_dim(v, r0, kr, axis=2)[:, :, :, col_idx]
        q_r = lax.dynamic_index_in_dim(q, r, axis=2, keepdims=False) * scale
        bias = jnp.take(bias_cols, r0 + jnp.arange(kr) - r + NA_WIN_ROWS - 1, axis=1)
        s_loc = jnp.einsum('bhwd,bhiwjd->bhwij', q_r, k_win) + bias.transpose(0, 2, 1, 3)
        s_ctx = jnp.einsum('bhwd,bhnd->bhwn', q_r, k_ctx)
        s = jnp.concatenate([s_loc.reshape(bsz, n_heads, width, n_loc), s_ctx], axis=-1).astype(jnp.float32)
        p = jax.nn.softmax(s, axis=-1).astype(v.dtype)
        p_loc = p[..., :n_loc].reshape(bsz, n_heads, width, kr, NA_WIN_COLS)
        return (jnp.einsum('bhwij,bhiwjd->bhwd', p_loc, v_win)
                + jnp.einsum('bhwn,bhnd->bhwd', p[..., n_loc:], v_ctx))

    o = lax.map(one_row, jnp.arange(rows))
    return o.transpose(1, 0, 3, 2, 4).reshape(bsz, rows * width, n_heads * hd)


def na_context(q, k, v):
    s = jnp.einsum('bhnd,bhmd->bhnm', q * q.shape[-1] ** -0.5, k).astype(jnp.float32)
    return from_heads(jnp.einsum('bhnm,bhmd->bhnd', jax.nn.softmax(s, axis=-1).astype(v.dtype), v))


def gdn_prepare(qkv, a, b, conv_w, a_log, dt_bias, rope):
    bsz, n, _ = qkv.shape
    qkv = jax.nn.silu(short_conv(qkv, conv_w)).astype(jnp.float32)
    q, k, v = qkv.reshape(bsz, n, 3, GDN_HEADS, GDN_HEAD_DIM).transpose(2, 0, 3, 1, 4)
    q, k = l2norm(q), l2norm(k)
    if rope is not None:
        q, k = apply_rope(q, *rope), apply_rope(k, *rope)
    q = q * GDN_HEAD_DIM ** -0.5
    a = a.astype(jnp.float32).reshape(bsz, n, 2, GDN_HEADS)
    b = b.astype(jnp.float32).reshape(bsz, n, 2, GDN_HEADS)
    g = (-jnp.exp(a_log.astype(jnp.float32)) * jax.nn.softplus(a + dt_bias.astype(jnp.float32))).transpose(2, 0, 3, 1)
    beta = jax.nn.sigmoid(b).transpose(2, 0, 3, 1)

    def both(t):
        return jnp.stack([t, jnp.flip(t, axis=2)])

    return (both(q), both(k), both(v),
            jnp.stack([g[0], jnp.flip(g[1], axis=-1)]),
            jnp.stack([beta[0], jnp.flip(beta[1], axis=-1)]))


def gated_delta_chunks(q, k, v, g, beta, s0, with_output):
    nd = k.ndim - 2
    n_tok, dv = v.shape[-2], v.shape[-1]
    n_chunk = n_tok // GDN_CHUNK

    def chunk(t):
        return t.reshape(t.shape[:nd] + (n_chunk, GDN_CHUNK) + t.shape[nd + 1:])

    k, v, g, beta = chunk(k), chunk(v), jnp.cumsum(chunk(g), axis=-1), chunk(beta)
    lower = np.tril(np.ones((GDN_CHUNK, GDN_CHUNK), dtype=bool))
    strict = np.tril(np.ones((GDN_CHUNK, GDN_CHUNK), dtype=bool), -1)
    decay = jnp.exp(jnp.where(lower, g[..., :, None] - g[..., None, :], -jnp.inf))
    k_beta = k * beta[..., None]
    m = jnp.where(strict, jnp.einsum('...ik,...jk->...ij', k_beta, k) * decay, 0.0)
    rhs = jnp.concatenate([v * beta[..., None], k_beta * jnp.exp(g)[..., None]], axis=-1)
    uw = lax.linalg.triangular_solve(jnp.eye(GDN_CHUNK, dtype=m.dtype) + m, rhs,
                                     left_side=True, lower=True, unit_diagonal=True)
    u, w = uw[..., :dv], uw[..., dv:]
    k_tail = k * jnp.exp(g[..., -1:] - g)[..., None]
    g_tot = jnp.exp(g[..., -1])
    xs = [u, w, k_tail, g_tot]
    if with_output:
        q = chunk(q)
        xs += [q * jnp.exp(g)[..., None], jnp.einsum('...ik,...jk->...ij', q, k) * decay]
    xs = [jnp.moveaxis(t, nd, 0) for t in xs]

    def step(s, xc):
        v_new = xc[0] - jnp.einsum('...ck,...kv->...cv', xc[1], s)
        s_next = s * xc[3][..., None, None] + jnp.einsum('...ck,...cv->...kv', xc[2], v_new)
        if not with_output:
            return s_next, None
        return s_next, (jnp.einsum('...ck,...kv->...cv', xc[4], s)
                        + jnp.einsum('...cj,...jv->...cv', xc[5], v_new))

    s_fin, o = lax.scan(step, s0, xs)
    if not with_output:
        return None, s_fin
    return jnp.moveaxis(o, 0, nd).reshape(q.shape[:nd] + (n_tok, dv)), s_fin


def gdn_output(o, z, norm_w):
    bsz, n_heads, n_tok, dv = o.shape
    o = o.transpose(0, 2, 1, 3)
    o = o * lax.rsqrt(jnp.mean(jnp.square(o), axis=-1, keepdims=True) + NORM_EPS) * norm_w.astype(jnp.float32)
    o = o * jax.nn.silu(z.astype(jnp.float32)).reshape(bsz, n_tok, n_heads, dv)
    return o.reshape(bsz, n_tok, n_heads * dv).astype(z.dtype)


def branch_merge(o_a, o_b, gate_a, gate_b, w_pa, w_pb, w_o):
    y = jax.nn.sigmoid(gate_a) * (o_a @ w_pa) + jax.nn.sigmoid(gate_b) * (o_b @ w_pb)
    return y @ w_o


def hybrid_layer(x, ctx, c, c_ctx, w_ada, b_ada, ln_g, ln_b, ffn1_w_in, ffn1_w_out, w_in, na_rpb,
                 gdn_conv_w, gdn_a_log, gdn_dt_bias, gdn_norm_w, w_pa, w_pb, w_o, ffn2_w_in, ffn2_w_out,
                 rope, update_ctx):
    bsz, n_lat, d = x.shape
    rows = n_lat // GRID_W
    mod_x = (jax.nn.silu(c) @ w_ada + b_ada).reshape(bsz, N_MOD, 1, d)
    mod_c = (jax.nn.silu(c_ctx) @ w_ada + b_ada).reshape(N_MOD, 1, d)

    def sub_mod(m, i):
        return m[..., 3 * i, :, :], m[..., 3 * i + 1, :, :], m[..., 3 * i + 2, :, :]

    def ffn_sublayer(h, m, i, w_up, w_down):
        shift, scale, gate = sub_mod(m, i)
        y = swiglu(modulate(h, shift, scale), w_up, w_down)
        return post_ln(DEEPNORM_ALPHA * h + 0.5 * gate * y, ln_g[i], ln_b[i])

    x = ffn_sublayer(x, mod_x, 0, ffn1_w_in, ffn1_w_out)
    ctx = ffn_sublayer(ctx, mod_c, 0, ffn1_w_in, ffn1_w_out)

    shift_x, scale_x, gate_x = sub_mod(mod_x, 1)
    shift_c, scale_c, gate_c = sub_mod(mod_c, 1)
    qa, ka, va, qkv_b, z_b, a_b, b_b, ga, gb = split_proj(modulate(x, shift_x, scale_x) @ w_in)
    qa_c, ka_c, va_c, qkv_c, z_c, a_c, b_c, ga_c, gb_c = split_proj(modulate(ctx, shift_c, scale_c) @ w_in)

    def grid(t):
        return t.reshape(bsz, rows, GRID_W, NA_HEADS, NA_HEAD_DIM).transpose(0, 3, 1, 2, 4)

    kc_h, vc_h = to_heads(ka_c, NA_HEADS), to_heads(va_c, NA_HEADS)
    o_a = na_latent(grid(qa), grid(ka), grid(va), kc_h, vc_h, na_rpb)

    s0 = jnp.zeros((2, bsz, GDN_HEADS, GDN_HEAD_DIM, GDN_HEAD_DIM), jnp.float32)
    o_ctx2, s_ctx = gated_delta_chunks(*gdn_prepare(qkv_c, a_c, b_c, gdn_conv_w, gdn_a_log, gdn_dt_bias, None),
                                       s0, update_ctx)
    o_lat2, _ = gated_delta_chunks(*gdn_prepare(qkv_b, a_b, b_b, gdn_conv_w, gdn_a_log, gdn_dt_bias, rope),
                                   s_ctx, True)
    o_b = gdn_output(o_lat2[0] + jnp.flip(o_lat2[1], axis=2), z_b, gdn_norm_w)

    x = post_ln(DEEPNORM_ALPHA * x + gate_x * branch_merge(o_a, o_b, ga, gb, w_pa, w_pb, w_o), ln_g[1], ln_b[1])
    if update_ctx:
        o_a_c = na_context(to_heads(qa_c, NA_HEADS), kc_h, vc_h)
        o_b_c = gdn_output(o_ctx2[0] + jnp.flip(o_ctx2[1], axis=2), z_c, gdn_norm_w)
        ctx = post_ln(DEEPNORM_ALPHA * ctx + gate_c * branch_merge(o_a_c, o_b_c, ga_c, gb_c, w_pa, w_pb, w_o),
                      ln_g[1], ln_b[1])
        ctx = ffn_sublayer(ctx, mod_c, 2, ffn2_w_in, ffn2_w_out)
    else:
        ctx = None

    x = ffn_sublayer(x, mod_x, 2, ffn2_w_in, ffn2_w_out)
    return x, ctx


def setup_inputs(seed: int = 0) -> dict:
    key = jax.random.key(seed)
    ks = jax.random.split(key, 22)
    f32 = jnp.float32

    def nrm(k, shape, scale=1.0):
        return jax.random.normal(k, shape, f32) * scale

    n_in = sum(IN_SIZES)
    dt = jnp.exp(jax.random.uniform(ks[14], (DEPTH, 2, GDN_HEADS), f32, math.log(1e-3), math.log(1e-1)))
    return {
        'x': nrm(ks[0], (BATCH, SEQ, D_MODEL)),
        'c': nrm(ks[1], (BATCH, D_MODEL)),
        'ctx': nrm(ks[2], (BATCH, CTX_LEN, D_MODEL)),
        'c_ctx': nrm(ks[3], (D_MODEL,)),
        'w_ada': nrm(ks[4], (DEPTH, D_MODEL, N_MOD * D_MODEL), 0.5 * D_MODEL ** -0.5),
        'b_ada': nrm(ks[5], (DEPTH, N_MOD * D_MODEL), 0.02),
        'ln_g': 1.0 + nrm(ks[6], (DEPTH, 3, D_MODEL), 0.02),
        'ln_b': nrm(ks[7], (DEPTH, 3, D_MODEL), 0.02),
        'ffn1_w_in': nrm(ks[8], (DEPTH, D_MODEL, 2 * D_FF), D_MODEL ** -0.5),
        'ffn1_w_out': nrm(ks[9], (DEPTH, D_FF, D_MODEL), DEEPNORM_BETA * D_FF ** -0.5),
        'w_in': nrm(ks[10], (DEPTH, D_MODEL, n_in), D_MODEL ** -0.5),
        'na_rpb': nrm(ks[11], (DEPTH, NA_HEADS, 2 * NA_WIN_ROWS - 1, 2 * NA_WIN_COLS - 1), 0.1),
        'gdn_conv_w': nrm(ks[12], (DEPTH, GDN_CONV, 3 * GDN_W), GDN_CONV ** -0.5),
        'gdn_a_log': jnp.log(jax.random.uniform(ks[13], (DEPTH, 2, GDN_HEADS), f32, 1.0, 16.0)),
        'gdn_dt_bias': dt + jnp.log(-jnp.expm1(-dt)),
        'gdn_norm_w': 1.0 + nrm(ks[15], (DEPTH, GDN_HEAD_DIM), 0.02),
        'w_pa': nrm(ks[16], (DEPTH, NA_W, D_MODEL), NA_W ** -0.5),
        'w_pb': nrm(ks[17], (DEPTH, GDN_W, D_MODEL), GDN_W ** -0.5),
        'w_o': nrm(ks[18], (DEPTH, D_MODEL, D_MODEL), DEEPNORM_BETA * D_MODEL ** -0.5),
        'ffn2_w_in': nrm(ks[19], (DEPTH, D_MODEL, 2 * D_FF), D_MODEL ** -0.5),
        'ffn2_w_out': nrm(ks[20], (DEPTH, D_FF, D_MODEL), DEEPNORM_BETA * D_FF ** -0.5),
    }


def reference(x, c, ctx, c_ctx, w_ada, b_ada, ln_g, ln_b, ffn1_w_in, ffn1_w_out, w_in, na_rpb,
              gdn_conv_w, gdn_a_log, gdn_dt_bias, gdn_norm_w, w_pa, w_pb, w_o, ffn2_w_in, ffn2_w_out):
    rope = axial_rope(x.shape[1], GDN_HEAD_DIM)
    for layer in range(DEPTH):
        x, ctx = hybrid_layer(x, ctx, c, c_ctx, w_ada[layer], b_ada[layer], ln_g[layer], ln_b[layer],
                              ffn1_w_in[layer], ffn1_w_out[layer], w_in[layer], na_rpb[layer],
                              gdn_conv_w[layer], gdn_a_log[layer], gdn_dt_bias[layer], gdn_norm_w[layer],
                              w_pa[layer], w_pb[layer], w_o[layer], ffn2_w_in[layer], ffn2_w_out[layer],
                              rope, update_ctx=layer < DEPTH - 1)
    return x
```

```python
import functools
import math

import numpy as np
import jax
import jax.numpy as jnp
from jax import lax
from jax.experimental import pallas as pl
from jax.experimental.pallas import tpu as pltpu

F32 = jnp.float32
BF16 = jnp.bfloat16

GRID_W = 64
NA_HEADS = 8
NA_HEAD_DIM = 64
NA_WIN_ROWS = 8
NA_WIN_COLS = 16
GDN_HEADS = 4
GDN_HEAD_DIM = 128
GDN_CONV = 5
N_MOD = 9
ROPE_THETA = 10000.0
LN_EPS = 1e-6
NORM_EPS = 1e-6
NA_W = NA_HEADS * NA_HEAD_DIM
GDN_W = GDN_HEADS * GDN_HEAD_DIM

LANES = 128
SUBLANES = 8
VMEM_LIMIT_BYTES = 56 * 1024 * 1024
MASK_VALUE = -1e30
GDN_CHUNK = 128


def _cparams(*sem):
    return pltpu.CompilerParams(dimension_semantics=sem, vmem_limit_bytes=VMEM_LIMIT_BYTES)


def _ln(x):
    mu = jnp.mean(x, axis=-1, keepdims=True)
    xc = x - mu
    var = jnp.mean(xc * xc, axis=-1, keepdims=True)
    return xc * lax.rsqrt(var + LN_EPS)


def _silu(x):
    return x * jax.nn.sigmoid(x)


def _dot(a, b):
    return jnp.dot(a, b, preferred_element_type=F32)


def _dot_nt(a, b):
    return lax.dot_general(a, b, (((1,), (1,)), ((), ())), preferred_element_type=F32)


def _dot_tn(a, b):
    return lax.dot_general(a, b, (((0,), (0,)), ((), ())), preferred_element_type=F32)


def _dot_exact_lhs(a_bf16, x):
    hi = x.astype(BF16)
    r1 = x - hi.astype(F32)
    mid = r1.astype(BF16)
    lo = (r1 - mid.astype(F32)).astype(BF16)
    return _dot(a_bf16, hi) + _dot(a_bf16, mid) + _dot(a_bf16, lo)


def _mod_kernel(c_ref, w_ref, b_ref, o_ref):
    s = _silu(c_ref[...]).astype(BF16)
    o_ref[...] = _dot(s, w_ref[...].astype(BF16)) + b_ref[...]


def _mod_call(cc, w_ada, b_ada, *, tn):
    r, d = cc.shape
    n = w_ada.shape[1]
    return pl.pallas_call(
        _mod_kernel,
        grid=(n // tn,),
        in_specs=[
            pl.BlockSpec((r, d), lambda j: (0, 0)),
            pl.BlockSpec((d, tn), lambda j: (0, j)),
            pl.BlockSpec((1, tn), lambda j: (0, j)),
        ],
        out_specs=pl.BlockSpec((r, tn), lambda j: (0, j)),
        out_shape=jax.ShapeDtypeStruct((r, n), F32),
        compiler_params=_cparams("arbitrary"),
        name="adaln_modulation",
    )(cc, w_ada, b_ada.reshape(1, n))


def _ffn_kernel(x_ref, m_ref, g_ref, b_ref, wa_ref, wb_ref, wd_ref, o_ref, u_ref, acc_ref, *, sub, alpha):
    j = pl.program_id(2)

    @pl.when(j == 0)
    def _():
        shift = m_ref[0, 3 * sub:3 * sub + 1, :]
        scale = m_ref[0, 3 * sub + 1:3 * sub + 2, :]
        u_ref[...] = (_ln(x_ref[0]) * (1.0 + scale) + shift).astype(BF16)
        acc_ref[...] = jnp.zeros_like(acc_ref)

    u = u_ref[...]
    a = _dot(u, wa_ref[...])
    b = _dot(u, wb_ref[...])
    h = (_silu(a) * b).astype(BF16)
    acc_ref[...] += _dot(h, wd_ref[...])

    @pl.when(j == pl.num_programs(2) - 1)
    def _():
        gate = m_ref[0, 3 * sub + 2:3 * sub + 3, :]
        y = alpha * x_ref[0] + 0.5 * gate * acc_ref[...]
        o_ref[0] = _ln(y) * g_ref[sub:sub + 1, :] + b_ref[sub:sub + 1, :]


def _ffn_call(x, mod3, ln_g, ln_b, w_up, w_down, *, sub, alpha, mod_row, tm, tf):
    bsz, n, d = x.shape
    f = w_down.shape[0]
    nf = f // tf
    if mod_row is None:
        mod_idx = lambda b, i, j: (b, 0, 0)
    else:
        mod_idx = lambda b, i, j: (mod_row, 0, 0)
    return pl.pallas_call(
        functools.partial(_ffn_kernel, sub=sub, alpha=alpha),
        grid=(bsz, n // tm, nf),
        in_specs=[
            pl.BlockSpec((1, tm, d), lambda b, i, j: (b, i, 0)),
            pl.BlockSpec((1, N_MOD, d), mod_idx),
            pl.BlockSpec(ln_g.shape, lambda b, i, j: (0, 0)),
            pl.BlockSpec(ln_b.shape, lambda b, i, j: (0, 0)),
            pl.BlockSpec((d, tf), lambda b, i, j: (0, j)),
            pl.BlockSpec((d, tf), lambda b, i, j: (0, j + nf)),
            pl.BlockSpec((tf, d), lambda b, i, j: (j, 0)),
        ],
        out_specs=pl.BlockSpec((1, tm, d), lambda b, i, j: (b, i, 0)),
        out_shape=jax.ShapeDtypeStruct((bsz, n, d), F32),
        scratch_shapes=[pltpu.VMEM((tm, d), BF16), pltpu.VMEM((tm, d), F32)],
        compiler_params=_cparams("parallel", "parallel", "arbitrary"),
        name=f"ffn_sublayer_{sub}",
    )(x, mod3, ln_g, ln_b, w_up, w_up, w_down)


def _proj_kernel(x_ref, m_ref, wa_ref, wg_ref, wz_ref, wgab_ref, wab_ref,
                 oa_ref, og_ref, oz_ref, ogab_ref, oab_ref):
    shift = m_ref[0, 3:4, :]
    scale = m_ref[0, 4:5, :]
    u = (_ln(x_ref[0]) * (1.0 + scale) + shift).astype(BF16)
    oa_ref[0] = _dot(u, wa_ref[...]).astype(BF16)
    og_ref[0] = _dot(u, wg_ref[...])
    oz_ref[0] = _dot(u, wz_ref[...])
    ogab_ref[0] = _dot(u, wgab_ref[...])
    oab_ref[0] = _dot(u, wab_ref[...])


def _proj_call(x, mod3, w_a, w_g, w_z, w_gab, w_ab, *, mod_row, tm):
    bsz, n, d = x.shape
    if mod_row is None:
        mod_idx = lambda b, i: (b, 0, 0)
    else:
        mod_idx = lambda b, i: (mod_row, 0, 0)
    ws = (w_a, w_g, w_z, w_gab, w_ab)
    dts = (BF16, F32, F32, F32, F32)
    return pl.pallas_call(
        _proj_kernel,
        grid=(bsz, n // tm),
        in_specs=[pl.BlockSpec((1, tm, d), lambda b, i: (b, i, 0)),
                  pl.BlockSpec((1, N_MOD, d), mod_idx)]
                 + [pl.BlockSpec(w.shape, lambda b, i: (0, 0)) for w in ws],
        out_specs=[pl.BlockSpec((1, tm, w.shape[1]), lambda b, i: (b, i, 0)) for w in ws],
        out_shape=[jax.ShapeDtypeStruct((bsz, n, w.shape[1]), dt) for w, dt in zip(ws, dts)],
        compiler_params=_cparams("parallel", "parallel"),
        name="mixer_in_proj",
    )(x, mod3, *ws)


def _attn_kernel(q_ref, k_ref, v_ref, kc_ref, vc_ref, bias_ref, o_ref, *, tr, rows):
    i = pl.program_id(1)
    n_pairs = NA_HEADS // 2
    pw = 2 * NA_HEAD_DIM
    win = NA_WIN_ROWS * GRID_W
    lane = lax.broadcasted_iota(jnp.int32, (GRID_W, pw), 1)
    first_head = lane < NA_HEAD_DIM
    scale = NA_HEAD_DIM ** -0.5

    def row_body(rl, carry):
        r = i * tr + rl
        r0 = jnp.clip(r - NA_WIN_ROWS // 2, 0, rows - NA_WIN_ROWS)
        tok0 = pl.multiple_of(r0 * GRID_W, GRID_W)
        q0 = pl.multiple_of(rl * GRID_W, GRID_W)
        b0 = r0 - r + NA_WIN_ROWS - 1
        for p in range(n_pairs):
            cs = slice(p * pw, (p + 1) * pw)
            q2 = q_ref[0, pl.ds(q0, GRID_W), cs] * scale
            zero = jnp.zeros_like(q2)
            qs = jnp.concatenate([jnp.where(first_head, q2, zero), jnp.where(first_head, zero, q2)], axis=0)
            s_loc = _dot_nt(qs, k_ref[0, pl.ds(tok0, win), cs])
            bias = jnp.concatenate([bias_ref[p, b0 + 2 * m] for m in range(NA_WIN_ROWS // 2)], axis=1)
            s_loc = s_loc + bias
            s_ctx = _dot_nt(qs, kc_ref[0, :, cs])
            mx = jnp.maximum(jnp.max(s_loc, axis=-1, keepdims=True), jnp.max(s_ctx, axis=-1, keepdims=True))
            p_loc = jnp.exp(s_loc - mx)
            p_ctx = jnp.exp(s_ctx - mx)
            den = jnp.sum(p_loc, axis=-1, keepdims=True) + jnp.sum(p_ctx, axis=-1, keepdims=True)
            o = _dot(p_loc.astype(BF16), v_ref[0, pl.ds(tok0, win), cs]) + _dot(p_ctx.astype(BF16), vc_ref[0, :, cs])
            o = o / den
            o_ref[0, pl.ds(q0, GRID_W), cs] = jnp.where(first_head, o[:GRID_W], o[GRID_W:]).astype(o_ref.dtype)
        return carry

    lax.fori_loop(0, tr, row_body, 0)


def _attn_call(qkv, qkv_ctx, bias_tab, *, tr):
    bsz, n, _ = qkv.shape
    rows = n // GRID_W
    lc = qkv_ctx.shape[1]
    assert rows >= NA_WIN_ROWS and rows % tr == 0
    return pl.pallas_call(
        functools.partial(_attn_kernel, tr=tr, rows=rows),
        grid=(bsz, rows // tr),
        in_specs=[
            pl.BlockSpec((1, tr * GRID_W, NA_W), lambda b, i: (b, i, 0)),
            pl.BlockSpec((1, n, NA_W), lambda b, i: (b, 0, 1)),
            pl.BlockSpec((1, n, NA_W), lambda b, i: (b, 0, 2)),
            pl.BlockSpec((1, lc, NA_W), lambda b, i: (b, 0, 1)),
            pl.BlockSpec((1, lc, NA_W), lambda b, i: (b, 0, 2)),
            pl.BlockSpec(bias_tab.shape, lambda b, i: (0, 0, 0, 0)),
        ],
        out_specs=pl.BlockSpec((1, tr * GRID_W, NA_W), lambda b, i: (b, i, 0)),
        out_shape=jax.ShapeDtypeStruct((bsz, n, NA_W), BF16),
        compiler_params=_cparams("parallel", "arbitrary"),
        name="neighbourhood_attention",
    )(qkv, qkv, qkv, qkv_ctx, qkv_ctx, bias_tab)


def _na_bias_table(rpb):
    col = np.arange(GRID_W)
    c0 = np.clip(col - NA_WIN_COLS // 2, 0, GRID_W - NA_WIN_COLS)
    rel = col[None, :] - col[:, None] + NA_WIN_COLS - 1
    valid = (col[None, :] >= c0[:, None]) & (col[None, :] < c0[:, None] + NA_WIN_COLS)
    dense = jnp.where(valid, rpb[:, :, np.clip(rel, 0, 2 * NA_WIN_COLS - 2)], MASK_VALUE)
    two = jnp.concatenate([dense[:, :-1], dense[:, 1:]], axis=-1)
    h, nr = two.shape[0], two.shape[1]
    two = two.reshape(h // 2, 2, nr, GRID_W, 2 * GRID_W).transpose(0, 2, 1, 3, 4)
    return two.reshape(h // 2, nr, 2 * GRID_W, 2 * GRID_W).astype(F32)


def _gdn_prep_kernel(x_ref, xp_ref, xn_ref, ab_ref, cw_ref, nea_ref, dtb_ref, cos_ref, sin_ref,
                     q_ref, k_ref, v_ref, gb_ref, gt_ref, xe_ref, *, t, use_rope, chunk):
    i = pl.program_id(1)
    halo = SUBLANES
    pad = GDN_CONV // 2
    prev = jnp.where(i > 0, xp_ref[0], 0.0)
    nxt = jnp.where(i < pl.num_programs(1) - 1, xn_ref[0], 0.0)
    xe_ref[0:halo, :] = prev
    xe_ref[halo:halo + t, :] = x_ref[0]
    xe_ref[halo + t:, :] = nxt
    acc = cw_ref[0:1, :] * xe_ref[halo - pad:halo - pad + t, :]
    for j in range(1, GDN_CONV):
        acc = acc + cw_ref[j:j + 1, :] * xe_ref[halo - pad + j:halo - pad + j + t, :]
    y = _silu(acc)

    lane = lax.broadcasted_iota(jnp.int32, (t, GDN_W), 1)
    half0 = (lane % (GDN_HEAD_DIM // 2)) < (GDN_HEAD_DIM // 4)

    def norm_rope(z):
        parts = []
        for h in range(GDN_HEADS):
            zh = z[:, h * GDN_HEAD_DIM:(h + 1) * GDN_HEAD_DIM]
            parts.append(zh * lax.rsqrt(jnp.sum(zh * zh, axis=-1, keepdims=True) + NORM_EPS))
        z = jnp.concatenate(parts, axis=1)
        if use_rope:
            quarter = GDN_HEAD_DIM // 4
            partner = jnp.where(half0, pltpu.roll(z, GDN_W - quarter, 1), pltpu.roll(z, quarter, 1))
            cos4 = jnp.concatenate([cos_ref[...]] * GDN_HEADS, axis=1)
            sin4 = jnp.concatenate([sin_ref[...]] * GDN_HEADS, axis=1)
            z = z * cos4 + partner * sin4
        return z

    q_ref[0] = norm_rope(y[:, :GDN_W]) * (GDN_HEAD_DIM ** -0.5)
    k_ref[0] = norm_rope(y[:, GDN_W:2 * GDN_W])
    v_ref[0] = y[:, 2 * GDN_W:]

    ab = ab_ref[0]
    nh2 = 2 * GDN_HEADS
    za = ab + dtb_ref[...]
    softplus = jnp.maximum(za, 0.0) + jnp.log(1.0 + jnp.exp(-jnp.abs(za)))
    g = nea_ref[...] * softplus
    beta = jax.nn.sigmoid(ab)
    row = lax.broadcasted_iota(jnp.int32, (t, t), 0)
    colm = lax.broadcasted_iota(jnp.int32, (t, t), 1)
    same = (row // chunk) == (colm // chunk)
    tri_f = (same & (colm <= row)).astype(BF16)
    tri_b = (same & (colm >= row)).astype(BF16)
    lane_g = lax.broadcasted_iota(jnp.int32, (t, LANES), 1)
    gcum = jnp.where(lane_g < GDN_HEADS, _dot_exact_lhs(tri_f, g), _dot_exact_lhs(tri_b, g))
    gb_ref[0] = jnp.where(lane_g < nh2, gcum, jnp.where(lane_g < 2 * nh2, beta, 0.0))
    gt_ref[0] = jnp.transpose(jnp.where(lane_g < nh2, gcum, 0.0))[0:nh2, :]


def _gdn_prep_call(qkv_g, ab, conv_w, nea_row, dtb_row, cos_t, sin_t, *, use_rope, t, chunk):
    bsz, n, c3 = qkv_g.shape
    nb8 = n // SUBLANES
    tb8 = t // SUBLANES
    f3 = jax.ShapeDtypeStruct((bsz, n, GDN_W), F32)
    return pl.pallas_call(
        functools.partial(_gdn_prep_kernel, t=t, use_rope=use_rope, chunk=chunk),
        grid=(bsz, n // t),
        in_specs=[
            pl.BlockSpec((1, t, c3), lambda b, i: (b, i, 0)),
            pl.BlockSpec((1, SUBLANES, c3), lambda b, i: (b, jnp.maximum(i * tb8 - 1, 0), 0)),
            pl.BlockSpec((1, SUBLANES, c3), lambda b, i: (b, jnp.minimum((i + 1) * tb8, nb8 - 1), 0)),
            pl.BlockSpec((1, t, LANES), lambda b, i: (b, i, 0)),
            pl.BlockSpec(conv_w.shape, lambda b, i: (0, 0)),
            pl.BlockSpec((1, LANES), lambda b, i: (0, 0)),
            pl.BlockSpec((1, LANES), lambda b, i: (0, 0)),
            pl.BlockSpec((t, GDN_HEAD_DIM), lambda b, i: (i, 0)),
            pl.BlockSpec((t, GDN_HEAD_DIM), lambda b, i: (i, 0)),
        ],
        out_specs=[
            pl.BlockSpec((1, t, GDN_W), lambda b, i: (b, i, 0)),
            pl.BlockSpec((1, t, GDN_W), lambda b, i: (b, i, 0)),
            pl.BlockSpec((1, t, GDN_W), lambda b, i: (b, i, 0)),
            pl.BlockSpec((1, t, LANES), lambda b, i: (b, i, 0)),
            pl.BlockSpec((1, 2 * GDN_HEADS, t), lambda b, i: (b, 0, i)),
        ],
        out_shape=[f3, f3, f3,
                   jax.ShapeDtypeStruct((bsz, n, LANES), F32),
                   jax.ShapeDtypeStruct((bsz, 2 * GDN_HEADS, n), F32)],
        scratch_shapes=[pltpu.VMEM((t + 2 * SUBLANES, c3), F32)],
        compiler_params=_cparams("parallel", "parallel"),
        name="gdn_prepare",
    )(qkv_g, qkv_g, qkv_g, ab, conv_w, nea_row, dtb_row, cos_t, sin_t)


TRI_BASE = 8


def _unit_tri_inverse(m):
    c = m.shape[0]
    row = lax.broadcasted_iota(jnp.int32, (c, c), 0)
    col = lax.broadcasted_iota(jnp.int32, (c, c), 1)
    eye = (row == col).astype(F32)
    same = lambda s: (row // s) == (col // s)
    d = jnp.where(same(TRI_BASE), m, 0.0)
    p = eye - d
    db = d.astype(BF16)
    q = _dot(db, db)
    n_iter = int(math.log2(TRI_BASE)) - 1
    for it in range(n_iter):
        qb = q.astype(BF16)
        if it < n_iter - 1:
            yq = _dot(qb, jnp.concatenate([p, q], axis=1).astype(BF16))
            p = p + yq[:, :c]
            q = yq[:, c:]
        else:
            p = p + _dot(qb, p.astype(BF16))
    t = p
    s = TRI_BASE
    while s < c:
        l = jnp.where(same(2 * s) & jnp.logical_not(same(s)), m, 0.0).astype(BF16)
        tb = t.astype(BF16)
        t = t - _dot(tb, _dot(l, tb).astype(BF16))
        s *= 2
    return t


def _gdn_scan_kernel(*refs, tb, chunk, with_output):
    (qf, kf, vf, gf, gtf, qb, kb, vb, gb, gtb, s0_ref) = refs[:11]
    if with_output:
        of_ref, ob_ref, s_ref = refs[11:]
        o_refs = (of_ref, ob_ref)
    else:
        sfin_ref, s_ref = refs[11:]
    ins = ((qf, kf, vf, gf, gtf), (qb, kb, vb, gb, gtb))
    j = pl.program_id(1)
    c = chunk
    nch = tb // c
    hd = GDN_HEAD_DIM

    @pl.when(j == 0)
    def _():
        s_ref[...] = s0_ref[0]

    row = lax.broadcasted_iota(jnp.int32, (c, c), 0)
    col = lax.broadcasted_iota(jnp.int32, (c, c), 1)
    incl = (col <= row, col >= row)
    strict = (col < row, col > row)

    def chunk_body(ci, carry):
        for d in range(2):
            q_ref, k_ref, v_ref, g_ref, gt_ref = ins[d]
            cc = ci if d == 0 else nch - 1 - ci
            off = pl.multiple_of(cc * c, c)
            gblk = g_ref[0, pl.ds(off, c), :]
            gtblk = gt_ref[0, :, pl.ds(off, c)]
            for h in range(GDN_HEADS):
                dh = d * GDN_HEADS + h
                sl = slice(h * hd, (h + 1) * hd)
                q = q_ref[0, pl.ds(off, c), sl]
                k = k_ref[0, pl.ds(off, c), sl]
                v = v_ref[0, pl.ds(off, c), sl]
                g_col = gblk[:, dh:dh + 1]
                g_row = gtblk[dh:dh + 1, :]
                beta = gblk[:, 2 * GDN_HEADS + dh:2 * GDN_HEADS + dh + 1]
                decay = jnp.exp(jnp.where(incl[d], g_col - g_row, MASK_VALUE))
                g_last = g_row[:, c - 1:c] if d == 0 else g_row[:, 0:1]
                e_g = jnp.exp(g_col)
                e_tail = jnp.exp(g_last - g_col)
                g_tot = jnp.exp(g_last)
                k_beta = k * beta
                kb16 = k.astype(BF16)
                m = jnp.where(strict[d], _dot_nt(k_beta.astype(BF16), kb16) * decay, 0.0)
                t_inv = _unit_tri_inverse(m)
                rhs = jnp.concatenate([v * beta, k_beta * e_g], axis=1).astype(BF16)
                uw = _dot(t_inv.astype(BF16), rhs)
                u = uw[:, :hd]
                w = uw[:, hd:]
                s = s_ref[d, h]
                s16 = s.astype(BF16)
                v_new = u - _dot(w.astype(BF16), s16)
                if with_output:
                    qk = _dot_nt(q.astype(BF16), kb16) * decay
                    o = _dot((q * e_g).astype(BF16), s16) + _dot(qk.astype(BF16), v_new.astype(BF16))
                    o_refs[d][0, pl.ds(off, c), sl] = o
                s_ref[d, h] = s * g_tot + _dot_tn(k * e_tail, v_new)
        return carry

    lax.fori_loop(0, nch, chunk_body, 0)

    if not with_output:
        @pl.when(j == pl.num_programs(1) - 1)
        def _():
            sfin_ref[0] = s_ref[...]


def _gdn_scan_call(q, k, v, gbeta, gt, s0, *, tb, chunk, with_output):
    bsz, n, _ = q.shape
    nblk = n // tb
    fwd = lambda b, j: (b, j, 0)
    bwd = lambda b, j: (b, nblk - 1 - j, 0)
    fwd_t = lambda b, j: (b, 0, j)
    bwd_t = lambda b, j: (b, 0, nblk - 1 - j)

    def specs(idx, idx_t):
        return [pl.BlockSpec((1, tb, GDN_W), idx)] * 3 + [pl.BlockSpec((1, tb, LANES), idx),
                                                          pl.BlockSpec((1, 2 * GDN_HEADS, tb), idx_t)]

    state_spec = pl.BlockSpec((1, 2, GDN_HEADS, GDN_HEAD_DIM, GDN_HEAD_DIM), lambda b, j: (b, 0, 0, 0, 0))
    if with_output:
        out_specs = [pl.BlockSpec((1, tb, GDN_W), fwd), pl.BlockSpec((1, tb, GDN_W), bwd)]
        out_shape = [jax.ShapeDtypeStruct((bsz, n, GDN_W), F32)] * 2
    else:
        out_specs = state_spec
        out_shape = jax.ShapeDtypeStruct(s0.shape, F32)
    return pl.pallas_call(
        functools.partial(_gdn_scan_kernel, tb=tb, chunk=chunk, with_output=with_output),
        grid=(bsz, nblk),
        in_specs=specs(fwd, fwd_t) + specs(bwd, bwd_t) + [state_spec],
        out_specs=out_specs,
        out_shape=out_shape,
        scratch_shapes=[pltpu.VMEM((2, GDN_HEADS, GDN_HEAD_DIM, GDN_HEAD_DIM), F32)],
        compiler_params=_cparams("parallel", "arbitrary"),
        name="gdn_scan_latent" if with_output else "gdn_scan_context",
    )(q, k, v, gbeta, gt, q, k, v, gbeta, gt, s0)


def _merge_kernel(x_ref, m_ref, g_ref, b_ref, oa_ref, of_ref, ob_ref, z_ref, gab_ref, nw_ref,
                  wpa_ref, wpb_ref, wo_ref, o_ref, *, alpha):
    o = of_ref[0] + ob_ref[0]
    nw = nw_ref[...]
    parts = []
    for h in range(GDN_HEADS):
        oh = o[:, h * GDN_HEAD_DIM:(h + 1) * GDN_HEAD_DIM]
        parts.append(oh * lax.rsqrt(jnp.mean(oh * oh, axis=-1, keepdims=True) + NORM_EPS) * nw)
    o_b = (jnp.concatenate(parts, axis=1) * _silu(z_ref[0])).astype(BF16)
    d = x_ref.shape[-1]
    gab = gab_ref[0]
    y = (jax.nn.sigmoid(gab[:, :d]) * _dot(oa_ref[0], wpa_ref[...])
         + jax.nn.sigmoid(gab[:, d:]) * _dot(o_b, wpb_ref[...]))
    yo = _dot(y.astype(BF16), wo_ref[...])
    gate = m_ref[0, 5:6, :]
    o_ref[0] = _ln(alpha * x_ref[0] + gate * yo) * g_ref[1:2, :] + b_ref[1:2, :]


def _merge_call(x, mod3, ln_g, ln_b, o_a, o_f, o_b, z, gab, norm_w, w_pa, w_pb, w_o, *, alpha, tm):
    bsz, n, d = x.shape
    tok = lambda w: pl.BlockSpec((1, tm, w), lambda b, i: (b, i, 0))
    full = lambda a: pl.BlockSpec(a.shape, lambda b, i: (0,) * a.ndim)
    return pl.pallas_call(
        functools.partial(_merge_kernel, alpha=alpha),
        grid=(bsz, n // tm),
        in_specs=[tok(d), pl.BlockSpec((1, N_MOD, d), lambda b, i: (b, 0, 0)), full(ln_g), full(ln_b),
                  tok(NA_W), tok(GDN_W), tok(GDN_W), tok(GDN_W), tok(2 * d), full(norm_w),
                  full(w_pa), full(w_pb), full(w_o)],
        out_specs=tok(d),
        out_shape=jax.ShapeDtypeStruct((bsz, n, d), F32),
        compiler_params=_cparams("parallel", "parallel"),
        name="branch_merge",
    )(x, mod3, ln_g, ln_b, o_a, o_f, o_b, z, gab, norm_w, w_pa, w_pb, w_o)


def _rope_tables(n_tok):
    n_freq = GDN_HEAD_DIM // 4
    freqs = ROPE_THETA ** (-jnp.arange(n_freq, dtype=F32) / n_freq)
    t = jnp.arange(n_tok)
    pos = jnp.stack([t // GRID_W, t % GRID_W], axis=-1).astype(F32)
    ang = pos[:, :, None] * freqs
    cos = jnp.cos(ang)
    sin = jnp.sin(ang)
    cos_t = jnp.concatenate([cos, cos], axis=-1).reshape(n_tok, GDN_HEAD_DIM)
    sin_t = jnp.concatenate([-sin, sin], axis=-1).reshape(n_tok, GDN_HEAD_DIM)
    return cos_t, sin_t


def _pick_tile(n, pref):
    t = min(n, pref)
    assert n % t == 0
    return t


def kernel(x, c, ctx, c_ctx, w_ada, b_ada, ln_g, ln_b, ffn1_w_in, ffn1_w_out, w_in, na_rpb, gdn_conv_w,
           gdn_a_log, gdn_dt_bias, gdn_norm_w, w_pa, w_pb, w_o, ffn2_w_in, ffn2_w_out):
    depth = w_ada.shape[0]
    assert depth == 1, "context-update path of non-final layers is not implemented"
    bsz, n, d = x.shape
    lc = ctx.shape[1]
    alpha = (2 * depth) ** 0.25
    layer = 0
    f = ffn1_w_out.shape[1]

    n_rows = -(-(bsz + 1) // SUBLANES) * SUBLANES
    cc = jnp.zeros((n_rows, d), F32).at[:bsz].set(c).at[bsz].set(c_ctx)
    mod3 = _mod_call(cc, w_ada[layer], b_ada[layer], tn=1152).reshape(n_rows, N_MOD, d)

    g_ln, b_ln = ln_g[layer], ln_b[layer]
    tf = 1408 if f % 1408 == 0 else f
    w1u, w1d = ffn1_w_in[layer].astype(BF16), ffn1_w_out[layer].astype(BF16)
    x1 = _ffn_call(x, mod3, g_ln, b_ln, w1u, w1d, sub=0, alpha=alpha, mod_row=None, tm=_pick_tile(n, 512), tf=tf)
    ctx1 = _ffn_call(ctx, mod3, g_ln, b_ln, w1u, w1d, sub=0, alpha=alpha, mod_row=bsz, tm=_pick_tile(lc, 512), tf=tf)

    wi = w_in[layer]
    o0 = 3 * NA_W
    o1 = o0 + 3 * GDN_W
    o2 = o1 + GDN_W
    o3 = o2 + 4 * GDN_HEADS
    w_a = wi[:, :o0].astype(BF16)
    w_g = wi[:, o0:o1].astype(BF16)
    w_z = wi[:, o1:o2].astype(BF16)
    w_ab = jnp.zeros((d, LANES), F32).at[:, :4 * GDN_HEADS].set(wi[:, o2:o3]).astype(BF16)
    w_gab = wi[:, o3:].astype(BF16)
    qkv_a, qkv_g, z, gab, ab = _proj_call(x1, mod3, w_a, w_g, w_z, w_gab, w_ab, mod_row=None, tm=_pick_tile(n, 256))
    qkv_a_c, qkv_g_c, _, _, ab_c = _proj_call(ctx1, mod3, w_a, w_g, w_z, w_gab, w_ab, mod_row=bsz,
                                              tm=_pick_tile(lc, 256))

    o_a = _attn_call(qkv_a, qkv_a_c, _na_bias_table(na_rpb[layer]), tr=8)

    nea = jnp.zeros((1, LANES), F32).at[0, :2 * GDN_HEADS].set(-jnp.exp(gdn_a_log[layer].astype(F32)).reshape(-1))
    dtb = jnp.zeros((1, LANES), F32).at[0, :2 * GDN_HEADS].set(gdn_dt_bias[layer].astype(F32).reshape(-1))
    cw = gdn_conv_w[layer]
    cos_t, sin_t = _rope_tables(n)
    tp = _pick_tile(n, 256)
    tpc = _pick_tile(lc, 256)
    q_l, k_l, v_l, gb_l, gt_l = _gdn_prep_call(qkv_g, ab, cw, nea, dtb, cos_t, sin_t, use_rope=True, t=tp,
                                                chunk=GDN_CHUNK)
    q_c, k_c, v_c, gb_c, gt_c = _gdn_prep_call(qkv_g_c, ab_c, cw, nea, dtb, cos_t[:lc], sin_t[:lc], use_rope=False,
                                                t=tpc, chunk=GDN_CHUNK)
    s0 = jnp.zeros((bsz, 2, GDN_HEADS, GDN_HEAD_DIM, GDN_HEAD_DIM), F32)
    s_ctx = _gdn_scan_call(q_c, k_c, v_c, gb_c, gt_c, s0, tb=tpc, chunk=GDN_CHUNK, with_output=False)
    o_f, o_b = _gdn_scan_call(q_l, k_l, v_l, gb_l, gt_l, s_ctx, tb=tp, chunk=GDN_CHUNK, with_output=True)

    x2 = _merge_call(x1, mod3, g_ln, b_ln, o_a, o_f, o_b, z, gab, gdn_norm_w[layer].reshape(1, GDN_HEAD_DIM),
                     w_pa[layer].astype(BF16), w_pb[layer].astype(BF16), w_o[layer].astype(BF16),
                     alpha=alpha, tm=_pick_tile(n, 256))

    w2u, w2d = ffn2_w_in[layer].astype(BF16), ffn2_w_out[layer].astype(BF16)
    return _ffn_call(x2, mod3, g_ln, b_ln, w2u, w2d, sub=2, alpha=alpha, mod_row=None, tm=_pick_tile(n, 512), tf=tf)
```

```python
import functools
import math

import numpy as np
import jax
import jax.numpy as jnp
from jax import lax
from jax.experimental import pallas as pl
from jax.experimental.pallas import tpu as pltpu

F32 = jnp.float32
BF16 = jnp.bfloat16

GRID_W = 64
NA_HEADS = 8
NA_HEAD_DIM = 64
NA_WIN_ROWS = 8
NA_WIN_COLS = 16
GDN_HEADS = 4
GDN_HEAD_DIM = 128
GDN_CONV = 5
N_MOD = 9
ROPE_THETA = 10000.0
LN_EPS = 1e-6
NORM_EPS = 1e-6
NA_W = NA_HEADS * NA_HEAD_DIM
GDN_W = GDN_HEADS * GDN_HEAD_DIM

LANES = 128
SUBLANES = 8
VMEM_LIMIT_BYTES = 56 * 1024 * 1024
MASK_VALUE = -1e30
GDN_CHUNK = 128


def _cparams(*sem):
    return pltpu.CompilerParams(dimension_semantics=sem, vmem_limit_bytes=VMEM_LIMIT_BYTES)


def _ln(x):
    mu = jnp.mean(x, axis=-1, keepdims=True)
    xc = x - mu
    var = jnp.mean(xc * xc, axis=-1, keepdims=True)
    return xc * lax.rsqrt(var + LN_EPS)


def _silu(x):
    return x * jax.nn.sigmoid(x)


def _dot(a, b):
    return jnp.dot(a, b, preferred_element_type=F32)


def _dot_nt(a, b):
    return lax.dot_general(a, b, (((1,), (1,)), ((), ())), preferred_element_type=F32)


def _dot_tn(a, b):
    return lax.dot_general(a, b, (((0,), (0,)), ((), ())), preferred_element_type=F32)


def _dot_exact_lhs(a_bf16, x):
    hi = x.astype(BF16)
    r1 = x - hi.astype(F32)
    mid = r1.astype(BF16)
    lo = (r1 - mid.astype(F32)).astype(BF16)
    return _dot(a_bf16, hi) + _dot(a_bf16, mid) + _dot(a_bf16, lo)


def _mod_kernel(c_ref, w_ref, b_ref, o_ref):
    s = _silu(c_ref[...]).astype(BF16)
    o_ref[...] = _dot(s, w_ref[...].astype(BF16)) + b_ref[...]


def _mod_call(cc, w_ada, b_ada, *, tn):
    r, d = cc.shape
    n = w_ada.shape[1]
    return pl.pallas_call(
        _mod_kernel,
        grid=(n // tn,),
        in_specs=[
            pl.BlockSpec((r, d), lambda j: (0, 0)),
            pl.BlockSpec((d, tn), lambda j: (0, j)),
            pl.BlockSpec((1, tn), lambda j: (0, j)),
        ],
        out_specs=pl.BlockSpec((r, tn), lambda j: (0, j)),
        out_shape=jax.ShapeDtypeStruct((r, n), F32),
        compiler_params=_cparams("arbitrary"),
        name="adaln_modulation",
    )(cc, w_ada, b_ada.reshape(1, n))


def _ffn_kernel(x_ref, m_ref, g_ref, b_ref, wa_ref, wb_ref, wd_ref, o_ref, u_ref, acc_ref, *, sub, alpha):
    j = pl.program_id(2)

    @pl.when(j == 0)
    def _():
        shift = m_ref[0, 3 * sub:3 * sub + 1, :]
        scale = m_ref[0, 3 * sub + 1:3 * sub + 2, :]
        u_ref[...] = (_ln(x_ref[0]) * (1.0 + scale) + shift).astype(BF16)
        acc_ref[...] = jnp.zeros_like(acc_ref)

    u = u_ref[...]
    a = _dot(u, wa_ref[...])
    b = _dot(u, wb_ref[...])
    h = (_silu(a) * b).astype(BF16)
    acc_ref[...] += _dot(h, wd_ref[...])

    @pl.when(j == pl.num_programs(2) - 1)
    def _():
        gate = m_ref[0, 3 * sub + 2:3 * sub + 3, :]
        y = alpha * x_ref[0] + 0.5 * gate * acc_ref[...]
        o_ref[0] = _ln(y) * g_ref[sub:sub + 1, :] + b_ref[sub:sub + 1, :]


def _ffn_call(x, mod3, ln_g, ln_b, w_up, w_down, *, sub, alpha, mod_row, tm, tf):
    bsz, n, d = x.shape
    f = w_down.shape[0]
    nf = f // tf
    if mod_row is None:
        mod_idx = lambda b, i, j: (b, 0, 0)
    else:
        mod_idx = lambda b, i, j: (mod_row, 0, 0)
    return pl.pallas_call(
        functools.partial(_ffn_kernel, sub=sub, alpha=alpha),
        grid=(bsz, n // tm, nf),
        in_specs=[
            pl.BlockSpec((1, tm, d), lambda b, i, j: (b, i, 0)),
            pl.BlockSpec((1, N_MOD, d), mod_idx),
            pl.BlockSpec(ln_g.shape, lambda b, i, j: (0, 0)),
            pl.BlockSpec(ln_b.shape, lambda b, i, j: (0, 0)),
            pl.BlockSpec((d, tf), lambda b, i, j: (0, j)),
            pl.BlockSpec((d, tf), lambda b, i, j: (0, j + nf)),
            pl.BlockSpec((tf, d), lambda b, i, j: (j, 0)),
        ],
        out_specs=pl.BlockSpec((1, tm, d), lambda b, i, j: (b, i, 0)),
        out_shape=jax.ShapeDtypeStruct((bsz, n, d), F32),
        scratch_shapes=[pltpu.VMEM((tm, d), BF16), pltpu.VMEM((tm, d), F32)],
        compiler_params=_cparams("parallel", "parallel", "arbitrary"),
        name=f"ffn_sublayer_{sub}",
    )(x, mod3, ln_g, ln_b, w_up, w_up, w_down)


def _proj_kernel(x_ref, m_ref, wa_ref, wg_ref, wz_ref, wgab_ref, wab_ref,
                 oa_ref, og_ref, oz_ref, ogab_ref, oab_ref):
    shift = m_ref[0, 3:4, :]
    scale = m_ref[0, 4:5, :]
    u = (_ln(x_ref[0]) * (1.0 + scale) + shift).astype(BF16)
    oa_ref[0] = _dot(u, wa_ref[...]).astype(BF16)
    og_ref[0] = _dot(u, wg_ref[...])
    oz_ref[0] = _dot(u, wz_ref[...])
    ogab_ref[0] = _dot(u, wgab_ref[...])
    oab_ref[0] = _dot(u, wab_ref[...])


def _proj_call(x, mod3, w_a, w_g, w_z, w_gab, w_ab, *, mod_row, tm):
    bsz, n, d = x.shape
    if mod_row is None:
        mod_idx = lambda b, i: (b, 0, 0)
    else:
        mod_idx = lambda b, i: (mod_row, 0, 0)
    ws = (w_a, w_g, w_z, w_gab, w_ab)
    dts = (BF16, F32, F32, F32, F32)
    return pl.pallas_call(
        _proj_kernel,
        grid=(bsz, n // tm),
        in_specs=[pl.BlockSpec((1, tm, d), lambda b, i: (b, i, 0)),
                  pl.BlockSpec((1, N_MOD, d), mod_idx)]
                 + [pl.BlockSpec(w.shape, lambda b, i: (0, 0)) for w in ws],
        out_specs=[pl.BlockSpec((1, tm, w.shape[1]), lambda b, i: (b, i, 0)) for w in ws],
        out_shape=[jax.ShapeDtypeStruct((bsz, n, w.shape[1]), dt) for w, dt in zip(ws, dts)],
        compiler_params=_cparams("parallel", "parallel"),
        name="mixer_in_proj",
    )(x, mod3, *ws)


def _attn_kernel(q_ref, k_ref, v_ref, kc_ref, vc_ref, bias_ref, o_ref, *, tr, rows):
    i = pl.program_id(1)
    n_pairs = NA_HEADS // 2
    pw = 2 * NA_HEAD_DIM
    win = NA_WIN_ROWS * GRID_W
    lane = lax.broadcasted_iota(jnp.int32, (GRID_W, pw), 1)
    first_head = lane < NA_HEAD_DIM
    scale = NA_HEAD_DIM ** -0.5

    def row_body(rl, carry):
        r = i * tr + rl
        r0 = jnp.clip(r - NA_WIN_ROWS // 2, 0, rows - NA_WIN_ROWS)
        tok0 = pl.multiple_of(r0 * GRID_W, GRID_W)
        q0 = pl.multiple_of(rl * GRID_W, GRID_W)
        b0 = r0 - r + NA_WIN_ROWS - 1
        for p in range(n_pairs):
            cs = slice(p * pw, (p + 1) * pw)
            q2 = q_ref[0, pl.ds(q0, GRID_W), cs] * scale
            zero = jnp.zeros_like(q2)
            qs = jnp.concatenate([jnp.where(first_head, q2, zero), jnp.where(first_head, zero, q2)], axis=0)
            s_loc = _dot_nt(qs, k_ref[0, pl.ds(tok0, win), cs])
            bias = jnp.concatenate([bias_ref[p, b0 + 2 * m] for m in range(NA_WIN_ROWS // 2)], axis=1)
            s_loc = s_loc + bias
            s_ctx = _dot_nt(qs, kc_ref[0, :, cs])
            mx = jnp.maximum(jnp.max(s_loc, axis=-1, keepdims=True), jnp.max(s_ctx, axis=-1, keepdims=True))
            p_loc = jnp.exp(s_loc - mx)
            p_ctx = jnp.exp(s_ctx - mx)
            den = jnp.sum(p_loc, axis=-1, keepdims=True) + jnp.sum(p_ctx, axis=-1, keepdims=True)
            o = _dot(p_loc.astype(BF16), v_ref[0, pl.ds(tok0, win), cs]) + _dot(p_ctx.astype(BF16), vc_ref[0, :, cs])
            o = o / den
            o_ref[0, pl.ds(q0, GRID_W), cs] = jnp.where(first_head, o[:GRID_W], o[GRID_W:]).astype(o_ref.dtype)
        return carry

    lax.fori_loop(0, tr, row_body, 0)


def _attn_call(qkv, qkv_ctx, bias_tab, *, tr):
    bsz, n, _ = qkv.shape
    rows = n // GRID_W
    lc = qkv_ctx.shape[1]
    assert rows >= NA_WIN_ROWS and rows % tr == 0
    return pl.pallas_call(
        functools.partial(_attn_kernel, tr=tr, rows=rows),
        grid=(bsz, rows // tr),
        in_specs=[
            pl.BlockSpec((1, tr * GRID_W, NA_W), lambda b, i: (b, i, 0)),
            pl.BlockSpec((1, n, NA_W), lambda b, i: (b, 0, 1)),
            pl.BlockSpec((1, n, NA_W), lambda b, i: (b, 0, 2)),
            pl.BlockSpec((1, lc, NA_W), lambda b, i: (b, 0, 1)),
            pl.BlockSpec((1, lc, NA_W), lambda b, i: (b, 0, 2)),
            pl.BlockSpec(bias_tab.shape, lambda b, i: (0, 0, 0, 0)),
        ],
        out_specs=pl.BlockSpec((1, tr * GRID_W, NA_W), lambda b, i: (b, i, 0)),
        out_shape=jax.ShapeDtypeStruct((bsz, n, NA_W), BF16),
        compiler_params=_cparams("parallel", "arbitrary"),
        name="neighbourhood_attention",
    )(qkv, qkv, qkv, qkv_ctx, qkv_ctx, bias_tab)


def _na_bias_table(rpb):
    col = np.arange(GRID_W)
    c0 = np.clip(col - NA_WIN_COLS // 2, 0, GRID_W - NA_WIN_COLS)
    rel = col[None, :] - col[:, None] + NA_WIN_COLS - 1
    valid = (col[None, :] >= c0[:, None]) & (col[None, :] < c0[:, None] + NA_WIN_COLS)
    dense = jnp.where(valid, rpb[:, :, np.clip(rel, 0, 2 * NA_WIN_COLS - 2)], MASK_VALUE)
    two = jnp.concatenate([dense[:, :-1], dense[:, 1:]], axis=-1)
    h, nr = two.shape[0], two.shape[1]
    two = two.reshape(h // 2, 2, nr, GRID_W, 2 * GRID_W).transpose(0, 2, 1, 3, 4)
    return two.reshape(h // 2, nr, 2 * GRID_W, 2 * GRID_W).astype(F32)


def _gdn_prep_kernel(x_ref, xp_ref, xn_ref, ab_ref, cw_ref, nea_ref, dtb_ref, cos_ref, sin_ref,
                     q_ref, k_ref, v_ref, gb_ref, gt_ref, xe_ref, *, t, use_rope, chunk):
    i = pl.program_id(1)
    halo = SUBLANES
    pad = GDN_CONV // 2
    prev = jnp.where(i > 0, xp_ref[0], 0.0)
    nxt = jnp.where(i < pl.num_programs(1) - 1, xn_ref[0], 0.0)
    xe_ref[0:halo, :] = prev
    xe_ref[halo:halo + t, :] = x_ref[0]
    xe_ref[halo + t:, :] = nxt
    acc = cw_ref[0:1, :] * xe_ref[halo - pad:halo - pad + t, :]
    for j in range(1, GDN_CONV):
        acc = acc + cw_ref[j:j + 1, :] * xe_ref[halo - pad + j:halo - pad + j + t, :]
    y = _silu(acc)

    lane = lax.broadcasted_iota(jnp.int32, (t, GDN_W), 1)
    half0 = (lane % (GDN_HEAD_DIM // 2)) < (GDN_HEAD_DIM // 4)

    def norm_rope(z):
        parts = []
        for h in range(GDN_HEADS):
            zh = z[:, h * GDN_HEAD_DIM:(h + 1) * GDN_HEAD_DIM]
            parts.append(zh * lax.rsqrt(jnp.sum(zh * zh, axis=-1, keepdims=True) + NORM_EPS))
        z = jnp.concatenate(parts, axis=1)
        if use_rope:
            quarter = GDN_HEAD_DIM // 4
            partner = jnp.where(half0, pltpu.roll(z, GDN_W - quarter, 1), pltpu.roll(z, quarter, 1))
            cos4 = jnp.concatenate([cos_ref[...]] * GDN_HEADS, axis=1)
            sin4 = jnp.concatenate([sin_ref[...]] * GDN_HEADS, axis=1)
            z = z * cos4 + partner * sin4
        return z

    q_ref[0] = norm_rope(y[:, :GDN_W]) * (GDN_HEAD_DIM ** -0.5)
    k_ref[0] = norm_rope(y[:, GDN_W:2 * GDN_W])
    v_ref[0] = y[:, 2 * GDN_W:]

    ab = ab_ref[0]
    nh2 = 2 * GDN_HEADS
    za = ab + dtb_ref[...]
    softplus = jnp.maximum(za, 0.0) + jnp.log(1.0 + jnp.exp(-jnp.abs(za)))
    g = nea_ref[...] * softplus
    beta = jax.nn.sigmoid(ab)
    row = lax.broadcasted_iota(jnp.int32, (t, t), 0)
    colm = lax.broadcasted_iota(jnp.int32, (t, t), 1)
    same = (row // chunk) == (colm // chunk)
    tri_f = (same & (colm <= row)).astype(BF16)
    tri_b = (same & (colm >= row)).astype(BF16)
    lane_g = lax.broadcasted_iota(jnp.int32, (t, LANES), 1)
    gcum = jnp.where(lane_g < GDN_HEADS, _dot_exact_lhs(tri_f, g), _dot_exact_lhs(tri_b, g))
    gtot = pltpu.roll(_dot_exact_lhs(same.astype(BF16), g), 2 * nh2, 1)
    gb_ref[0] = jnp.where(lane_g < nh2, gcum,
                          jnp.where(lane_g < 2 * nh2, beta, jnp.where(lane_g < 3 * nh2, gtot, 0.0)))
    gt_ref[0] = jnp.transpose(jnp.where(lane_g < nh2, gcum, 0.0))[0:nh2, :]


def _gdn_prep_call(qkv_g, ab, conv_w, nea_row, dtb_row, cos_t, sin_t, *, use_rope, t, chunk):
    bsz, n, c3 = qkv_g.shape
    nb8 = n // SUBLANES
    tb8 = t // SUBLANES
    f3 = jax.ShapeDtypeStruct((bsz, n, GDN_W), F32)
    return pl.pallas_call(
        functools.partial(_gdn_prep_kernel, t=t, use_rope=use_rope, chunk=chunk),
        grid=(bsz, n // t),
        in_specs=[
            pl.BlockSpec((1, t, c3), lambda b, i: (b, i, 0)),
            pl.BlockSpec((1, SUBLANES, c3), lambda b, i: (b, jnp.maximum(i * tb8 - 1, 0), 0)),
            pl.BlockSpec((1, SUBLANES, c3), lambda b, i: (b, jnp.minimum((i + 1) * tb8, nb8 - 1), 0)),
            pl.BlockSpec((1, t, LANES), lambda b, i: (b, i, 0)),
            pl.BlockSpec(conv_w.shape, lambda b, i: (0, 0)),
            pl.BlockSpec((1, LANES), lambda b, i: (0, 0)),
            pl.BlockSpec((1, LANES), lambda b, i: (0, 0)),
            pl.BlockSpec((t, GDN_HEAD_DIM), lambda b, i: (i, 0)),
            pl.BlockSpec((t, GDN_HEAD_DIM), lambda b, i: (i, 0)),
        ],
        out_specs=[
            pl.BlockSpec((1, t, GDN_W), lambda b, i: (b, i, 0)),
            pl.BlockSpec((1, t, GDN_W), lambda b, i: (b, i, 0)),
            pl.BlockSpec((1, t, GDN_W), lambda b, i: (b, i, 0)),
            pl.BlockSpec((1, t, LANES), lambda b, i: (b, i, 0)),
            pl.BlockSpec((1, 2 * GDN_HEADS, t), lambda b, i: (b, 0, i)),
        ],
        out_shape=[f3, f3, f3,
                   jax.ShapeDtypeStruct((bsz, n, LANES), F32),
                   jax.ShapeDtypeStruct((bsz, 2 * GDN_HEADS, n), F32)],
        scratch_shapes=[pltpu.VMEM((t + 2 * SUBLANES, c3), F32)],
        compiler_params=_cparams("parallel", "parallel"),
        name="gdn_prepare",
    )(qkv_g, qkv_g, qkv_g, ab, conv_w, nea_row, dtb_row, cos_t, sin_t)


TRI_BASE = 8


def _unit_tri_inverse(ms):
    c = ms[0].shape[0]
    row = lax.broadcasted_iota(jnp.int32, (c, c), 0)
    col = lax.broadcasted_iota(jnp.int32, (c, c), 1)
    eye = (row == col).astype(F32)
    same = lambda s: (row // s) == (col // s)
    base = same(TRI_BASE)
    ds = [jnp.where(base, m, 0.0) for m in ms]
    ps = [eye - d for d in ds]
    dbs = [d.astype(BF16) for d in ds]
    qs = [_dot(db, db) for db in dbs]
    n_iter = int(math.log2(TRI_BASE)) - 1
    for it in range(n_iter):
        qbs = [q.astype(BF16) for q in qs]
        if it < n_iter - 1:
            yqs = [_dot(qb, jnp.concatenate([p, q], axis=1).astype(BF16)) for qb, p, q in zip(qbs, ps, qs)]
            ps = [p + yq[:, :c] for p, yq in zip(ps, yqs)]
            qs = [yq[:, c:] for yq in yqs]
        else:
            ys = [_dot(qb, p.astype(BF16)) for qb, p in zip(qbs, ps)]
            ps = [p + y for p, y in zip(ps, ys)]
    ts = ps
    s = TRI_BASE
    while s < c:
        off_diag = same(2 * s) & jnp.logical_not(same(s))
        ls = [jnp.where(off_diag, m, 0.0).astype(BF16) for m in ms]
        tbs = [t.astype(BF16) for t in ts]
        xs = [_dot(l, tb).astype(BF16) for l, tb in zip(ls, tbs)]
        ys = [_dot(tb, x) for tb, x in zip(tbs, xs)]
        ts = [t - y for t, y in zip(ts, ys)]
        s *= 2
    return ts


def _gdn_scan_kernel(*refs, tb, chunk, with_output):
    (qf, kf, vf, gf, gtf, qb, kb, vb, gb, gtb, s0_ref) = refs[:11]
    if with_output:
        of_ref, ob_ref, s_ref = refs[11:]
        o_refs = (of_ref, ob_ref)
    else:
        sfin_ref, s_ref = refs[11:]
    ins = ((qf, kf, vf, gf, gtf), (qb, kb, vb, gb, gtb))
    j = pl.program_id(1)
    c = chunk
    nch = tb // c
    hd = GDN_HEAD_DIM

    @pl.when(j == 0)
    def _():
        s_ref[...] = s0_ref[0]

    row = lax.broadcasted_iota(jnp.int32, (c, c), 0)
    col = lax.broadcasted_iota(jnp.int32, (c, c), 1)
    incl = (col <= row, col >= row)
    strict = (col < row, col > row)
    nh2 = 2 * GDN_HEADS
    units = [(d, h) for d in range(2) for h in range(GDN_HEADS)]
    assert c <= LANES

    def chunk_body(ci, carry):
        offs = [pl.multiple_of((ci if d == 0 else nch - 1 - ci) * c, c) for d in range(2)]
        gblk = [ins[d][3][0, pl.ds(offs[d], c), :] for d in range(2)]
        gtblk = [ins[d][4][0, :, pl.ds(offs[d], c)] for d in range(2)]

        def lane_bcast(d, h, o):
            dh = d * GDN_HEADS + h
            return jnp.broadcast_to(gblk[d][:, o + dh:o + dh + 1], (c, LANES))

        g_b = [lane_bcast(d, h, 0) for d, h in units]
        beta = [lane_bcast(d, h, nh2) for d, h in units]
        g_last = [lane_bcast(d, h, 2 * nh2) for d, h in units]
        g_row = [gtblk[d][d * GDN_HEADS + h:d * GDN_HEADS + h + 1, :] for d, h in units]
        ld = lambda idx: [ins[d][idx][0, pl.ds(offs[d], c), h * hd:(h + 1) * hd] for d, h in units]
        q, k, v = ld(0), ld(1), ld(2)
        decay = [jnp.exp(jnp.where(incl[d], gb_[:, :c] - gr, MASK_VALUE)) for (d, h), gb_, gr in zip(units, g_b, g_row)]
        e_g = [jnp.exp(x) for x in g_b]
        k_beta = [a * b for a, b in zip(k, beta)]
        k16 = [a.astype(BF16) for a in k]
        kk = [_dot_nt(a.astype(BF16), b) for a, b in zip(k_beta, k16)]
        m = [jnp.where(strict[d], a * b, 0.0) for (d, h), a, b in zip(units, kk, decay)]
        t_inv = _unit_tri_inverse(m)
        rhs = [jnp.concatenate([a * b, kb_ * e], axis=1).astype(BF16) for a, b, kb_, e in zip(v, beta, k_beta, e_g)]
        uw = [_dot(t.astype(BF16), r) for t, r in zip(t_inv, rhs)]
        s = [s_ref[d, h] for d, h in units]
        s16 = [a.astype(BF16) for a in s]
        ws = [_dot(a[:, hd:].astype(BF16), b) for a, b in zip(uw, s16)]
        v_new = [a[:, :hd] - b for a, b in zip(uw, ws)]
        if with_output:
            qk = [_dot_nt(a.astype(BF16), b) * dcy for a, b, dcy in zip(q, k16, decay)]
            o_state = [_dot((a * e).astype(BF16), b) for a, e, b in zip(q, e_g, s16)]
            o_local = [_dot(a.astype(BF16), b.astype(BF16)) for a, b in zip(qk, v_new)]
            for (d, h), a, b in zip(units, o_state, o_local):
                o_refs[d][0, pl.ds(offs[d], c), h * hd:(h + 1) * hd] = a + b
        ktv = [_dot_tn(a * jnp.exp(gl - gb_), b) for a, gl, gb_, b in zip(k, g_last, g_b, v_new)]
        for (d, h), a, gl, b in zip(units, s, g_last, ktv):
            s_ref[d, h] = a * jnp.exp(gl[0:1, :]) + b
        return carry

    lax.fori_loop(0, nch, chunk_body, 0)

    if not with_output:
        @pl.when(j == pl.num_programs(1) - 1)
        def _():
            sfin_ref[0] = s_ref[...]


def _gdn_scan_call(q, k, v, gbeta, gt, s0, *, tb, chunk, with_output):
    bsz, n, _ = q.shape
    nblk = n // tb
    fwd = lambda b, j: (b, j, 0)
    bwd = lambda b, j: (b, nblk - 1 - j, 0)
    fwd_t = lambda b, j: (b, 0, j)
    bwd_t = lambda b, j: (b, 0, nblk - 1 - j)

    def specs(idx, idx_t):
        return [pl.BlockSpec((1, tb, GDN_W), idx)] * 3 + [pl.BlockSpec((1, tb, LANES), idx),
                                                          pl.BlockSpec((1, 2 * GDN_HEADS, tb), idx_t)]

    state_spec = pl.BlockSpec((1, 2, GDN_HEADS, GDN_HEAD_DIM, GDN_HEAD_DIM), lambda b, j: (b, 0, 0, 0, 0))
    if with_output:
        out_specs = [pl.BlockSpec((1, tb, GDN_W), fwd), pl.BlockSpec((1, tb, GDN_W), bwd)]
        out_shape = [jax.ShapeDtypeStruct((bsz, n, GDN_W), F32)] * 2
    else:
        out_specs = state_spec
        out_shape = jax.ShapeDtypeStruct(s0.shape, F32)
    return pl.pallas_call(
        functools.partial(_gdn_scan_kernel, tb=tb, chunk=chunk, with_output=with_output),
        grid=(bsz, nblk),
        in_specs=specs(fwd, fwd_t) + specs(bwd, bwd_t) + [state_spec],
        out_specs=out_specs,
        out_shape=out_shape,
        scratch_shapes=[pltpu.VMEM((2, GDN_HEADS, GDN_HEAD_DIM, GDN_HEAD_DIM), F32)],
        compiler_params=_cparams("parallel", "arbitrary"),
        name="gdn_scan_latent" if with_output else "gdn_scan_context",
    )(q, k, v, gbeta, gt, q, k, v, gbeta, gt, s0)


def _merge_kernel(x_ref, m_ref, g_ref, b_ref, oa_ref, of_ref, ob_ref, z_ref, gab_ref, nw_ref,
                  wpa_ref, wpb_ref, wo_ref, o_ref, *, alpha):
    o = of_ref[0] + ob_ref[0]
    nw = nw_ref[...]
    parts = []
    for h in range(GDN_HEADS):
        oh = o[:, h * GDN_HEAD_DIM:(h + 1) * GDN_HEAD_DIM]
        parts.append(oh * lax.rsqrt(jnp.mean(oh * oh, axis=-1, keepdims=True) + NORM_EPS) * nw)
    o_b = (jnp.concatenate(parts, axis=1) * _silu(z_ref[0])).astype(BF16)
    d = x_ref.shape[-1]
    gab = gab_ref[0]
    y = (jax.nn.sigmoid(gab[:, :d]) * _dot(oa_ref[0], wpa_ref[...])
         + jax.nn.sigmoid(gab[:, d:]) * _dot(o_b, wpb_ref[...]))
    yo = _dot(y.astype(BF16), wo_ref[...])
    gate = m_ref[0, 5:6, :]
    o_ref[0] = _ln(alpha * x_ref[0] + gate * yo) * g_ref[1:2, :] + b_ref[1:2, :]


def _merge_call(x, mod3, ln_g, ln_b, o_a, o_f, o_b, z, gab, norm_w, w_pa, w_pb, w_o, *, alpha, tm):
    bsz, n, d = x.shape
    tok = lambda w: pl.BlockSpec((1, tm, w), lambda b, i: (b, i, 0))
    full = lambda a: pl.BlockSpec(a.shape, lambda b, i: (0,) * a.ndim)
    return pl.pallas_call(
        functools.partial(_merge_kernel, alpha=alpha),
        grid=(bsz, n // tm),
        in_specs=[tok(d), pl.BlockSpec((1, N_MOD, d), lambda b, i: (b, 0, 0)), full(ln_g), full(ln_b),
                  tok(NA_W), tok(GDN_W), tok(GDN_W), tok(GDN_W), tok(2 * d), full(norm_w),
                  full(w_pa), full(w_pb), full(w_o)],
        out_specs=tok(d),
        out_shape=jax.ShapeDtypeStruct((bsz, n, d), F32),
        compiler_params=_cparams("parallel", "parallel"),
        name="branch_merge",
    )(x, mod3, ln_g, ln_b, o_a, o_f, o_b, z, gab, norm_w, w_pa, w_pb, w_o)


def _rope_tables(n_tok):
    n_freq = GDN_HEAD_DIM // 4
    freqs = ROPE_THETA ** (-jnp.arange(n_freq, dtype=F32) / n_freq)
    t = jnp.arange(n_tok)
    pos = jnp.stack([t // GRID_W, t % GRID_W], axis=-1).astype(F32)
    ang = pos[:, :, None] * freqs
    cos = jnp.cos(ang)
    sin = jnp.sin(ang)
    cos_t = jnp.concatenate([cos, cos], axis=-1).reshape(n_tok, GDN_HEAD_DIM)
    sin_t = jnp.concatenate([-sin, sin], axis=-1).reshape(n_tok, GDN_HEAD_DIM)
    return cos_t, sin_t


def _pick_tile(n, pref):
    t = min(n, pref)
    assert n % t == 0
    return t


def kernel(x, c, ctx, c_ctx, w_ada, b_ada, ln_g, ln_b, ffn1_w_in, ffn1_w_out, w_in, na_rpb, gdn_conv_w,
           gdn_a_log, gdn_dt_bias, gdn_norm_w, w_pa, w_pb, w_o, ffn2_w_in, ffn2_w_out):
    depth = w_ada.shape[0]
    assert depth == 1, "context-update path of non-final layers is not implemented"
    bsz, n, d = x.shape
    lc = ctx.shape[1]
    alpha = (2 * depth) ** 0.25
    layer = 0
    f = ffn1_w_out.shape[1]

    n_rows = -(-(bsz + 1) // SUBLANES) * SUBLANES
    cc = jnp.zeros((n_rows, d), F32).at[:bsz].set(c).at[bsz].set(c_ctx)
    mod3 = _mod_call(cc, w_ada[layer], b_ada[layer], tn=1152).reshape(n_rows, N_MOD, d)

    g_ln, b_ln = ln_g[layer], ln_b[layer]
    tf = 1408 if f % 1408 == 0 else f
    w1u, w1d = ffn1_w_in[layer].astype(BF16), ffn1_w_out[layer].astype(BF16)
    x1 = _ffn_call(x, mod3, g_ln, b_ln, w1u, w1d, sub=0, alpha=alpha, mod_row=None, tm=_pick_tile(n, 512), tf=tf)
    ctx1 = _ffn_call(ctx, mod3, g_ln, b_ln, w1u, w1d, sub=0, alpha=alpha, mod_row=bsz, tm=_pick_tile(lc, 512), tf=tf)

    wi = w_in[layer]
    o0 = 3 * NA_W
    o1 = o0 + 3 * GDN_W
    o2 = o1 + GDN_W
    o3 = o2 + 4 * GDN_HEADS
    w_a = wi[:, :o0].astype(BF16)
    w_g = wi[:, o0:o1].astype(BF16)
    w_z = wi[:, o1:o2].astype(BF16)
    w_ab = jnp.zeros((d, LANES), F32).at[:, :4 * GDN_HEADS].set(wi[:, o2:o3]).astype(BF16)
    w_gab = wi[:, o3:].astype(BF16)
    qkv_a, qkv_g, z, gab, ab = _proj_call(x1, mod3, w_a, w_g, w_z, w_gab, w_ab, mod_row=None, tm=_pick_tile(n, 256))
    qkv_a_c, qkv_g_c, _, _, ab_c = _proj_call(ctx1, mod3, w_a, w_g, w_z, w_gab, w_ab, mod_row=bsz,
                                              tm=_pick_tile(lc, 256))

    o_a = _attn_call(qkv_a, qkv_a_c, _na_bias_table(na_rpb[layer]), tr=8)

    nea = jnp.zeros((1, LANES), F32).at[0, :2 * GDN_HEADS].set(-jnp.exp(gdn_a_log[layer].astype(F32)).reshape(-1))
    dtb = jnp.zeros((1, LANES), F32).at[0, :2 * GDN_HEADS].set(gdn_dt_bias[layer].astype(F32).reshape(-1))
    cw = gdn_conv_w[layer]
    cos_t, sin_t = _rope_tables(n)
    tp = _pick_tile(n, 256)
    tpc = _pick_tile(lc, 256)
    q_l, k_l, v_l, gb_l, gt_l = _gdn_prep_call(qkv_g, ab, cw, nea, dtb, cos_t, sin_t, use_rope=True, t=tp,
                                                chunk=GDN_CHUNK)
    q_c, k_c, v_c, gb_c, gt_c = _gdn_prep_call(qkv_g_c, ab_c, cw, nea, dtb, cos_t[:lc], sin_t[:lc], use_rope=False,
                                                t=tpc, chunk=GDN_CHUNK)
    s0 = jnp.zeros((bsz, 2, GDN_HEADS, GDN_HEAD_DIM, GDN_HEAD_DIM), F32)
    s_ctx = _gdn_scan_call(q_c, k_c, v_c, gb_c, gt_c, s0, tb=tpc, chunk=GDN_CHUNK, with_output=False)
    o_f, o_b = _gdn_scan_call(q_l, k_l, v_l, gb_l, gt_l, s_ctx, tb=tp, chunk=GDN_CHUNK, with_output=True)

    x2 = _merge_call(x1, mod3, g_ln, b_ln, o_a, o_f, o_b, z, gab, gdn_norm_w[layer].reshape(1, GDN_HEAD_DIM),
                     w_pa[layer].astype(BF16), w_pb[layer].astype(BF16), w_o[layer].astype(BF16),
                     alpha=alpha, tm=_pick_tile(n, 256))

    w2u, w2d = ffn2_w_in[layer].astype(BF16), ffn2_w_out[layer].astype(BF16)
    return _ffn_call(x2, mod3, g_ln, b_ln, w2u, w2d, sub=2, alpha=alpha, mod_row=None, tm=_pick_tile(n, 512), tf=tf)
```

```python
import functools
import math

import numpy as np
import jax
import jax.numpy as jnp
from jax import lax
from jax.experimental import pallas as pl
from jax.experimental.pallas import tpu as pltpu

F32 = jnp.float32
BF16 = jnp.bfloat16

GRID_W = 64
NA_HEADS = 8
NA_HEAD_DIM = 64
NA_WIN_ROWS = 8
NA_WIN_COLS = 16
GDN_HEADS = 4
GDN_HEAD_DIM = 128
GDN_CONV = 5
N_MOD = 9
ROPE_THETA = 10000.0
LN_EPS = 1e-6
NORM_EPS = 1e-6
NA_W = NA_HEADS * NA_HEAD_DIM
GDN_W = GDN_HEADS * GDN_HEAD_DIM

LANES = 128
SUBLANES = 8
VMEM_LIMIT_BYTES = 56 * 1024 * 1024
MASK_VALUE = -1e30
GDN_CHUNK = 128


def _cparams(*sem):
    return pltpu.CompilerParams(dimension_semantics=sem, vmem_limit_bytes=VMEM_LIMIT_BYTES)


def _ln(x):
    mu = jnp.mean(x, axis=-1, keepdims=True)
    xc = x - mu
    var = jnp.mean(xc * xc, axis=-1, keepdims=True)
    return xc * lax.rsqrt(var + LN_EPS)


def _silu(x):
    return x * jax.nn.sigmoid(x)


def _dot(a, b):
    return jnp.dot(a, b, preferred_element_type=F32)


def _dot_nt(a, b):
    return lax.dot_general(a, b, (((1,), (1,)), ((), ())), preferred_element_type=F32)


def _dot_tn(a, b):
    return lax.dot_general(a, b, (((0,), (0,)), ((), ())), preferred_element_type=F32)


def _dot_exact_lhs(a_bf16, x):
    hi = x.astype(BF16)
    r1 = x - hi.astype(F32)
    mid = r1.astype(BF16)
    lo = (r1 - mid.astype(F32)).astype(BF16)
    return _dot(a_bf16, hi) + _dot(a_bf16, mid) + _dot(a_bf16, lo)


def _mod_kernel(c_ref, w_ref, b_ref, o_ref):
    s = _silu(c_ref[...]).astype(BF16)
    o_ref[...] = _dot(s, w_ref[...].astype(BF16)) + b_ref[...]


def _mod_call(cc, w_ada, b_ada, *, tn):
    r, d = cc.shape
    n = w_ada.shape[1]
    return pl.pallas_call(
        _mod_kernel,
        grid=(n // tn,),
        in_specs=[
            pl.BlockSpec((r, d), lambda j: (0, 0)),
            pl.BlockSpec((d, tn), lambda j: (0, j)),
            pl.BlockSpec((1, tn), lambda j: (0, j)),
        ],
        out_specs=pl.BlockSpec((r, tn), lambda j: (0, j)),
        out_shape=jax.ShapeDtypeStruct((r, n), F32),
        compiler_params=_cparams("arbitrary"),
        name="adaln_modulation",
    )(cc, w_ada, b_ada.reshape(1, n))


def _ffn_kernel(x_ref, m_ref, g_ref, b_ref, wa_ref, wb_ref, wd_ref, o_ref, u_ref, acc_ref, *, sub, alpha):
    j = pl.program_id(2)

    @pl.when(j == 0)
    def _():
        shift = m_ref[0, 3 * sub:3 * sub + 1, :]
        scale = m_ref[0, 3 * sub + 1:3 * sub + 2, :]
        u_ref[...] = (_ln(x_ref[0]) * (1.0 + scale) + shift).astype(BF16)
        acc_ref[...] = jnp.zeros_like(acc_ref)

    u = u_ref[...]
    a = _dot(u, wa_ref[...])
    b = _dot(u, wb_ref[...])
    h = (_silu(a) * b).astype(BF16)
    acc_ref[...] += _dot(h, wd_ref[...])

    @pl.when(j == pl.num_programs(2) - 1)
    def _():
        gate = m_ref[0, 3 * sub + 2:3 * sub + 3, :]
        y = alpha * x_ref[0] + 0.5 * gate * acc_ref[...]
        o_ref[0] = _ln(y) * g_ref[sub:sub + 1, :] + b_ref[sub:sub + 1, :]


def _ffn_call(x, mod3, ln_g, ln_b, w_up, w_down, *, sub, alpha, mod_row, tm, tf):
    bsz, n, d = x.shape
    f = w_down.shape[0]
    nf = f // tf
    if mod_row is None:
        mod_idx = lambda b, i, j: (b, 0, 0)
    else:
        mod_idx = lambda b, i, j: (mod_row, 0, 0)
    return pl.pallas_call(
        functools.partial(_ffn_kernel, sub=sub, alpha=alpha),
        grid=(bsz, n // tm, nf),
        in_specs=[
            pl.BlockSpec((1, tm, d), lambda b, i, j: (b, i, 0)),
            pl.BlockSpec((1, N_MOD, d), mod_idx),
            pl.BlockSpec(ln_g.shape, lambda b, i, j: (0, 0)),
            pl.BlockSpec(ln_b.shape, lambda b, i, j: (0, 0)),
            pl.BlockSpec((d, tf), lambda b, i, j: (0, j)),
            pl.BlockSpec((d, tf), lambda b, i, j: (0, j + nf)),
            pl.BlockSpec((tf, d), lambda b, i, j: (j, 0)),
        ],
        out_specs=pl.BlockSpec((1, tm, d), lambda b, i, j: (b, i, 0)),
        out_shape=jax.ShapeDtypeStruct((bsz, n, d), F32),
        scratch_shapes=[pltpu.VMEM((tm, d), BF16), pltpu.VMEM((tm, d), F32)],
        compiler_params=_cparams("parallel", "parallel", "arbitrary"),
        name=f"ffn_sublayer_{sub}",
    )(x, mod3, ln_g, ln_b, w_up, w_up, w_down)


def _proj_kernel(x_ref, m_ref, wa_ref, wg_ref, wz_ref, wgab_ref, wab_ref,
                 oa_ref, og_ref, oz_ref, ogab_ref, oab_ref):
    shift = m_ref[0, 3:4, :]
    scale = m_ref[0, 4:5, :]
    u = (_ln(x_ref[0]) * (1.0 + scale) + shift).astype(BF16)
    oa_ref[0] = _dot(u, wa_ref[...]).astype(BF16)
    og_ref[0] = _dot(u, wg_ref[...])
    oz_ref[0] = _dot(u, wz_ref[...])
    ogab_ref[0] = _dot(u, wgab_ref[...])
    oab_ref[0] = _dot(u, wab_ref[...])


def _proj_call(x, mod3, w_a, w_g, w_z, w_gab, w_ab, *, mod_row, tm):
    bsz, n, d = x.shape
    if mod_row is None:
        mod_idx = lambda b, i: (b, 0, 0)
    else:
        mod_idx = lambda b, i: (mod_row, 0, 0)
    ws = (w_a, w_g, w_z, w_gab, w_ab)
    dts = (BF16, F32, F32, F32, F32)
    return pl.pallas_call(
        _proj_kernel,
        grid=(bsz, n // tm),
        in_specs=[pl.BlockSpec((1, tm, d), lambda b, i: (b, i, 0)),
                  pl.BlockSpec((1, N_MOD, d), mod_idx)]
                 + [pl.BlockSpec(w.shape, lambda b, i: (0, 0)) for w in ws],
        out_specs=[pl.BlockSpec((1, tm, w.shape[1]), lambda b, i: (b, i, 0)) for w in ws],
        out_shape=[jax.ShapeDtypeStruct((bsz, n, w.shape[1]), dt) for w, dt in zip(ws, dts)],
        compiler_params=_cparams("parallel", "parallel"),
        name="mixer_in_proj",
    )(x, mod3, *ws)


def _attn_kernel(q_ref, k_ref, v_ref, kc_ref, vc_ref, bias_ref, o_ref, *, tr, rows):
    i = pl.program_id(1)
    n_pairs = NA_HEADS // 2
    pw = 2 * NA_HEAD_DIM
    win = NA_WIN_ROWS * GRID_W
    lane = lax.broadcasted_iota(jnp.int32, (GRID_W, pw), 1)
    first_head = lane < NA_HEAD_DIM
    scale = NA_HEAD_DIM ** -0.5

    def row_body(rl, carry):
        r = i * tr + rl
        r0 = jnp.clip(r - NA_WIN_ROWS // 2, 0, rows - NA_WIN_ROWS)
        tok0 = pl.multiple_of(r0 * GRID_W, GRID_W)
        q0 = pl.multiple_of(rl * GRID_W, GRID_W)
        b0 = r0 - r + NA_WIN_ROWS - 1
        pairs = range(n_pairs)
        cs = [slice(p * pw, (p + 1) * pw) for p in pairs]
        qs = []
        for p in pairs:
            q2 = q_ref[0, pl.ds(q0, GRID_W), cs[p]] * scale
            zero = jnp.zeros_like(q2)
            qs.append(jnp.concatenate([jnp.where(first_head, q2, zero), jnp.where(first_head, zero, q2)], axis=0))
        s_loc = [_dot_nt(qs[p], k_ref[0, pl.ds(tok0, win), cs[p]]) for p in pairs]
        s_ctx = [_dot_nt(qs[p], kc_ref[0, :, cs[p]]) for p in pairs]
        s_loc = [s_loc[p] + jnp.concatenate([bias_ref[p, b0 + 2 * m] for m in range(NA_WIN_ROWS // 2)], axis=1)
                 for p in pairs]
        mx = [jnp.maximum(jnp.max(s_loc[p], axis=-1, keepdims=True), jnp.max(s_ctx[p], axis=-1, keepdims=True))
              for p in pairs]
        p_loc = [jnp.exp(s_loc[p] - mx[p]) for p in pairs]
        p_ctx = [jnp.exp(s_ctx[p] - mx[p]) for p in pairs]
        den = [jnp.sum(p_loc[p], axis=-1, keepdims=True) + jnp.sum(p_ctx[p], axis=-1, keepdims=True) for p in pairs]
        o = [_dot(p_loc[p].astype(BF16), v_ref[0, pl.ds(tok0, win), cs[p]])
             + _dot(p_ctx[p].astype(BF16), vc_ref[0, :, cs[p]]) for p in pairs]
        for p in pairs:
            on = o[p] / den[p]
            o_ref[0, pl.ds(q0, GRID_W), cs[p]] = jnp.where(first_head, on[:GRID_W], on[GRID_W:]).astype(o_ref.dtype)
        return carry

    lax.fori_loop(0, tr, row_body, 0)


def _attn_call(qkv, qkv_ctx, bias_tab, *, tr):
    bsz, n, _ = qkv.shape
    rows = n // GRID_W
    lc = qkv_ctx.shape[1]
    assert rows >= NA_WIN_ROWS and rows % tr == 0
    return pl.pallas_call(
        functools.partial(_attn_kernel, tr=tr, rows=rows),
        grid=(bsz, rows // tr),
        in_specs=[
            pl.BlockSpec((1, tr * GRID_W, NA_W), lambda b, i: (b, i, 0)),
            pl.BlockSpec((1, n, NA_W), lambda b, i: (b, 0, 1)),
            pl.BlockSpec((1, n, NA_W), lambda b, i: (b, 0, 2)),
            pl.BlockSpec((1, lc, NA_W), lambda b, i: (b, 0, 1)),
            pl.BlockSpec((1, lc, NA_W), lambda b, i: (b, 0, 2)),
            pl.BlockSpec(bias_tab.shape, lambda b, i: (0, 0, 0, 0)),
        ],
        out_specs=pl.BlockSpec((1, tr * GRID_W, NA_W), lambda b, i: (b, i, 0)),
        out_shape=jax.ShapeDtypeStruct((bsz, n, NA_W), BF16),
        compiler_params=_cparams("parallel", "arbitrary"),
        name="neighbourhood_attention",
    )(qkv, qkv, qkv, qkv_ctx, qkv_ctx, bias_tab)


def _na_bias_table(rpb):
    col = np.arange(GRID_W)
    c0 = np.clip(col - NA_WIN_COLS // 2, 0, GRID_W - NA_WIN_COLS)
    rel = col[None, :] - col[:, None] + NA_WIN_COLS - 1
    valid = (col[None, :] >= c0[:, None]) & (col[None, :] < c0[:, None] + NA_WIN_COLS)
    dense = jnp.where(valid, rpb[:, :, np.clip(rel, 0, 2 * NA_WIN_COLS - 2)], MASK_VALUE)
    two = jnp.concatenate([dense[:, :-1], dense[:, 1:]], axis=-1)
    h, nr = two.shape[0], two.shape[1]
    two = two.reshape(h // 2, 2, nr, GRID_W, 2 * GRID_W).transpose(0, 2, 1, 3, 4)
    return two.reshape(h // 2, nr, 2 * GRID_W, 2 * GRID_W).astype(F32)


def _gdn_prep_kernel(x_ref, xp_ref, xn_ref, ab_ref, cw_ref, nea_ref, dtb_ref, cos_ref, sin_ref,
                     q_ref, k_ref, v_ref, gb_ref, gt_ref, xe_ref, *, t, use_rope, chunk):
    i = pl.program_id(1)
    halo = SUBLANES
    pad = GDN_CONV // 2
    prev = jnp.where(i > 0, xp_ref[0], 0.0)
    nxt = jnp.where(i < pl.num_programs(1) - 1, xn_ref[0], 0.0)
    xe_ref[0:halo, :] = prev
    xe_ref[halo:halo + t, :] = x_ref[0]
    xe_ref[halo + t:, :] = nxt
    acc = cw_ref[0:1, :] * xe_ref[halo - pad:halo - pad + t, :]
    for j in range(1, GDN_CONV):
        acc = acc + cw_ref[j:j + 1, :] * xe_ref[halo - pad + j:halo - pad + j + t, :]
    y = _silu(acc)

    lane = lax.broadcasted_iota(jnp.int32, (t, GDN_W), 1)
    half0 = (lane % (GDN_HEAD_DIM // 2)) < (GDN_HEAD_DIM // 4)

    def norm_rope(z):
        parts = []
        for h in range(GDN_HEADS):
            zh = z[:, h * GDN_HEAD_DIM:(h + 1) * GDN_HEAD_DIM]
            parts.append(zh * lax.rsqrt(jnp.sum(zh * zh, axis=-1, keepdims=True) + NORM_EPS))
        z = jnp.concatenate(parts, axis=1)
        if use_rope:
            quarter = GDN_HEAD_DIM // 4
            partner = jnp.where(half0, pltpu.roll(z, GDN_W - quarter, 1), pltpu.roll(z, quarter, 1))
            cos4 = jnp.concatenate([cos_ref[...]] * GDN_HEADS, axis=1)
            sin4 = jnp.concatenate([sin_ref[...]] * GDN_HEADS, axis=1)
            z = z * cos4 + partner * sin4
        return z

    q_ref[0] = norm_rope(y[:, :GDN_W]) * (GDN_HEAD_DIM ** -0.5)
    k_ref[0] = norm_rope(y[:, GDN_W:2 * GDN_W])
    v_ref[0] = y[:, 2 * GDN_W:]

    ab = ab_ref[0]
    nh2 = 2 * GDN_HEADS
    za = ab + dtb_ref[...]
    softplus = jnp.maximum(za, 0.0) + jnp.log(1.0 + jnp.exp(-jnp.abs(za)))
    g = nea_ref[...] * softplus
    beta = jax.nn.sigmoid(ab)
    row = lax.broadcasted_iota(jnp.int32, (t, t), 0)
    colm = lax.broadcasted_iota(jnp.int32, (t, t), 1)
    same = (row // chunk) == (colm // chunk)
    tri_f = (same & (colm <= row)).astype(BF16)
    tri_b = (same & (colm >= row)).astype(BF16)
    lane_g = lax.broadcasted_iota(jnp.int32, (t, LANES), 1)
    gcum = jnp.where(lane_g < GDN_HEADS, _dot_exact_lhs(tri_f, g), _dot_exact_lhs(tri_b, g))
    gtot = pltpu.roll(_dot_exact_lhs(same.astype(BF16), g), 2 * nh2, 1)
    gb_ref[0] = jnp.where(lane_g < nh2, gcum,
                          jnp.where(lane_g < 2 * nh2, beta, jnp.where(lane_g < 3 * nh2, gtot, 0.0)))
    gt_ref[0] = jnp.transpose(jnp.where(lane_g < nh2, gcum, 0.0))[0:nh2, :]


def _gdn_prep_call(qkv_g, ab, conv_w, nea_row, dtb_row, cos_t, sin_t, *, use_rope, t, chunk):
    bsz, n, c3 = qkv_g.shape
    nb8 = n // SUBLANES
    tb8 = t // SUBLANES
    f3 = jax.ShapeDtypeStruct((bsz, n, GDN_W), F32)
    return pl.pallas_call(
        functools.partial(_gdn_prep_kernel, t=t, use_rope=use_rope, chunk=chunk),
        grid=(bsz, n // t),
        in_specs=[
            pl.BlockSpec((1, t, c3), lambda b, i: (b, i, 0)),
            pl.BlockSpec((1, SUBLANES, c3), lambda b, i: (b, jnp.maximum(i * tb8 - 1, 0), 0)),
            pl.BlockSpec((1, SUBLANES, c3), lambda b, i: (b, jnp.minimum((i + 1) * tb8, nb8 - 1), 0)),
            pl.BlockSpec((1, t, LANES), lambda b, i: (b, i, 0)),
            pl.BlockSpec(conv_w.shape, lambda b, i: (0, 0)),
            pl.BlockSpec((1, LANES), lambda b, i: (0, 0)),
            pl.BlockSpec((1, LANES), lambda b, i: (0, 0)),
            pl.BlockSpec((t, GDN_HEAD_DIM), lambda b, i: (i, 0)),
            pl.BlockSpec((t, GDN_HEAD_DIM), lambda b, i: (i, 0)),
        ],
        out_specs=[
            pl.BlockSpec((1, t, GDN_W), lambda b, i: (b, i, 0)),
            pl.BlockSpec((1, t, GDN_W), lambda b, i: (b, i, 0)),
            pl.BlockSpec((1, t, GDN_W), lambda b, i: (b, i, 0)),
            pl.BlockSpec((1, t, LANES), lambda b, i: (b, i, 0)),
            pl.BlockSpec((1, 2 * GDN_HEADS, t), lambda b, i: (b, 0, i)),
        ],
        out_shape=[f3, f3, f3,
                   jax.ShapeDtypeStruct((bsz, n, LANES), F32),
                   jax.ShapeDtypeStruct((bsz, 2 * GDN_HEADS, n), F32)],
        scratch_shapes=[pltpu.VMEM((t + 2 * SUBLANES, c3), F32)],
        compiler_params=_cparams("parallel", "parallel"),
        name="gdn_prepare",
    )(qkv_g, qkv_g, qkv_g, ab, conv_w, nea_row, dtb_row, cos_t, sin_t)


TRI_BASE = 8


def _slabs(x, s, second):
    start = s if second else 0
    return jnp.concatenate([x[i:i + s] for i in range(start, x.shape[0], 2 * s)], axis=0)


def _interleave(upd, other, s, second):
    parts = []
    for p in range(other.shape[0] // (2 * s)):
        u = upd[s * p:s * (p + 1)]
        if second:
            parts += [other[2 * s * p:2 * s * p + s], u]
        else:
            parts += [u, other[2 * s * p + s:2 * s * (p + 1)]]
    return jnp.concatenate(parts, axis=0)


def _unit_tri_solve(ms, rhss, lower):
    c = ms[0].shape[0]
    n = len(ms)
    row = lax.broadcasted_iota(jnp.int32, (c, c), 0)
    col = lax.broadcasted_iota(jnp.int32, (c, c), 1)
    eye = (row == col).astype(F32)
    same = lambda s: (row // s) == (col // s)
    base = same(TRI_BASE)
    ds = [jnp.where(base, m, 0.0) for m in ms]
    ps = [eye - d for d in ds]
    dbs = [d.astype(BF16) for d in ds]
    qs = [_dot(db, db) for db in dbs]
    n_iter = int(math.log2(TRI_BASE)) - 1
    for it in range(n_iter):
        qbs = [q.astype(BF16) for q in qs]
        if it < n_iter - 1:
            yqs = [_dot(qb, jnp.concatenate([p, q], axis=1).astype(BF16)) for qb, p, q in zip(qbs, ps, qs)]
            ps = [p + yq[:, :c] for p, yq in zip(ps, yqs)]
            qs = [yq[:, c:] for yq in yqs]
        else:
            ys = [_dot(qb, p.astype(BF16)) for qb, p in zip(qbs, ps)]
            ps = [p + y for p, y in zip(ps, ys)]
    ts = ps
    zeros = jnp.zeros((c, c), F32)
    s = TRI_BASE
    while 2 * s < c:
        off_diag = same(2 * s) & jnp.logical_not(same(s))
        l_rows = [_slabs(jnp.where(off_diag, m, 0.0), s, lo).astype(BF16) for m, lo in zip(ms, lower)]
        t_rows = [_slabs(t, s, lo) for t, lo in zip(ts, lower)]
        ys = [_dot(l, t.astype(BF16)) for l, t in zip(l_rows, ts)]
        y_full = [_interleave(y, zeros, s, lo).astype(BF16) for y, lo in zip(ys, lower)]
        zs = [_dot(tr.astype(BF16), yf) for tr, yf in zip(t_rows, y_full)]
        ts = [_interleave(tr - z, t, s, lo) for tr, z, t, lo in zip(t_rows, zs, ts, lower)]
        s *= 2
    hc = c // 2
    off_diag = jnp.logical_not(same(hc))
    ind = [slice(0, hc) if lo else slice(hc, c) for lo in lower]
    dep = [slice(hc, c) if lo else slice(0, hc) for lo in lower]

    def place(x_ind, x_dep, lo):
        return jnp.concatenate([x_ind, x_dep] if lo else [x_dep, x_ind], axis=0)

    x1 = [_dot(t[i].astype(BF16), r.astype(BF16)) for t, i, r in zip(ts, ind, rhss)]
    mixed = [place(x, r[dp], lo).astype(BF16) for x, r, dp, lo in zip(x1, rhss, dep, lower)]
    ys = [_dot(jnp.where(off_diag, m, 0.0)[dp].astype(BF16), mx) for m, dp, mx in zip(ms, dep, mixed)]
    mixed2 = [place(r[i], r[dp] - y, lo).astype(BF16) for r, i, dp, y, lo in zip(rhss, ind, dep, ys, lower)]
    x2 = [_dot(t[dp].astype(BF16), mx) for t, dp, mx in zip(ts, dep, mixed2)]
    return [place(a, b, lo) for a, b, lo in zip(x1, x2, lower)]


def _gdn_scan_kernel(*refs, tb, chunk, with_output):
    (qf, kf, vf, gf, gtf, qb, kb, vb, gb, gtb, s0_ref) = refs[:11]
    if with_output:
        of_ref, ob_ref, s_ref = refs[11:]
        o_refs = (of_ref, ob_ref)
    else:
        sfin_ref, s_ref = refs[11:]
    ins = ((qf, kf, vf, gf, gtf), (qb, kb, vb, gb, gtb))
    j = pl.program_id(1)
    c = chunk
    nch = tb // c
    hd = GDN_HEAD_DIM

    @pl.when(j == 0)
    def _():
        s_ref[...] = s0_ref[0]

    row = lax.broadcasted_iota(jnp.int32, (c, c), 0)
    col = lax.broadcasted_iota(jnp.int32, (c, c), 1)
    incl = (col <= row, col >= row)
    strict = (col < row, col > row)
    nh2 = 2 * GDN_HEADS
    heads = [(d, h) for d in range(2) for h in range(GDN_HEADS)]
    units = [(ci, d, h) for ci in range(nch) for d, h in heads]
    off = lambda ci, d: (ci if d == 0 else nch - 1 - ci) * c
    assert c <= LANES

    gblk = {(ci, d): ins[d][3][0, off(ci, d):off(ci, d) + c, :] for ci in range(nch) for d in range(2)}
    gtblk = {(ci, d): ins[d][4][0, :, off(ci, d):off(ci, d) + c] for ci in range(nch) for d in range(2)}

    def lane_bcast(ci, d, h, o):
        dh = d * GDN_HEADS + h
        return jnp.broadcast_to(gblk[ci, d][:, o + dh:o + dh + 1], (c, LANES))

    g_b = [lane_bcast(ci, d, h, 0) for ci, d, h in units]
    beta = [lane_bcast(ci, d, h, nh2) for ci, d, h in units]
    g_last = [lane_bcast(ci, d, h, 2 * nh2) for ci, d, h in units]
    g_row = [gtblk[ci, d][d * GDN_HEADS + h:d * GDN_HEADS + h + 1, :] for ci, d, h in units]
    ld = lambda idx: [ins[d][idx][0, off(ci, d):off(ci, d) + c, h * hd:(h + 1) * hd] for ci, d, h in units]
    q, k, v = ld(0), ld(1), ld(2)
    decay = [jnp.exp(jnp.where(incl[d], gb_[:, :c] - gr, MASK_VALUE)) for (ci, d, h), gb_, gr in zip(units, g_b, g_row)]
    e_g = [jnp.exp(x) for x in g_b]
    k_beta = [a * b for a, b in zip(k, beta)]
    k16 = [a.astype(BF16) for a in k]
    lhs = [jnp.concatenate([a, b], axis=0) for a, b in zip(k_beta, q)] if with_output else k_beta
    kq = [_dot_nt(a.astype(BF16), b) for a, b in zip(lhs, k16)]
    m = [jnp.where(strict[d], a[:c] * b, 0.0) for (ci, d, h), a, b in zip(units, kq, decay)]
    rhs = [jnp.concatenate([a * b, kb_ * e], axis=1) for a, b, kb_, e in zip(v, beta, k_beta, e_g)]
    uw = _unit_tri_solve(m, rhs, [d == 0 for ci, d, h in units])
    k_tail = [a * jnp.exp(gl - gb_) for a, gl, gb_ in zip(k, g_last, g_b)]
    g_tot = [jnp.exp(gl[0:1, :]) for gl in g_last]

    s = [s_ref[d, h] for d, h in heads]
    nhd = len(heads)
    for ci in range(nch):
        u0 = ci * nhd
        sel = lambda xs_: xs_[u0:u0 + nhd]
        s16 = [a.astype(BF16) for a in s]
        lhs = ([jnp.concatenate([a[:, hd:], b * e], axis=0) for a, b, e in zip(sel(uw), sel(q), sel(e_g))]
               if with_output else [a[:, hd:] for a in sel(uw)])
        xs = [_dot(a.astype(BF16), b) for a, b in zip(lhs, s16)]
        v_new = [a[:, :hd] - b[:c] for a, b in zip(sel(uw), xs)]
        if with_output:
            o_local = [_dot((a[c:] * dcy).astype(BF16), b.astype(BF16))
                       for a, dcy, b in zip(sel(kq), sel(decay), v_new)]
            for (d, h), a, b in zip(heads, xs, o_local):
                o_refs[d][0, off(ci, d):off(ci, d) + c, h * hd:(h + 1) * hd] = a[c:] + b
        ktv = [_dot_tn(a, b) for a, b in zip(sel(k_tail), v_new)]
        s = [a * gt_ + b for a, gt_, b in zip(s, sel(g_tot), ktv)]
    for (d, h), a in zip(heads, s):
        s_ref[d, h] = a

    if not with_output:
        @pl.when(j == pl.num_programs(1) - 1)
        def _():
            sfin_ref[0] = s_ref[...]


def _gdn_scan_call(q, k, v, gbeta, gt, s0, *, tb, chunk, with_output):
    bsz, n, _ = q.shape
    nblk = n // tb
    fwd = lambda b, j: (b, j, 0)
    bwd = lambda b, j: (b, nblk - 1 - j, 0)
    fwd_t = lambda b, j: (b, 0, j)
    bwd_t = lambda b, j: (b, 0, nblk - 1 - j)

    def specs(idx, idx_t):
        return [pl.BlockSpec((1, tb, GDN_W), idx)] * 3 + [pl.BlockSpec((1, tb, LANES), idx),
                                                          pl.BlockSpec((1, 2 * GDN_HEADS, tb), idx_t)]

    state_spec = pl.BlockSpec((1, 2, GDN_HEADS, GDN_HEAD_DIM, GDN_HEAD_DIM), lambda b, j: (b, 0, 0, 0, 0))
    if with_output:
        out_specs = [pl.BlockSpec((1, tb, GDN_W), fwd), pl.BlockSpec((1, tb, GDN_W), bwd)]
        out_shape = [jax.ShapeDtypeStruct((bsz, n, GDN_W), F32)] * 2
    else:
        out_specs = state_spec
        out_shape = jax.ShapeDtypeStruct(s0.shape, F32)
    return pl.pallas_call(
        functools.partial(_gdn_scan_kernel, tb=tb, chunk=chunk, with_output=with_output),
        grid=(bsz, nblk),
        in_specs=specs(fwd, fwd_t) + specs(bwd, bwd_t) + [state_spec],
        out_specs=out_specs,
        out_shape=out_shape,
        scratch_shapes=[pltpu.VMEM((2, GDN_HEADS, GDN_HEAD_DIM, GDN_HEAD_DIM), F32)],
        compiler_params=_cparams("parallel", "arbitrary"),
        name="gdn_scan_latent" if with_output else "gdn_scan_context",
    )(q, k, v, gbeta, gt, q, k, v, gbeta, gt, s0)


def _merge_kernel(x_ref, m_ref, g_ref, b_ref, oa_ref, of_ref, ob_ref, z_ref, gab_ref, nw_ref,
                  wpa_ref, wpb_ref, wo_ref, o_ref, *, alpha):
    o = of_ref[0] + ob_ref[0]
    nw = nw_ref[...]
    parts = []
    for h in range(GDN_HEADS):
        oh = o[:, h * GDN_HEAD_DIM:(h + 1) * GDN_HEAD_DIM]
        parts.append(oh * lax.rsqrt(jnp.mean(oh * oh, axis=-1, keepdims=True) + NORM_EPS) * nw)
    o_b = (jnp.concatenate(parts, axis=1) * _silu(z_ref[0])).astype(BF16)
    d = x_ref.shape[-1]
    gab = gab_ref[0]
    y = (jax.nn.sigmoid(gab[:, :d]) * _dot(oa_ref[0], wpa_ref[...])
         + jax.nn.sigmoid(gab[:, d:]) * _dot(o_b, wpb_ref[...]))
    yo = _dot(y.astype(BF16), wo_ref[...])
    gate = m_ref[0, 5:6, :]
    o_ref[0] = _ln(alpha * x_ref[0] + gate * yo) * g_ref[1:2, :] + b_ref[1:2, :]


def _merge_call(x, mod3, ln_g, ln_b, o_a, o_f, o_b, z, gab, norm_w, w_pa, w_pb, w_o, *, alpha, tm):
    bsz, n, d = x.shape
    tok = lambda w: pl.BlockSpec((1, tm, w), lambda b, i: (b, i, 0))
    full = lambda a: pl.BlockSpec(a.shape, lambda b, i: (0,) * a.ndim)
    return pl.pallas_call(
        functools.partial(_merge_kernel, alpha=alpha),
        grid=(bsz, n // tm),
        in_specs=[tok(d), pl.BlockSpec((1, N_MOD, d), lambda b, i: (b, 0, 0)), full(ln_g), full(ln_b),
                  tok(NA_W), tok(GDN_W), tok(GDN_W), tok(GDN_W), tok(2 * d), full(norm_w),
                  full(w_pa), full(w_pb), full(w_o)],
        out_specs=tok(d),
        out_shape=jax.ShapeDtypeStruct((bsz, n, d), F32),
        compiler_params=_cparams("parallel", "parallel"),
        name="branch_merge",
    )(x, mod3, ln_g, ln_b, o_a, o_f, o_b, z, gab, norm_w, w_pa, w_pb, w_o)


def _rope_tables(n_tok):
    n_freq = GDN_HEAD_DIM // 4
    freqs = ROPE_THETA ** (-jnp.arange(n_freq, dtype=F32) / n_freq)
    t = jnp.arange(n_tok)
    pos = jnp.stack([t // GRID_W, t % GRID_W], axis=-1).astype(F32)
    ang = pos[:, :, None] * freqs
    cos = jnp.cos(ang)
    sin = jnp.sin(ang)
    cos_t = jnp.concatenate([cos, cos], axis=-1).reshape(n_tok, GDN_HEAD_DIM)
    sin_t = jnp.concatenate([-sin, sin], axis=-1).reshape(n_tok, GDN_HEAD_DIM)
    return cos_t, sin_t


def _pick_tile(n, pref):
    t = min(n, pref)
    assert n % t == 0
    return t


def kernel(x, c, ctx, c_ctx, w_ada, b_ada, ln_g, ln_b, ffn1_w_in, ffn1_w_out, w_in, na_rpb, gdn_conv_w,
           gdn_a_log, gdn_dt_bias, gdn_norm_w, w_pa, w_pb, w_o, ffn2_w_in, ffn2_w_out):
    depth = w_ada.shape[0]
    assert depth == 1, "context-update path of non-final layers is not implemented"
    bsz, n, d = x.shape
    lc = ctx.shape[1]
    alpha = (2 * depth) ** 0.25
    layer = 0
    f = ffn1_w_out.shape[1]

    n_rows = -(-(bsz + 1) // SUBLANES) * SUBLANES
    cc = jnp.zeros((n_rows, d), F32).at[:bsz].set(c).at[bsz].set(c_ctx)
    mod3 = _mod_call(cc, w_ada[layer], b_ada[layer], tn=1152).reshape(n_rows, N_MOD, d)

    g_ln, b_ln = ln_g[layer], ln_b[layer]
    tf = 1408 if f % 1408 == 0 else f
    w1u, w1d = ffn1_w_in[layer].astype(BF16), ffn1_w_out[layer].astype(BF16)
    x1 = _ffn_call(x, mod3, g_ln, b_ln, w1u, w1d, sub=0, alpha=alpha, mod_row=None, tm=_pick_tile(n, 512), tf=tf)
    ctx1 = _ffn_call(ctx, mod3, g_ln, b_ln, w1u, w1d, sub=0, alpha=alpha, mod_row=bsz, tm=_pick_tile(lc, 512), tf=tf)

    wi = w_in[layer]
    o0 = 3 * NA_W
    o1 = o0 + 3 * GDN_W
    o2 = o1 + GDN_W
    o3 = o2 + 4 * GDN_HEADS
    w_a = wi[:, :o0].astype(BF16)
    w_g = wi[:, o0:o1].astype(BF16)
    w_z = wi[:, o1:o2].astype(BF16)
    w_ab = jnp.zeros((d, LANES), F32).at[:, :4 * GDN_HEADS].set(wi[:, o2:o3]).astype(BF16)
    w_gab = wi[:, o3:].astype(BF16)
    qkv_a, qkv_g, z, gab, ab = _proj_call(x1, mod3, w_a, w_g, w_z, w_gab, w_ab, mod_row=None, tm=_pick_tile(n, 256))
    qkv_a_c, qkv_g_c, _, _, ab_c = _proj_call(ctx1, mod3, w_a, w_g, w_z, w_gab, w_ab, mod_row=bsz,
                                              tm=_pick_tile(lc, 256))

    o_a = _attn_call(qkv_a, qkv_a_c, _na_bias_table(na_rpb[layer]), tr=8)

    nea = jnp.zeros((1, LANES), F32).at[0, :2 * GDN_HEADS].set(-jnp.exp(gdn_a_log[layer].astype(F32)).reshape(-1))
    dtb = jnp.zeros((1, LANES), F32).at[0, :2 * GDN_HEADS].set(gdn_dt_bias[layer].astype(F32).reshape(-1))
    cw = gdn_conv_w[layer]
    cos_t, sin_t = _rope_tables(n)
    tp = _pick_tile(n, 256)
    tpc = _pick_tile(lc, 256)
    q_l, k_l, v_l, gb_l, gt_l = _gdn_prep_call(qkv_g, ab, cw, nea, dtb, cos_t, sin_t, use_rope=True, t=tp,
                                                chunk=GDN_CHUNK)
    q_c, k_c, v_c, gb_c, gt_c = _gdn_prep_call(qkv_g_c, ab_c, cw, nea, dtb, cos_t[:lc], sin_t[:lc], use_rope=False,
                                                t=tpc, chunk=GDN_CHUNK)
    s0 = jnp.zeros((bsz, 2, GDN_HEADS, GDN_HEAD_DIM, GDN_HEAD_DIM), F32)
    s_ctx = _gdn_scan_call(q_c, k_c, v_c, gb_c, gt_c, s0, tb=tpc, chunk=GDN_CHUNK, with_output=False)
    o_f, o_b = _gdn_scan_call(q_l, k_l, v_l, gb_l, gt_l, s_ctx, tb=tp, chunk=GDN_CHUNK, with_output=True)

    x2 = _merge_call(x1, mod3, g_ln, b_ln, o_a, o_f, o_b, z, gab, gdn_norm_w[layer].reshape(1, GDN_HEAD_DIM),
                     w_pa[layer].astype(BF16), w_pb[layer].astype(BF16), w_o[layer].astype(BF16),
                     alpha=alpha, tm=_pick_tile(n, 256))

    w2u, w2d = ffn2_w_in[layer].astype(BF16), ffn2_w_out[layer].astype(BF16)
    return _ffn_call(x2, mod3, g_ln, b_ln, w2u, w2d, sub=2, alpha=alpha, mod_row=None, tm=_pick_tile(n, 512), tf=tf)
```

```python
import functools
import math

import numpy as np
import jax
import jax.numpy as jnp
from jax import lax
from jax.experimental import pallas as pl
from jax.experimental.pallas import tpu as pltpu

F32 = jnp.float32
BF16 = jnp.bfloat16

GRID_W = 64
NA_HEADS = 8
NA_HEAD_DIM = 64
NA_WIN_ROWS = 8
NA_WIN_COLS = 16
GDN_HEADS = 4
GDN_HEAD_DIM = 128
GDN_CONV = 5
N_MOD = 9
ROPE_THETA = 10000.0
LN_EPS = 1e-6
NORM_EPS = 1e-6
NA_W = NA_HEADS * NA_HEAD_DIM
GDN_W = GDN_HEADS * GDN_HEAD_DIM

LANES = 128
SUBLANES = 8
VMEM_LIMIT_BYTES = 56 * 1024 * 1024
MASK_VALUE = -1e30
GDN_CHUNK = 128


def _cparams(*sem):
    return pltpu.CompilerParams(dimension_semantics=sem, vmem_limit_bytes=VMEM_LIMIT_BYTES)


def _ln(x):
    mu = jnp.mean(x, axis=-1, keepdims=True)
    xc = x - mu
    var = jnp.mean(xc * xc, axis=-1, keepdims=True)
    return xc * lax.rsqrt(var + LN_EPS)


def _silu(x):
    return x * jax.nn.sigmoid(x)


def _dot(a, b):
    return jnp.dot(a, b, preferred_element_type=F32)


def _dot_nt(a, b):
    return lax.dot_general(a, b, (((1,), (1,)), ((), ())), preferred_element_type=F32)


def _dot_tn(a, b):
    return lax.dot_general(a, b, (((0,), (0,)), ((), ())), preferred_element_type=F32)


def _dot_exact_lhs(a_bf16, x):
    hi = x.astype(BF16)
    r1 = x - hi.astype(F32)
    mid = r1.astype(BF16)
    lo = (r1 - mid.astype(F32)).astype(BF16)
    return _dot(a_bf16, hi) + _dot(a_bf16, mid) + _dot(a_bf16, lo)


def _mod_kernel(c_ref, w_ref, b_ref, o_ref):
    s = _silu(c_ref[...]).astype(BF16)
    o_ref[...] = _dot(s, w_ref[...].astype(BF16)) + b_ref[...]


def _mod_call(cc, w_ada, b_ada, *, tn):
    r, d = cc.shape
    n = w_ada.shape[1]
    return pl.pallas_call(
        _mod_kernel,
        grid=(n // tn,),
        in_specs=[
            pl.BlockSpec((r, d), lambda j: (0, 0)),
            pl.BlockSpec((d, tn), lambda j: (0, j)),
            pl.BlockSpec((1, tn), lambda j: (0, j)),
        ],
        out_specs=pl.BlockSpec((r, tn), lambda j: (0, j)),
        out_shape=jax.ShapeDtypeStruct((r, n), F32),
        compiler_params=_cparams("arbitrary"),
        name="adaln_modulation",
    )(cc, w_ada, b_ada.reshape(1, n))


def _ffn_kernel(x_ref, m_ref, g_ref, b_ref, wa_ref, wb_ref, wd_ref, o_ref, *, sub, alpha, n_sub):
    shift = m_ref[0, 3 * sub:3 * sub + 1, :]
    scale = m_ref[0, 3 * sub + 1:3 * sub + 2, :]
    gate = 0.5 * m_ref[0, 3 * sub + 2:3 * sub + 3, :]
    ts = x_ref.shape[1] // n_sub
    rows = [slice(i * ts, (i + 1) * ts) for i in range(n_sub)]
    u = [(_ln(x_ref[0, r, :]) * (1.0 + scale) + shift).astype(BF16) for r in rows]
    a = [_dot(ui, wa_ref[...]) for ui in u]
    b = [_dot(ui, wb_ref[...]) for ui in u]
    h = [(_silu(ai) * bi).astype(BF16) for ai, bi in zip(a, b)]
    y = [_dot(hi, wd_ref[...]) for hi in h]
    for r, yi in zip(rows, y):
        o_ref[0, r, :] = _ln(alpha * x_ref[0, r, :] + gate * yi) * g_ref[sub:sub + 1, :] + b_ref[sub:sub + 1, :]


def _ffn_call(x, mod3, ln_g, ln_b, w_up, w_down, *, sub, alpha, mod_row, tm, n_sub):
    bsz, n, d = x.shape
    f = w_down.shape[0]
    if mod_row is None:
        mod_idx = lambda b, i: (b, 0, 0)
    else:
        mod_idx = lambda b, i: (mod_row, 0, 0)
    once = pl.Buffered(1)
    return pl.pallas_call(
        functools.partial(_ffn_kernel, sub=sub, alpha=alpha, n_sub=n_sub),
        grid=(bsz, n // tm),
        in_specs=[
            pl.BlockSpec((1, tm, d), lambda b, i: (b, i, 0)),
            pl.BlockSpec((1, N_MOD, d), mod_idx),
            pl.BlockSpec(ln_g.shape, lambda b, i: (0, 0)),
            pl.BlockSpec(ln_b.shape, lambda b, i: (0, 0)),
            pl.BlockSpec((d, f), lambda b, i: (0, 0), pipeline_mode=once),
            pl.BlockSpec((d, f), lambda b, i: (0, 1), pipeline_mode=once),
            pl.BlockSpec((f, d), lambda b, i: (0, 0), pipeline_mode=once),
        ],
        out_specs=pl.BlockSpec((1, tm, d), lambda b, i: (b, i, 0)),
        out_shape=jax.ShapeDtypeStruct((bsz, n, d), F32),
        compiler_params=_cparams("parallel", "parallel"),
        name=f"ffn_sublayer_{sub}",
    )(x, mod3, ln_g, ln_b, w_up, w_up, w_down)


def _proj_kernel(x_ref, m_ref, wa_ref, wg_ref, wz_ref, wgab_ref, wab_ref,
                 oa_ref, og_ref, oz_ref, ogab_ref, oab_ref, *, n_sub):
    shift = m_ref[0, 3:4, :]
    scale = m_ref[0, 4:5, :]
    ts = x_ref.shape[1] // n_sub
    rows = [slice(i * ts, (i + 1) * ts) for i in range(n_sub)]
    u = [(_ln(x_ref[0, r, :]) * (1.0 + scale) + shift).astype(BF16) for r in rows]
    for w_ref, o_ref in ((wa_ref, oa_ref), (wg_ref, og_ref), (wz_ref, oz_ref), (wgab_ref, ogab_ref),
                         (wab_ref, oab_ref)):
        for r, ui in zip(rows, u):
            o_ref[0, r, :] = _dot(ui, w_ref[...]).astype(o_ref.dtype)


def _proj_call(x, mod3, w_a, w_g, w_z, w_gab, w_ab, *, mod_row, tm, n_sub):
    bsz, n, d = x.shape
    if mod_row is None:
        mod_idx = lambda b, i: (b, 0, 0)
    else:
        mod_idx = lambda b, i: (mod_row, 0, 0)
    ws = (w_a, w_g, w_z, w_gab, w_ab)
    dts = (BF16, F32, F32, F32, F32)
    return pl.pallas_call(
        functools.partial(_proj_kernel, n_sub=n_sub),
        grid=(bsz, n // tm),
        in_specs=[pl.BlockSpec((1, tm, d), lambda b, i: (b, i, 0)),
                  pl.BlockSpec((1, N_MOD, d), mod_idx)]
                 + [pl.BlockSpec(w.shape, lambda b, i: (0, 0), pipeline_mode=pl.Buffered(1)) for w in ws],
        out_specs=[pl.BlockSpec((1, tm, w.shape[1]), lambda b, i: (b, i, 0)) for w in ws],
        out_shape=[jax.ShapeDtypeStruct((bsz, n, w.shape[1]), dt) for w, dt in zip(ws, dts)],
        compiler_params=_cparams("parallel", "parallel"),
        name="mixer_in_proj",
    )(x, mod3, *ws)


def _attn_kernel(q_ref, k_ref, v_ref, kc_ref, vc_ref, bias_ref, o_ref, *, tr, rows, rpi):
    i = pl.program_id(1)
    n_pairs = NA_HEADS // 2
    pw = 2 * NA_HEAD_DIM
    win = NA_WIN_ROWS * GRID_W
    lane = lax.broadcasted_iota(jnp.int32, (GRID_W, pw), 1)
    first_head = lane < NA_HEAD_DIM
    scale = NA_HEAD_DIM ** -0.5

    def row_body(it, carry):
        chains = [(jr, p) for jr in range(rpi) for p in range(n_pairs)]
        cs = [slice(p * pw, (p + 1) * pw) for jr, p in chains]
        q0, tok0, b0 = [], [], []
        for jr in range(rpi):
            rl = it * rpi + jr
            r = i * tr + rl
            r0 = jnp.clip(r - NA_WIN_ROWS // 2, 0, rows - NA_WIN_ROWS)
            tok0 += [pl.multiple_of(r0 * GRID_W, GRID_W)] * n_pairs
            q0 += [pl.multiple_of(rl * GRID_W, GRID_W)] * n_pairs
            b0 += [r0 - r + NA_WIN_ROWS - 1] * n_pairs
        ids = range(len(chains))
        qs = []
        for c in ids:
            q2 = q_ref[0, pl.ds(q0[c], GRID_W), cs[c]] * scale
            zero = jnp.zeros_like(q2)
            qs.append(jnp.concatenate([jnp.where(first_head, q2, zero), jnp.where(first_head, zero, q2)], axis=0))
        s_loc = [_dot_nt(qs[c], k_ref[0, pl.ds(tok0[c], win), cs[c]]) for c in ids]
        s_ctx = [_dot_nt(qs[c], kc_ref[0, :, cs[c]]) for c in ids]
        s_loc = [s_loc[c] + jnp.concatenate([bias_ref[chains[c][1], b0[c] + 2 * m] for m in range(NA_WIN_ROWS // 2)],
                                            axis=1) for c in ids]
        mx = [jnp.maximum(jnp.max(s_loc[c], axis=-1, keepdims=True), jnp.max(s_ctx[c], axis=-1, keepdims=True))
              for c in ids]
        p_loc = [jnp.exp(s_loc[c] - mx[c]) for c in ids]
        p_ctx = [jnp.exp(s_ctx[c] - mx[c]) for c in ids]
        den = [jnp.sum(p_loc[c], axis=-1, keepdims=True) + jnp.sum(p_ctx[c], axis=-1, keepdims=True) for c in ids]
        o = [_dot(p_loc[c].astype(BF16), v_ref[0, pl.ds(tok0[c], win), cs[c]])
             + _dot(p_ctx[c].astype(BF16), vc_ref[0, :, cs[c]]) for c in ids]
        for c in ids:
            on = o[c] / den[c]
            o_ref[0, pl.ds(q0[c], GRID_W), cs[c]] = jnp.where(first_head, on[:GRID_W], on[GRID_W:]).astype(o_ref.dtype)
        return carry

    lax.fori_loop(0, tr // rpi, row_body, 0)


def _attn_call(qkv, qkv_ctx, bias_tab, *, tr, rpi):
    bsz, n, _ = qkv.shape
    rows = n // GRID_W
    lc = qkv_ctx.shape[1]
    assert rows >= NA_WIN_ROWS and rows % tr == 0 and tr % rpi == 0
    return pl.pallas_call(
        functools.partial(_attn_kernel, tr=tr, rows=rows, rpi=rpi),
        grid=(bsz, rows // tr),
        in_specs=[
            pl.BlockSpec((1, tr * GRID_W, NA_W), lambda b, i: (b, i, 0)),
            pl.BlockSpec((1, n, NA_W), lambda b, i: (b, 0, 1)),
            pl.BlockSpec((1, n, NA_W), lambda b, i: (b, 0, 2)),
            pl.BlockSpec((1, lc, NA_W), lambda b, i: (b, 0, 1)),
            pl.BlockSpec((1, lc, NA_W), lambda b, i: (b, 0, 2)),
            pl.BlockSpec(bias_tab.shape, lambda b, i: (0, 0, 0, 0)),
        ],
        out_specs=pl.BlockSpec((1, tr * GRID_W, NA_W), lambda b, i: (b, i, 0)),
        out_shape=jax.ShapeDtypeStruct((bsz, n, NA_W), BF16),
        compiler_params=_cparams("parallel", "arbitrary"),
        name="neighbourhood_attention",
    )(qkv, qkv, qkv, qkv_ctx, qkv_ctx, bias_tab)


def _na_bias_table(rpb):
    col = np.arange(GRID_W)
    c0 = np.clip(col - NA_WIN_COLS // 2, 0, GRID_W - NA_WIN_COLS)
    rel = col[None, :] - col[:, None] + NA_WIN_COLS - 1
    valid = (col[None, :] >= c0[:, None]) & (col[None, :] < c0[:, None] + NA_WIN_COLS)
    dense = jnp.where(valid, rpb[:, :, np.clip(rel, 0, 2 * NA_WIN_COLS - 2)], MASK_VALUE)
    two = jnp.concatenate([dense[:, :-1], dense[:, 1:]], axis=-1)
    h, nr = two.shape[0], two.shape[1]
    two = two.reshape(h // 2, 2, nr, GRID_W, 2 * GRID_W).transpose(0, 2, 1, 3, 4)
    return two.reshape(h // 2, nr, 2 * GRID_W, 2 * GRID_W).astype(F32)


def _gdn_prep_kernel(x_ref, xp_ref, xn_ref, ab_ref, cw_ref, nea_ref, dtb_ref, cos_ref, sin_ref,
                     q_ref, k_ref, v_ref, gb_ref, gt_ref, xe_ref, *, t, use_rope, chunk):
    i = pl.program_id(1)
    halo = SUBLANES
    pad = GDN_CONV // 2
    prev = jnp.where(i > 0, xp_ref[0], 0.0)
    nxt = jnp.where(i < pl.num_programs(1) - 1, xn_ref[0], 0.0)
    xe_ref[0:halo, :] = prev
    xe_ref[halo:halo + t, :] = x_ref[0]
    xe_ref[halo + t:, :] = nxt
    acc = cw_ref[0:1, :] * xe_ref[halo - pad:halo - pad + t, :]
    for j in range(1, GDN_CONV):
        acc = acc + cw_ref[j:j + 1, :] * xe_ref[halo - pad + j:halo - pad + j + t, :]
    y = _silu(acc)

    lane = lax.broadcasted_iota(jnp.int32, (t, GDN_W), 1)
    half0 = (lane % (GDN_HEAD_DIM // 2)) < (GDN_HEAD_DIM // 4)

    def norm_rope(z):
        parts = []
        for h in range(GDN_HEADS):
            zh = z[:, h * GDN_HEAD_DIM:(h + 1) * GDN_HEAD_DIM]
            parts.append(zh * lax.rsqrt(jnp.sum(zh * zh, axis=-1, keepdims=True) + NORM_EPS))
        z = jnp.concatenate(parts, axis=1)
        if use_rope:
            quarter = GDN_HEAD_DIM // 4
            partner = jnp.where(half0, pltpu.roll(z, GDN_W - quarter, 1), pltpu.roll(z, quarter, 1))
            cos4 = jnp.concatenate([cos_ref[...]] * GDN_HEADS, axis=1)
            sin4 = jnp.concatenate([sin_ref[...]] * GDN_HEADS, axis=1)
            z = z * cos4 + partner * sin4
        return z

    q_ref[0] = norm_rope(y[:, :GDN_W]) * (GDN_HEAD_DIM ** -0.5)
    k_ref[0] = norm_rope(y[:, GDN_W:2 * GDN_W])
    v_ref[0] = y[:, 2 * GDN_W:]

    ab = ab_ref[0]
    nh2 = 2 * GDN_HEADS
    za = ab + dtb_ref[...]
    softplus = jnp.maximum(za, 0.0) + jnp.log(1.0 + jnp.exp(-jnp.abs(za)))
    g = nea_ref[...] * softplus
    beta = jax.nn.sigmoid(ab)
    row = lax.broadcasted_iota(jnp.int32, (t, t), 0)
    colm = lax.broadcasted_iota(jnp.int32, (t, t), 1)
    same = (row // chunk) == (colm // chunk)
    tri_f = (same & (colm <= row)).astype(BF16)
    tri_b = (same & (colm >= row)).astype(BF16)
    lane_g = lax.broadcasted_iota(jnp.int32, (t, LANES), 1)
    gcum = jnp.where(lane_g < GDN_HEADS, _dot_exact_lhs(tri_f, g), _dot_exact_lhs(tri_b, g))
    gtot = pltpu.roll(_dot_exact_lhs(same.astype(BF16), g), 2 * nh2, 1)
    gb_ref[0] = jnp.where(lane_g < nh2, gcum,
                          jnp.where(lane_g < 2 * nh2, beta, jnp.where(lane_g < 3 * nh2, gtot, 0.0)))
    gt_ref[0] = jnp.transpose(jnp.where(lane_g < nh2, gcum, 0.0))[0:nh2, :]


def _gdn_prep_call(qkv_g, ab, conv_w, nea_row, dtb_row, cos_t, sin_t, *, use_rope, t, chunk):
    bsz, n, c3 = qkv_g.shape
    nb8 = n // SUBLANES
    tb8 = t // SUBLANES
    f3 = jax.ShapeDtypeStruct((bsz, n, GDN_W), F32)
    return pl.pallas_call(
        functools.partial(_gdn_prep_kernel, t=t, use_rope=use_rope, chunk=chunk),
        grid=(bsz, n // t),
        in_specs=[
            pl.BlockSpec((1, t, c3), lambda b, i: (b, i, 0)),
            pl.BlockSpec((1, SUBLANES, c3), lambda b, i: (b, jnp.maximum(i * tb8 - 1, 0), 0)),
            pl.BlockSpec((1, SUBLANES, c3), lambda b, i: (b, jnp.minimum((i + 1) * tb8, nb8 - 1), 0)),
            pl.BlockSpec((1, t, LANES), lambda b, i: (b, i, 0)),
            pl.BlockSpec(conv_w.shape, lambda b, i: (0, 0)),
            pl.BlockSpec((1, LANES), lambda b, i: (0, 0)),
            pl.BlockSpec((1, LANES), lambda b, i: (0, 0)),
            pl.BlockSpec((t, GDN_HEAD_DIM), lambda b, i: (i, 0)),
            pl.BlockSpec((t, GDN_HEAD_DIM), lambda b, i: (i, 0)),
        ],
        out_specs=[
            pl.BlockSpec((1, t, GDN_W), lambda b, i: (b, i, 0)),
            pl.BlockSpec((1, t, GDN_W), lambda b, i: (b, i, 0)),
            pl.BlockSpec((1, t, GDN_W), lambda b, i: (b, i, 0)),
            pl.BlockSpec((1, t, LANES), lambda b, i: (b, i, 0)),
            pl.BlockSpec((1, 2 * GDN_HEADS, t), lambda b, i: (b, 0, i)),
        ],
        out_shape=[f3, f3, f3,
                   jax.ShapeDtypeStruct((bsz, n, LANES), F32),
                   jax.ShapeDtypeStruct((bsz, 2 * GDN_HEADS, n), F32)],
        scratch_shapes=[pltpu.VMEM((t + 2 * SUBLANES, c3), F32)],
        compiler_params=_cparams("parallel", "parallel"),
        name="gdn_prepare",
    )(qkv_g, qkv_g, qkv_g, ab, conv_w, nea_row, dtb_row, cos_t, sin_t)


TRI_BASE = 8


def _slabs(x, s, second):
    start = s if second else 0
    return jnp.concatenate([x[i:i + s] for i in range(start, x.shape[0], 2 * s)], axis=0)


def _interleave(upd, other, s, second):
    parts = []
    for p in range(other.shape[0] // (2 * s)):
        u = upd[s * p:s * (p + 1)]
        if second:
            parts += [other[2 * s * p:2 * s * p + s], u]
        else:
            parts += [u, other[2 * s * p + s:2 * s * (p + 1)]]
    return jnp.concatenate(parts, axis=0)


def _unit_tri_solve(ms, rhss, lower):
    c = ms[0].shape[0]
    n = len(ms)
    row = lax.broadcasted_iota(jnp.int32, (c, c), 0)
    col = lax.broadcasted_iota(jnp.int32, (c, c), 1)
    eye = (row == col).astype(F32)
    same = lambda s: (row // s) == (col // s)
    base = same(TRI_BASE)
    ds = [jnp.where(base, m, 0.0) for m in ms]
    ps = [eye - d for d in ds]
    dbs = [d.astype(BF16) for d in ds]
    qs = [_dot(db, db) for db in dbs]
    n_iter = int(math.log2(TRI_BASE)) - 1
    for it in range(n_iter):
        qbs = [q.astype(BF16) for q in qs]
        if it < n_iter - 1:
            yqs = [_dot(qb, jnp.concatenate([p, q], axis=1).astype(BF16)) for qb, p, q in zip(qbs, ps, qs)]
            ps = [p + yq[:, :c] for p, yq in zip(ps, yqs)]
            qs = [yq[:, c:] for yq in yqs]
        else:
            ys = [_dot(qb, p.astype(BF16)) for qb, p in zip(qbs, ps)]
            ps = [p + y for p, y in zip(ps, ys)]
    ts = ps
    zeros = jnp.zeros((c, c), F32)
    s = TRI_BASE
    while 2 * s < c:
        off_diag = same(2 * s) & jnp.logical_not(same(s))
        l_rows = [_slabs(jnp.where(off_diag, m, 0.0), s, lo).astype(BF16) for m, lo in zip(ms, lower)]
        t_rows = [_slabs(t, s, lo) for t, lo in zip(ts, lower)]
        ys = [_dot(l, t.astype(BF16)) for l, t in zip(l_rows, ts)]
        y_full = [_interleave(y, zeros, s, lo).astype(BF16) for y, lo in zip(ys, lower)]
        zs = [_dot(tr.astype(BF16), yf) for tr, yf in zip(t_rows, y_full)]
        ts = [_interleave(tr - z, t, s, lo) for tr, z, t, lo in zip(t_rows, zs, ts, lower)]
        s *= 2
    hc = c // 2
    off_diag = jnp.logical_not(same(hc))
    ind = [slice(0, hc) if lo else slice(hc, c) for lo in lower]
    dep = [slice(hc, c) if lo else slice(0, hc) for lo in lower]

    def place(x_ind, x_dep, lo):
        return jnp.concatenate([x_ind, x_dep] if lo else [x_dep, x_ind], axis=0)

    x1 = [_dot(t[i].astype(BF16), r.astype(BF16)) for t, i, r in zip(ts, ind, rhss)]
    mixed = [place(x, r[dp], lo).astype(BF16) for x, r, dp, lo in zip(x1, rhss, dep, lower)]
    ys = [_dot(jnp.where(off_diag, m, 0.0)[dp].astype(BF16), mx) for m, dp, mx in zip(ms, dep, mixed)]
    mixed2 = [place(r[i], r[dp] - y, lo).astype(BF16) for r, i, dp, y, lo in zip(rhss, ind, dep, ys, lower)]
    x2 = [_dot(t[dp].astype(BF16), mx) for t, dp, mx in zip(ts, dep, mixed2)]
    return [place(a, b, lo) for a, b, lo in zip(x1, x2, lower)]


def _gdn_scan_kernel(*refs, tb, chunk, with_output):
    (qf, kf, vf, gf, gtf, qb, kb, vb, gb, gtb, s0_ref) = refs[:11]
    if with_output:
        of_ref, ob_ref, s_ref = refs[11:]
        o_refs = (of_ref, ob_ref)
    else:
        sfin_ref, s_ref = refs[11:]
    ins = ((qf, kf, vf, gf, gtf), (qb, kb, vb, gb, gtb))
    j = pl.program_id(1)
    c = chunk
    nch = tb // c
    hd = GDN_HEAD_DIM

    @pl.when(j == 0)
    def _():
        s_ref[...] = s0_ref[0]

    row = lax.broadcasted_iota(jnp.int32, (c, c), 0)
    col = lax.broadcasted_iota(jnp.int32, (c, c), 1)
    incl = (col <= row, col >= row)
    strict = (col < row, col > row)
    nh2 = 2 * GDN_HEADS
    heads = [(d, h) for d in range(2) for h in range(GDN_HEADS)]
    units = [(ci, d, h) for ci in range(nch) for d, h in heads]
    off = lambda ci, d: (ci if d == 0 else nch - 1 - ci) * c
    assert c <= LANES

    gblk = {(ci, d): ins[d][3][0, off(ci, d):off(ci, d) + c, :] for ci in range(nch) for d in range(2)}
    gtblk = {(ci, d): ins[d][4][0, :, off(ci, d):off(ci, d) + c] for ci in range(nch) for d in range(2)}

    def lane_bcast(ci, d, h, o):
        dh = d * GDN_HEADS + h
        return jnp.broadcast_to(gblk[ci, d][:, o + dh:o + dh + 1], (c, LANES))

    g_b = [lane_bcast(ci, d, h, 0) for ci, d, h in units]
    beta = [lane_bcast(ci, d, h, nh2) for ci, d, h in units]
    g_last = [lane_bcast(ci, d, h, 2 * nh2) for ci, d, h in units]
    g_row = [gtblk[ci, d][d * GDN_HEADS + h:d * GDN_HEADS + h + 1, :] for ci, d, h in units]
    ld = lambda idx: [ins[d][idx][0, off(ci, d):off(ci, d) + c, h * hd:(h + 1) * hd] for ci, d, h in units]
    q, k, v = ld(0), ld(1), ld(2)
    decay = [jnp.exp(jnp.where(incl[d], gb_[:, :c] - gr, MASK_VALUE)) for (ci, d, h), gb_, gr in zip(units, g_b, g_row)]
    e_g = [jnp.exp(x) for x in g_b]
    k_beta = [a * b for a, b in zip(k, beta)]
    k16 = [a.astype(BF16) for a in k]
    lhs = [jnp.concatenate([a, b], axis=0) for a, b in zip(k_beta, q)] if with_output else k_beta
    kq = [_dot_nt(a.astype(BF16), b) for a, b in zip(lhs, k16)]
    m = [jnp.where(strict[d], a[:c] * b, 0.0) for (ci, d, h), a, b in zip(units, kq, decay)]
    rhs = [jnp.concatenate([a * b, kb_ * e], axis=1) for a, b, kb_, e in zip(v, beta, k_beta, e_g)]
    uw = _unit_tri_solve(m, rhs, [d == 0 for ci, d, h in units])
    k_tail = [a * jnp.exp(gl - gb_) for a, gl, gb_ in zip(k, g_last, g_b)]
    g_tot = [jnp.exp(gl[0:1, :]) for gl in g_last]

    s = [s_ref[d, h] for d, h in heads]
    nhd = len(heads)
    for ci in range(nch):
        u0 = ci * nhd
        sel = lambda xs_: xs_[u0:u0 + nhd]
        s16 = [a.astype(BF16) for a in s]
        lhs = ([jnp.concatenate([a[:, hd:], b * e], axis=0) for a, b, e in zip(sel(uw), sel(q), sel(e_g))]
               if with_output else [a[:, hd:] for a in sel(uw)])
        xs = [_dot(a.astype(BF16), b) for a, b in zip(lhs, s16)]
        v_new = [a[:, :hd] - b[:c] for a, b in zip(sel(uw), xs)]
        if with_output:
            o_local = [_dot((a[c:] * dcy).astype(BF16), b.astype(BF16))
                       for a, dcy, b in zip(sel(kq), sel(decay), v_new)]
            for (d, h), a, b in zip(heads, xs, o_local):
                o_refs[d][0, off(ci, d):off(ci, d) + c, h * hd:(h + 1) * hd] = a[c:] + b
        ktv = [_dot_tn(a, b) for a, b in zip(sel(k_tail), v_new)]
        s = [a * gt_ + b for a, gt_, b in zip(s, sel(g_tot), ktv)]
    for (d, h), a in zip(heads, s):
        s_ref[d, h] = a

    if not with_output:
        @pl.when(j == pl.num_programs(1) - 1)
        def _():
            sfin_ref[0] = s_ref[...]


def _gdn_scan_call(q, k, v, gbeta, gt, s0, *, tb, chunk, with_output):
    bsz, n, _ = q.shape
    nblk = n // tb
    fwd = lambda b, j: (b, j, 0)
    bwd = lambda b, j: (b, nblk - 1 - j, 0)
    fwd_t = lambda b, j: (b, 0, j)
    bwd_t = lambda b, j: (b, 0, nblk - 1 - j)

    def specs(idx, idx_t):
        return [pl.BlockSpec((1, tb, GDN_W), idx)] * 3 + [pl.BlockSpec((1, tb, LANES), idx),
                                                          pl.BlockSpec((1, 2 * GDN_HEADS, tb), idx_t)]

    state_spec = pl.BlockSpec((1, 2, GDN_HEADS, GDN_HEAD_DIM, GDN_HEAD_DIM), lambda b, j: (b, 0, 0, 0, 0))
    if with_output:
        out_specs = [pl.BlockSpec((1, tb, GDN_W), fwd), pl.BlockSpec((1, tb, GDN_W), bwd)]
        out_shape = [jax.ShapeDtypeStruct((bsz, n, GDN_W), F32)] * 2
    else:
        out_specs = state_spec
        out_shape = jax.ShapeDtypeStruct(s0.shape, F32)
    return pl.pallas_call(
        functools.partial(_gdn_scan_kernel, tb=tb, chunk=chunk, with_output=with_output),
        grid=(bsz, nblk),
        in_specs=specs(fwd, fwd_t) + specs(bwd, bwd_t) + [state_spec],
        out_specs=out_specs,
        out_shape=out_shape,
        scratch_shapes=[pltpu.VMEM((2, GDN_HEADS, GDN_HEAD_DIM, GDN_HEAD_DIM), F32)],
        compiler_params=_cparams("parallel", "arbitrary"),
        name="gdn_scan_latent" if with_output else "gdn_scan_context",
    )(q, k, v, gbeta, gt, q, k, v, gbeta, gt, s0)


def _merge_kernel(x_ref, m_ref, g_ref, b_ref, oa_ref, of_ref, ob_ref, z_ref, gab_ref, nw_ref,
                  wpa_ref, wpb_ref, wo_ref, o_ref, *, alpha, n_sub):
    nw = nw_ref[...]
    d = x_ref.shape[-1]
    gate = m_ref[0, 5:6, :]
    ts = x_ref.shape[1] // n_sub
    rows = [slice(i * ts, (i + 1) * ts) for i in range(n_sub)]

    def gdn_out(r):
        o = of_ref[0, r, :] + ob_ref[0, r, :]
        parts = []
        for h in range(GDN_HEADS):
            oh = o[:, h * GDN_HEAD_DIM:(h + 1) * GDN_HEAD_DIM]
            parts.append(oh * lax.rsqrt(jnp.mean(oh * oh, axis=-1, keepdims=True) + NORM_EPS) * nw)
        return (jnp.concatenate(parts, axis=1) * _silu(z_ref[0, r, :])).astype(BF16)

    o_b = [gdn_out(r) for r in rows]
    ya = [_dot(oa_ref[0, r, :], wpa_ref[...]) for r in rows]
    yb = [_dot(ob, wpb_ref[...]) for ob in o_b]
    y = [(jax.nn.sigmoid(gab_ref[0, r, :d]) * a + jax.nn.sigmoid(gab_ref[0, r, d:]) * b).astype(BF16)
         for r, a, b in zip(rows, ya, yb)]
    yo = [_dot(yi, wo_ref[...]) for yi in y]
    for r, yoi in zip(rows, yo):
        o_ref[0, r, :] = _ln(alpha * x_ref[0, r, :] + gate * yoi) * g_ref[1:2, :] + b_ref[1:2, :]


def _merge_call(x, mod3, ln_g, ln_b, o_a, o_f, o_b, z, gab, norm_w, w_pa, w_pb, w_o, *, alpha, tm, n_sub):
    bsz, n, d = x.shape
    tok = lambda w: pl.BlockSpec((1, tm, w), lambda b, i: (b, i, 0))
    full = lambda a: pl.BlockSpec(a.shape, lambda b, i: (0,) * a.ndim)
    return pl.pallas_call(
        functools.partial(_merge_kernel, alpha=alpha, n_sub=n_sub),
        grid=(bsz, n // tm),
        in_specs=[tok(d), pl.BlockSpec((1, N_MOD, d), lambda b, i: (b, 0, 0)), full(ln_g), full(ln_b),
                  tok(NA_W), tok(GDN_W), tok(GDN_W), tok(GDN_W), tok(2 * d), full(norm_w),
                  full(w_pa), full(w_pb), full(w_o)],
        out_specs=tok(d),
        out_shape=jax.ShapeDtypeStruct((bsz, n, d), F32),
        compiler_params=_cparams("parallel", "parallel"),
        name="branch_merge",
    )(x, mod3, ln_g, ln_b, o_a, o_f, o_b, z, gab, norm_w, w_pa, w_pb, w_o)


def _rope_tables(n_tok):
    n_freq = GDN_HEAD_DIM // 4
    freqs = ROPE_THETA ** (-jnp.arange(n_freq, dtype=F32) / n_freq)
    t = jnp.arange(n_tok)
    pos = jnp.stack([t // GRID_W, t % GRID_W], axis=-1).astype(F32)
    ang = pos[:, :, None] * freqs
    cos = jnp.cos(ang)
    sin = jnp.sin(ang)
    cos_t = jnp.concatenate([cos, cos], axis=-1).reshape(n_tok, GDN_HEAD_DIM)
    sin_t = jnp.concatenate([-sin, sin], axis=-1).reshape(n_tok, GDN_HEAD_DIM)
    return cos_t, sin_t


def _pick_tile(n, pref):
    t = min(n, pref)
    assert n % t == 0
    return t


def kernel(x, c, ctx, c_ctx, w_ada, b_ada, ln_g, ln_b, ffn1_w_in, ffn1_w_out, w_in, na_rpb, gdn_conv_w,
           gdn_a_log, gdn_dt_bias, gdn_norm_w, w_pa, w_pb, w_o, ffn2_w_in, ffn2_w_out):
    depth = w_ada.shape[0]
    assert depth == 1, "context-update path of non-final layers is not implemented"
    bsz, n, d = x.shape
    lc = ctx.shape[1]
    alpha = (2 * depth) ** 0.25
    layer = 0
    f = ffn1_w_out.shape[1]

    n_rows = -(-(bsz + 1) // SUBLANES) * SUBLANES
    cc = jnp.zeros((n_rows, d), F32).at[:bsz].set(c).at[bsz].set(c_ctx)
    mod3 = _mod_call(cc, w_ada[layer], b_ada[layer], tn=1152).reshape(n_rows, N_MOD, d)

    g_ln, b_ln = ln_g[layer], ln_b[layer]
    w1u, w1d = ffn1_w_in[layer].astype(BF16), ffn1_w_out[layer].astype(BF16)
    x1 = _ffn_call(x, mod3, g_ln, b_ln, w1u, w1d, sub=0, alpha=alpha, mod_row=None, tm=_pick_tile(n, 512), n_sub=2)
    ctx1 = _ffn_call(ctx, mod3, g_ln, b_ln, w1u, w1d, sub=0, alpha=alpha, mod_row=bsz, tm=_pick_tile(lc, 512),
                     n_sub=2)

    wi = w_in[layer]
    o0 = 3 * NA_W
    o1 = o0 + 3 * GDN_W
    o2 = o1 + GDN_W
    o3 = o2 + 4 * GDN_HEADS
    w_a = wi[:, :o0].astype(BF16)
    w_g = wi[:, o0:o1].astype(BF16)
    w_z = wi[:, o1:o2].astype(BF16)
    w_ab = jnp.zeros((d, LANES), F32).at[:, :4 * GDN_HEADS].set(wi[:, o2:o3]).astype(BF16)
    w_gab = wi[:, o3:].astype(BF16)
    qkv_a, qkv_g, z, gab, ab = _proj_call(x1, mod3, w_a, w_g, w_z, w_gab, w_ab, mod_row=None,
                                          tm=_pick_tile(n, 512), n_sub=2)
    qkv_a_c, qkv_g_c, _, _, ab_c = _proj_call(ctx1, mod3, w_a, w_g, w_z, w_gab, w_ab, mod_row=bsz,
                                              tm=_pick_tile(lc, 512), n_sub=2)

    o_a = _attn_call(qkv_a, qkv_a_c, _na_bias_table(na_rpb[layer]), tr=8, rpi=2)

    nea = jnp.zeros((1, LANES), F32).at[0, :2 * GDN_HEADS].set(-jnp.exp(gdn_a_log[layer].astype(F32)).reshape(-1))
    dtb = jnp.zeros((1, LANES), F32).at[0, :2 * GDN_HEADS].set(gdn_dt_bias[layer].astype(F32).reshape(-1))
    cw = gdn_conv_w[layer]
    cos_t, sin_t = _rope_tables(n)
    tp = _pick_tile(n, 256)
    tpc = _pick_tile(lc, 256)
    q_l, k_l, v_l, gb_l, gt_l = _gdn_prep_call(qkv_g, ab, cw, nea, dtb, cos_t, sin_t, use_rope=True, t=tp,
                                                chunk=GDN_CHUNK)
    q_c, k_c, v_c, gb_c, gt_c = _gdn_prep_call(qkv_g_c, ab_c, cw, nea, dtb, cos_t[:lc], sin_t[:lc], use_rope=False,
                                                t=tpc, chunk=GDN_CHUNK)
    s0 = jnp.zeros((bsz, 2, GDN_HEADS, GDN_HEAD_DIM, GDN_HEAD_DIM), F32)
    s_ctx = _gdn_scan_call(q_c, k_c, v_c, gb_c, gt_c, s0, tb=tpc, chunk=GDN_CHUNK, with_output=False)
    o_f, o_b = _gdn_scan_call(q_l, k_l, v_l, gb_l, gt_l, s_ctx, tb=tp, chunk=GDN_CHUNK, with_output=True)

    x2 = _merge_call(x1, mod3, g_ln, b_ln, o_a, o_f, o_b, z, gab, gdn_norm_w[layer].reshape(1, GDN_HEAD_DIM),
                     w_pa[layer].astype(BF16), w_pb[layer].astype(BF16), w_o[layer].astype(BF16),
                     alpha=alpha, tm=_pick_tile(n, 512), n_sub=2)

    w2u, w2d = ffn2_w_in[layer].astype(BF16), ffn2_w_out[layer].astype(BF16)
    return _ffn_call(x2, mod3, g_ln, b_ln, w2u, w2d, sub=2, alpha=alpha, mod_row=None, tm=_pick_tile(n, 512), n_sub=2)
```

```python
import functools
import math

import numpy as np
import jax
import jax.numpy as jnp
from jax import lax
from jax.experimental import pallas as pl
from jax.experimental.pallas import tpu as pltpu

F32 = jnp.float32
BF16 = jnp.bfloat16

GRID_W = 64
NA_HEADS = 8
NA_HEAD_DIM = 64
NA_WIN_ROWS = 8
NA_WIN_COLS = 16
GDN_HEADS = 4
GDN_HEAD_DIM = 128
GDN_CONV = 5
N_MOD = 9
ROPE_THETA = 10000.0
LN_EPS = 1e-6
NORM_EPS = 1e-6
NA_W = NA_HEADS * NA_HEAD_DIM
GDN_W = GDN_HEADS * GDN_HEAD_DIM

LANES = 128
SUBLANES = 8
VMEM_LIMIT_BYTES = 56 * 1024 * 1024
MASK_VALUE = -1e30
GDN_CHUNK = 128


def _cparams(*sem):
    return pltpu.CompilerParams(dimension_semantics=sem, vmem_limit_bytes=VMEM_LIMIT_BYTES)


def _ln(x):
    mu = jnp.mean(x, axis=-1, keepdims=True)
    xc = x - mu
    var = jnp.mean(xc * xc, axis=-1, keepdims=True)
    return xc * lax.rsqrt(var + LN_EPS)


def _silu(x):
    return x * jax.nn.sigmoid(x)


def _dot(a, b):
    return jnp.dot(a, b, preferred_element_type=F32)


def _dot_nt(a, b):
    return lax.dot_general(a, b, (((1,), (1,)), ((), ())), preferred_element_type=F32)


def _dot_tn(a, b):
    return lax.dot_general(a, b, (((0,), (0,)), ((), ())), preferred_element_type=F32)


def _dot_exact_lhs(a_bf16, x):
    hi = x.astype(BF16)
    r1 = x - hi.astype(F32)
    mid = r1.astype(BF16)
    lo = (r1 - mid.astype(F32)).astype(BF16)
    return _dot(a_bf16, hi) + _dot(a_bf16, mid) + _dot(a_bf16, lo)


def _mod_kernel(c_ref, w_ref, b_ref, o_ref):
    s = _silu(c_ref[...]).astype(BF16)
    o_ref[...] = _dot(s, w_ref[...].astype(BF16)) + b_ref[...]


def _mod_call(cc, w_ada, b_ada, *, tn):
    r, d = cc.shape
    n = w_ada.shape[1]
    return pl.pallas_call(
        _mod_kernel,
        grid=(n // tn,),
        in_specs=[
            pl.BlockSpec((r, d), lambda j: (0, 0)),
            pl.BlockSpec((d, tn), lambda j: (0, j)),
            pl.BlockSpec((1, tn), lambda j: (0, j)),
        ],
        out_specs=pl.BlockSpec((r, tn), lambda j: (0, j)),
        out_shape=jax.ShapeDtypeStruct((r, n), F32),
        compiler_params=_cparams("arbitrary"),
        name="adaln_modulation",
    )(cc, w_ada, b_ada.reshape(1, n))


def _ffn_kernel(x_ref, m_ref, g_ref, b_ref, wa_ref, wb_ref, wd_ref, o_ref, *, sub, alpha, n_sub):
    shift = m_ref[0, 3 * sub:3 * sub + 1, :]
    scale = m_ref[0, 3 * sub + 1:3 * sub + 2, :]
    gate = 0.5 * m_ref[0, 3 * sub + 2:3 * sub + 3, :]
    ts = x_ref.shape[1] // n_sub
    rows = [slice(i * ts, (i + 1) * ts) for i in range(n_sub)]
    u = [(_ln(x_ref[0, r, :]) * (1.0 + scale) + shift).astype(BF16) for r in rows]
    a = [_dot(ui, wa_ref[...]) for ui in u]
    b = [_dot(ui, wb_ref[...]) for ui in u]
    h = [(_silu(ai) * bi).astype(BF16) for ai, bi in zip(a, b)]
    y = [_dot(hi, wd_ref[...]) for hi in h]
    for r, yi in zip(rows, y):
        o_ref[0, r, :] = _ln(alpha * x_ref[0, r, :] + gate * yi) * g_ref[sub:sub + 1, :] + b_ref[sub:sub + 1, :]


def _ffn_call(x, mod3, ln_g, ln_b, w_up, w_down, *, sub, alpha, mod_row, tm, n_sub):
    bsz, n, d = x.shape
    f = w_down.shape[0]
    if mod_row is None:
        mod_idx = lambda b, i: (b, 0, 0)
    else:
        mod_idx = lambda b, i: (mod_row, 0, 0)
    once = pl.Buffered(1)
    return pl.pallas_call(
        functools.partial(_ffn_kernel, sub=sub, alpha=alpha, n_sub=n_sub),
        grid=(bsz, n // tm),
        in_specs=[
            pl.BlockSpec((1, tm, d), lambda b, i: (b, i, 0)),
            pl.BlockSpec((1, N_MOD, d), mod_idx),
            pl.BlockSpec(ln_g.shape, lambda b, i: (0, 0)),
            pl.BlockSpec(ln_b.shape, lambda b, i: (0, 0)),
            pl.BlockSpec((d, f), lambda b, i: (0, 0), pipeline_mode=once),
            pl.BlockSpec((d, f), lambda b, i: (0, 1), pipeline_mode=once),
            pl.BlockSpec((f, d), lambda b, i: (0, 0), pipeline_mode=once),
        ],
        out_specs=pl.BlockSpec((1, tm, d), lambda b, i: (b, i, 0)),
        out_shape=jax.ShapeDtypeStruct((bsz, n, d), F32),
        compiler_params=_cparams("parallel", "parallel"),
        name=f"ffn_sublayer_{sub}",
    )(x, mod3, ln_g, ln_b, w_up, w_up, w_down)


def _proj_kernel(x_ref, m_ref, *refs, n_sub):
    n_out = len(refs) // 2
    shift = m_ref[0, 3:4, :]
    scale = m_ref[0, 4:5, :]
    ts = x_ref.shape[1] // n_sub
    rows = [slice(i * ts, (i + 1) * ts) for i in range(n_sub)]
    u = [(_ln(x_ref[0, r, :]) * (1.0 + scale) + shift).astype(BF16) for r in rows]
    for w_ref, o_ref in zip(refs[:n_out], refs[n_out:]):
        for r, ui in zip(rows, u):
            o_ref[0, r, :] = _dot(ui, w_ref[...]).astype(o_ref.dtype)


def _proj_call(x, mod3, ws, dts, *, mod_row, tm, n_sub):
    bsz, n, d = x.shape
    if mod_row is None:
        mod_idx = lambda b, i: (b, 0, 0)
    else:
        mod_idx = lambda b, i: (mod_row, 0, 0)
    return pl.pallas_call(
        functools.partial(_proj_kernel, n_sub=n_sub),
        grid=(bsz, n // tm),
        in_specs=[pl.BlockSpec((1, tm, d), lambda b, i: (b, i, 0)),
                  pl.BlockSpec((1, N_MOD, d), mod_idx)]
                 + [pl.BlockSpec(w.shape, lambda b, i: (0, 0), pipeline_mode=pl.Buffered(1)) for w in ws],
        out_specs=[pl.BlockSpec((1, tm, w.shape[1]), lambda b, i: (b, i, 0)) for w in ws],
        out_shape=[jax.ShapeDtypeStruct((bsz, n, w.shape[1]), dt) for w, dt in zip(ws, dts)],
        compiler_params=_cparams("parallel", "parallel"),
        name="mixer_in_proj",
    )(x, mod3, *ws)


def _attn_kernel(q_ref, k_ref, v_ref, kc_ref, vc_ref, bias_ref, o_ref, *, tr, rows, rpi):
    i = pl.program_id(1)
    n_pairs = NA_HEADS // 2
    pw = 2 * NA_HEAD_DIM
    win = NA_WIN_ROWS * GRID_W
    lane = lax.broadcasted_iota(jnp.int32, (GRID_W, pw), 1)
    first_head = lane < NA_HEAD_DIM
    scale = NA_HEAD_DIM ** -0.5

    def row_body(it, carry):
        chains = [(jr, p) for jr in range(rpi) for p in range(n_pairs)]
        cs = [slice(p * pw, (p + 1) * pw) for jr, p in chains]
        q0, tok0, b0 = [], [], []
        for jr in range(rpi):
            rl = it * rpi + jr
            r = i * tr + rl
            r0 = jnp.clip(r - NA_WIN_ROWS // 2, 0, rows - NA_WIN_ROWS)
            tok0 += [pl.multiple_of(r0 * GRID_W, GRID_W)] * n_pairs
            q0 += [pl.multiple_of(rl * GRID_W, GRID_W)] * n_pairs
            b0 += [r0 - r + NA_WIN_ROWS - 1] * n_pairs
        ids = range(len(chains))
        qs = []
        for c in ids:
            q2 = q_ref[0, pl.ds(q0[c], GRID_W), cs[c]] * scale
            zero = jnp.zeros_like(q2)
            qs.append(jnp.concatenate([jnp.where(first_head, q2, zero), jnp.where(first_head, zero, q2)], axis=0))
        s_loc = [_dot_nt(qs[c], k_ref[0, pl.ds(tok0[c], win), cs[c]]) for c in ids]
        s_ctx = [_dot_nt(qs[c], kc_ref[0, :, cs[c]]) for c in ids]
        s_loc = [s_loc[c] + jnp.concatenate([bias_ref[b0[c] + 2 * m, chains[c][1]] for m in range(NA_WIN_ROWS // 2)],
                                            axis=1) for c in ids]
        mx = [jnp.maximum(jnp.max(s_loc[c], axis=-1, keepdims=True), jnp.max(s_ctx[c], axis=-1, keepdims=True))
              for c in ids]
        p_loc = [jnp.exp(s_loc[c] - mx[c]) for c in ids]
        p_ctx = [jnp.exp(s_ctx[c] - mx[c]) for c in ids]
        den = [jnp.sum(p_loc[c], axis=-1, keepdims=True) + jnp.sum(p_ctx[c], axis=-1, keepdims=True) for c in ids]
        o = [_dot(p_loc[c].astype(BF16), v_ref[0, pl.ds(tok0[c], win), cs[c]])
             + _dot(p_ctx[c].astype(BF16), vc_ref[0, :, cs[c]]) for c in ids]
        for c in ids:
            on = o[c] / den[c]
            o_ref[0, pl.ds(q0[c], GRID_W), cs[c]] = jnp.where(first_head, on[:GRID_W], on[GRID_W:]).astype(o_ref.dtype)
        return carry

    lax.fori_loop(0, tr // rpi, row_body, 0)


def _attn_call(qkv, kv_ctx, bias_tab, *, tr, rpi):
    bsz, n, _ = qkv.shape
    rows = n // GRID_W
    lc = kv_ctx.shape[1]
    assert rows >= NA_WIN_ROWS and rows % tr == 0 and tr % rpi == 0
    return pl.pallas_call(
        functools.partial(_attn_kernel, tr=tr, rows=rows, rpi=rpi),
        grid=(bsz, rows // tr),
        in_specs=[
            pl.BlockSpec((1, tr * GRID_W, NA_W), lambda b, i: (b, i, 0)),
            pl.BlockSpec((1, n, NA_W), lambda b, i: (b, 0, 1)),
            pl.BlockSpec((1, n, NA_W), lambda b, i: (b, 0, 2)),
            pl.BlockSpec((1, lc, NA_W), lambda b, i: (b, 0, 0)),
            pl.BlockSpec((1, lc, NA_W), lambda b, i: (b, 0, 1)),
            pl.BlockSpec(bias_tab.shape, lambda b, i: (0, 0, 0, 0)),
        ],
        out_specs=pl.BlockSpec((1, tr * GRID_W, NA_W), lambda b, i: (b, i, 0)),
        out_shape=jax.ShapeDtypeStruct((bsz, n, NA_W), BF16),
        compiler_params=_cparams("parallel", "arbitrary"),
        name="neighbourhood_attention",
    )(qkv, qkv, qkv, kv_ctx, kv_ctx, bias_tab)


def _na_bias_table(rpb):
    col = np.arange(GRID_W)
    c0 = np.clip(col - NA_WIN_COLS // 2, 0, GRID_W - NA_WIN_COLS)
    rel = col[None, :] - col[:, None] + NA_WIN_COLS - 1
    valid = (col[None, :] >= c0[:, None]) & (col[None, :] < c0[:, None] + NA_WIN_COLS)
    n_rel = 2 * NA_WIN_COLS - 1
    onehot = ((np.arange(n_rel)[:, None, None] == rel[None]) & valid[None]).astype(np.float32)
    dense = jnp.einsum("rhk,kwc->rhwc", rpb.astype(F32).transpose(1, 0, 2), onehot,
                       precision=lax.Precision.HIGHEST)
    dense = jnp.where(valid, dense, MASK_VALUE)
    two = jnp.concatenate([dense[:-1], dense[1:]], axis=-1)
    nr, h = two.shape[0], two.shape[1]
    return two.reshape(nr, h // 2, 2 * GRID_W, 2 * GRID_W)


def _gdn_prep_kernel(x_ref, xp_ref, xn_ref, ab_ref, cw_ref, nea_ref, dtb_ref, cos_ref, sin_ref, mats_ref,
                     q_ref, k_ref, v_ref, gb_ref, gt_ref, *, t, use_rope):
    i = pl.program_id(1)
    halo = SUBLANES
    pad = GDN_CONV // 2
    hd = GDN_HEAD_DIM
    taps = [j for j in range(GDN_CONV) if j != pad]
    n_tap = len(taps)
    shift_stack = mats_ref[0:n_tap].reshape(n_tap * t, t)
    first = i > 0
    last = i < pl.num_programs(1) - 1
    zeros = jnp.zeros((halo, hd), F32)
    lane = lax.broadcasted_iota(jnp.int32, (t, hd), 1)
    half0 = (lane % (hd // 2)) < (hd // 4)
    quarter = hd // 4
    shifted2 = None
    for cb in range(3 * GDN_HEADS):
        cs = slice(cb * hd, (cb + 1) * hd)
        xb = x_ref[0, :, cs]
        cw = cw_ref[:, cs]
        if cb % 2 == 0:
            shifted2 = _dot(shift_stack, x_ref[0, :, cb * hd:(cb + 2) * hd].astype(BF16))
        shifted = shifted2[:, (cb % 2) * hd:(cb % 2 + 1) * hd]
        acc = cw[pad:pad + 1, :] * xb
        for n, j in enumerate(taps):
            acc = acc + cw[j:j + 1, :] * shifted[n * t:(n + 1) * t]
        top = jnp.concatenate([jnp.where(first, xp_ref[0, :, cs], 0.0), zeros], axis=0)
        bot = jnp.concatenate([zeros, jnp.where(last, xn_ref[0, :, cs], 0.0)], axis=0)
        e_top = sum(cw[j:j + 1, :] * top[halo - pad + j:2 * halo - pad + j, :] for j in range(pad))
        e_bot = sum(cw[j:j + 1, :] * bot[j - pad:j - pad + halo, :] for j in range(pad + 1, GDN_CONV))
        acc = jnp.concatenate([acc[:halo] + e_top, acc[halo:t - halo], acc[t - halo:] + e_bot], axis=0)
        y = _silu(acc)
        which, h = divmod(cb, GDN_HEADS)
        if which < 2:
            y = y * lax.rsqrt(jnp.sum(y * y, axis=-1, keepdims=True) + NORM_EPS)
            if use_rope:
                partner = jnp.where(half0, pltpu.roll(y, hd - quarter, 1), pltpu.roll(y, quarter, 1))
                y = y * cos_ref[...] + partner * sin_ref[...]
        if which == 0:
            y = y * (hd ** -0.5)
        (q_ref, k_ref, v_ref)[which][0, :, h * hd:(h + 1) * hd] = y

    ab = ab_ref[0]
    nh2 = 2 * GDN_HEADS
    za = ab + dtb_ref[...]
    softplus = jnp.maximum(za, 0.0) + jnp.log(1.0 + jnp.exp(-jnp.abs(za)))
    g = nea_ref[...] * softplus
    beta = jax.nn.sigmoid(ab)
    n_tap = GDN_CONV - 1
    tri_f, tri_b, same = mats_ref[n_tap], mats_ref[n_tap + 1], mats_ref[n_tap + 2]
    lane_g = lax.broadcasted_iota(jnp.int32, (t, LANES), 1)
    gcum = jnp.where(lane_g < GDN_HEADS, _dot_exact_lhs(tri_f, g), _dot_exact_lhs(tri_b, g))
    gtot = pltpu.roll(_dot_exact_lhs(same, g), 2 * nh2, 1)
    gb_ref[0] = jnp.where(lane_g < nh2, gcum,
                          jnp.where(lane_g < 2 * nh2, beta, jnp.where(lane_g < 3 * nh2, gtot, 0.0)))
    gt_ref[0] = jnp.transpose(jnp.where(lane_g < nh2, gcum, 0.0))[0:nh2, :]


def _prep_matrices(t, chunk):
    r = np.arange(t)[:, None]
    c = np.arange(t)[None, :]
    pad = GDN_CONV // 2
    shifts = [c == r + (j - pad) for j in range(GDN_CONV) if j != pad]
    same = (r // chunk) == (c // chunk)
    mats = shifts + [same & (c <= r), same & (c >= r), same]
    return jnp.asarray(np.stack(mats).astype(np.float32), dtype=BF16)


def _gdn_prep_call(qkv_g, ab, conv_w, nea_row, dtb_row, cos_t, sin_t, *, use_rope, t, chunk):
    bsz, n, c3 = qkv_g.shape
    nb8 = n // SUBLANES
    tb8 = t // SUBLANES
    f3 = jax.ShapeDtypeStruct((bsz, n, GDN_W), F32)
    mats = _prep_matrices(t, chunk)
    return pl.pallas_call(
        functools.partial(_gdn_prep_kernel, t=t, use_rope=use_rope),
        grid=(bsz, n // t),
        in_specs=[
            pl.BlockSpec((1, t, c3), lambda b, i: (b, i, 0)),
            pl.BlockSpec((1, SUBLANES, c3), lambda b, i: (b, jnp.maximum(i * tb8 - 1, 0), 0)),
            pl.BlockSpec((1, SUBLANES, c3), lambda b, i: (b, jnp.minimum((i + 1) * tb8, nb8 - 1), 0)),
            pl.BlockSpec((1, t, LANES), lambda b, i: (b, i, 0)),
            pl.BlockSpec(conv_w.shape, lambda b, i: (0, 0)),
            pl.BlockSpec((1, LANES), lambda b, i: (0, 0)),
            pl.BlockSpec((1, LANES), lambda b, i: (0, 0)),
            pl.BlockSpec((t, GDN_HEAD_DIM), lambda b, i: (i, 0)),
            pl.BlockSpec((t, GDN_HEAD_DIM), lambda b, i: (i, 0)),
            pl.BlockSpec(mats.shape, lambda b, i: (0, 0, 0)),
        ],
        out_specs=[
            pl.BlockSpec((1, t, GDN_W), lambda b, i: (b, i, 0)),
            pl.BlockSpec((1, t, GDN_W), lambda b, i: (b, i, 0)),
            pl.BlockSpec((1, t, GDN_W), lambda b, i: (b, i, 0)),
            pl.BlockSpec((1, t, LANES), lambda b, i: (b, i, 0)),
            pl.BlockSpec((1, 2 * GDN_HEADS, t), lambda b, i: (b, 0, i)),
        ],
        out_shape=[f3, f3, f3,
                   jax.ShapeDtypeStruct((bsz, n, LANES), F32),
                   jax.ShapeDtypeStruct((bsz, 2 * GDN_HEADS, n), F32)],
        compiler_params=_cparams("parallel", "parallel"),
        name="gdn_prepare",
    )(qkv_g, qkv_g, qkv_g, ab, conv_w, nea_row, dtb_row, cos_t, sin_t, mats)


TRI_BASE = 8


def _slabs(x, s, second):
    start = s if second else 0
    return jnp.concatenate([x[i:i + s] for i in range(start, x.shape[0], 2 * s)], axis=0)


def _interleave(upd, other, s, second):
    parts = []
    for p in range(other.shape[0] // (2 * s)):
        u = upd[s * p:s * (p + 1)]
        if second:
            parts += [other[2 * s * p:2 * s * p + s], u]
        else:
            parts += [u, other[2 * s * p + s:2 * s * (p + 1)]]
    return jnp.concatenate(parts, axis=0)


def _unit_tri_solve(ms, rhss, lower):
    c = ms[0].shape[0]
    n = len(ms)
    row = lax.broadcasted_iota(jnp.int32, (c, c), 0)
    col = lax.broadcasted_iota(jnp.int32, (c, c), 1)
    eye = (row == col).astype(F32)
    same = lambda s: (row // s) == (col // s)
    base = same(TRI_BASE)
    ds = [jnp.where(base, m, 0.0) for m in ms]
    ps = [eye - d for d in ds]
    dbs = [d.astype(BF16) for d in ds]
    qs = [_dot(db, db) for db in dbs]
    n_iter = int(math.log2(TRI_BASE)) - 1
    for it in range(n_iter):
        qbs = [q.astype(BF16) for q in qs]
        if it < n_iter - 1:
            yqs = [_dot(qb, jnp.concatenate([p, q], axis=1).astype(BF16)) for qb, p, q in zip(qbs, ps, qs)]
            ps = [p + yq[:, :c] for p, yq in zip(ps, yqs)]
            qs = [yq[:, c:] for yq in yqs]
        else:
            ys = [_dot(qb, p.astype(BF16)) for qb, p in zip(qbs, ps)]
            ps = [p + y for p, y in zip(ps, ys)]
    ts = ps
    zeros = jnp.zeros((c, c), F32)
    s = TRI_BASE
    while 2 * s < c:
        off_diag = same(2 * s) & jnp.logical_not(same(s))
        l_rows = [_slabs(jnp.where(off_diag, m, 0.0), s, lo).astype(BF16) for m, lo in zip(ms, lower)]
        t_rows = [_slabs(t, s, lo) for t, lo in zip(ts, lower)]
        ys = [_dot(l, t.astype(BF16)) for l, t in zip(l_rows, ts)]
        y_full = [_interleave(y, zeros, s, lo).astype(BF16) for y, lo in zip(ys, lower)]
        zs = [_dot(tr.astype(BF16), yf) for tr, yf in zip(t_rows, y_full)]
        ts = [_interleave(tr - z, t, s, lo) for tr, z, t, lo in zip(t_rows, zs, ts, lower)]
        s *= 2
    hc = c // 2
    off_diag = jnp.logical_not(same(hc))
    ind = [slice(0, hc) if lo else slice(hc, c) for lo in lower]
    dep = [slice(hc, c) if lo else slice(0, hc) for lo in lower]

    def place(x_ind, x_dep, lo):
        return jnp.concatenate([x_ind, x_dep] if lo else [x_dep, x_ind], axis=0)

    x1 = [_dot(t[i].astype(BF16), r.astype(BF16)) for t, i, r in zip(ts, ind, rhss)]
    mixed = [place(x, r[dp], lo).astype(BF16) for x, r, dp, lo in zip(x1, rhss, dep, lower)]
    ys = [_dot(jnp.where(off_diag, m, 0.0)[dp].astype(BF16), mx) for m, dp, mx in zip(ms, dep, mixed)]
    mixed2 = [place(r[i], r[dp] - y, lo).astype(BF16) for r, i, dp, y, lo in zip(rhss, ind, dep, ys, lower)]
    x2 = [_dot(t[dp].astype(BF16), mx) for t, dp, mx in zip(ts, dep, mixed2)]
    return [place(a, b, lo) for a, b, lo in zip(x1, x2, lower)]


def _gdn_scan_kernel(*refs, tb, chunk, with_output):
    (qf, kf, vf, gf, gtf, qb, kb, vb, gb, gtb, s0_ref) = refs[:11]
    if with_output:
        of_ref, ob_ref, s_ref = refs[11:]
        o_refs = (of_ref, ob_ref)
    else:
        sfin_ref, s_ref = refs[11:]
    ins = ((qf, kf, vf, gf, gtf), (qb, kb, vb, gb, gtb))
    j = pl.program_id(1)
    c = chunk
    nch = tb // c
    hd = GDN_HEAD_DIM

    @pl.when(j == 0)
    def _():
        s_ref[...] = s0_ref[0]

    row = lax.broadcasted_iota(jnp.int32, (c, c), 0)
    col = lax.broadcasted_iota(jnp.int32, (c, c), 1)
    incl = (col <= row, col >= row)
    strict = (col < row, col > row)
    nh2 = 2 * GDN_HEADS
    heads = [(d, h) for d in range(2) for h in range(GDN_HEADS)]
    units = [(ci, d, h) for ci in range(nch) for d, h in heads]
    off = lambda ci, d: (ci if d == 0 else nch - 1 - ci) * c
    assert c <= LANES

    gblk = {(ci, d): ins[d][3][0, off(ci, d):off(ci, d) + c, :] for ci in range(nch) for d in range(2)}
    gtblk = {(ci, d): ins[d][4][0, :, off(ci, d):off(ci, d) + c] for ci in range(nch) for d in range(2)}

    def lane_bcast(ci, d, h, o):
        dh = d * GDN_HEADS + h
        return jnp.broadcast_to(gblk[ci, d][:, o + dh:o + dh + 1], (c, LANES))

    g_b = [lane_bcast(ci, d, h, 0) for ci, d, h in units]
    beta = [lane_bcast(ci, d, h, nh2) for ci, d, h in units]
    g_last = [lane_bcast(ci, d, h, 2 * nh2) for ci, d, h in units]
    g_row = [gtblk[ci, d][d * GDN_HEADS + h:d * GDN_HEADS + h + 1, :] for ci, d, h in units]
    ld = lambda idx: [ins[d][idx][0, off(ci, d):off(ci, d) + c, h * hd:(h + 1) * hd] for ci, d, h in units]
    q, k, v = ld(0), ld(1), ld(2)
    decay = [jnp.exp(jnp.where(incl[d], gb_[:, :c] - gr, MASK_VALUE)) for (ci, d, h), gb_, gr in zip(units, g_b, g_row)]
    e_g = [jnp.exp(x) for x in g_b]
    k_beta = [a * b for a, b in zip(k, beta)]
    k16 = [a.astype(BF16) for a in k]
    lhs = [jnp.concatenate([a, b], axis=0) for a, b in zip(k_beta, q)] if with_output else k_beta
    kq = [_dot_nt(a.astype(BF16), b) for a, b in zip(lhs, k16)]
    m = [jnp.where(strict[d], a[:c] * b, 0.0) for (ci, d, h), a, b in zip(units, kq, decay)]
    rhs = [jnp.concatenate([a * b, kb_ * e], axis=1) for a, b, kb_, e in zip(v, beta, k_beta, e_g)]
    uw = _unit_tri_solve(m, rhs, [d == 0 for ci, d, h in units])
    k_tail = [a * jnp.exp(gl - gb_) for a, gl, gb_ in zip(k, g_last, g_b)]
    g_tot = [jnp.exp(gl[0:1, :]) for gl in g_last]

    s = [s_ref[d, h] for d, h in heads]
    nhd = len(heads)
    for ci in range(nch):
        u0 = ci * nhd
        sel = lambda xs_: xs_[u0:u0 + nhd]
        s16 = [a.astype(BF16) for a in s]
        lhs = ([jnp.concatenate([a[:, hd:], b * e], axis=0) for a, b, e in zip(sel(uw), sel(q), sel(e_g))]
               if with_output else [a[:, hd:] for a in sel(uw)])
        xs = [_dot(a.astype(BF16), b) for a, b in zip(lhs, s16)]
        v_new = [a[:, :hd] - b[:c] for a, b in zip(sel(uw), xs)]
        if with_output:
            o_local = [_dot((a[c:] * dcy).astype(BF16), b.astype(BF16))
                       for a, dcy, b in zip(sel(kq), sel(decay), v_new)]
            for (d, h), a, b in zip(heads, xs, o_local):
                o_refs[d][0, off(ci, d):off(ci, d) + c, h * hd:(h + 1) * hd] = a[c:] + b
        ktv = [_dot_tn(a.astype(BF16), b.astype(BF16)) for a, b in zip(sel(k_tail), v_new)]
        s = [a * gt_ + b for a, gt_, b in zip(s, sel(g_tot), ktv)]
    for (d, h), a in zip(heads, s):
        s_ref[d, h] = a

    if not with_output:
        @pl.when(j == pl.num_programs(1) - 1)
        def _():
            sfin_ref[0] = s_ref[...]


def _gdn_scan_call(q, k, v, gbeta, gt, s0, *, tb, chunk, with_output):
    bsz, n, _ = q.shape
    nblk = n // tb
    fwd = lambda b, j: (b, j, 0)
    bwd = lambda b, j: (b, nblk - 1 - j, 0)
    fwd_t = lambda b, j: (b, 0, j)
    bwd_t = lambda b, j: (b, 0, nblk - 1 - j)

    def specs(idx, idx_t):
        return [pl.BlockSpec((1, tb, GDN_W), idx)] * 3 + [pl.BlockSpec((1, tb, LANES), idx),
                                                          pl.BlockSpec((1, 2 * GDN_HEADS, tb), idx_t)]

    state_spec = pl.BlockSpec((1, 2, GDN_HEADS, GDN_HEAD_DIM, GDN_HEAD_DIM), lambda b, j: (b, 0, 0, 0, 0))
    if with_output:
        out_specs = [pl.BlockSpec((1, tb, GDN_W), fwd), pl.BlockSpec((1, tb, GDN_W), bwd)]
        out_shape = [jax.ShapeDtypeStruct((bsz, n, GDN_W), F32)] * 2
    else:
        out_specs = state_spec
        out_shape = jax.ShapeDtypeStruct(s0.shape, F32)
    return pl.pallas_call(
        functools.partial(_gdn_scan_kernel, tb=tb, chunk=chunk, with_output=with_output),
        grid=(bsz, nblk),
        in_specs=specs(fwd, fwd_t) + specs(bwd, bwd_t) + [state_spec],
        out_specs=out_specs,
        out_shape=out_shape,
        scratch_shapes=[pltpu.VMEM((2, GDN_HEADS, GDN_HEAD_DIM, GDN_HEAD_DIM), F32)],
        compiler_params=_cparams("parallel", "arbitrary"),
        name="gdn_scan_latent" if with_output else "gdn_scan_context",
    )(q, k, v, gbeta, gt, q, k, v, gbeta, gt, s0)


def _merge_kernel(x_ref, m_ref, g_ref, b_ref, oa_ref, of_ref, ob_ref, z_ref, gab_ref, nw_ref,
                  wpa_ref, wpb_ref, wo_ref, o_ref, *, alpha, n_sub):
    nw = nw_ref[...]
    d = x_ref.shape[-1]
    gate = m_ref[0, 5:6, :]
    ts = x_ref.shape[1] // n_sub
    rows = [slice(i * ts, (i + 1) * ts) for i in range(n_sub)]

    def gdn_out(r):
        o = of_ref[0, r, :] + ob_ref[0, r, :]
        parts = []
        for h in range(GDN_HEADS):
            oh = o[:, h * GDN_HEAD_DIM:(h + 1) * GDN_HEAD_DIM]
            parts.append(oh * lax.rsqrt(jnp.mean(oh * oh, axis=-1, keepdims=True) + NORM_EPS) * nw)
        return (jnp.concatenate(parts, axis=1) * _silu(z_ref[0, r, :])).astype(BF16)

    o_b = [gdn_out(r) for r in rows]
    ya = [_dot(oa_ref[0, r, :], wpa_ref[...]) for r in rows]
    yb = [_dot(ob, wpb_ref[...]) for ob in o_b]
    y = [(jax.nn.sigmoid(gab_ref[0, r, :d]) * a + jax.nn.sigmoid(gab_ref[0, r, d:]) * b).astype(BF16)
         for r, a, b in zip(rows, ya, yb)]
    yo = [_dot(yi, wo_ref[...]) for yi in y]
    for r, yoi in zip(rows, yo):
        o_ref[0, r, :] = _ln(alpha * x_ref[0, r, :] + gate * yoi) * g_ref[1:2, :] + b_ref[1:2, :]


def _merge_call(x, mod3, ln_g, ln_b, o_a, o_f, o_b, z, gab, norm_w, w_pa, w_pb, w_o, *, alpha, tm, n_sub):
    bsz, n, d = x.shape
    tok = lambda w: pl.BlockSpec((1, tm, w), lambda b, i: (b, i, 0))
    full = lambda a: pl.BlockSpec(a.shape, lambda b, i: (0,) * a.ndim)
    return pl.pallas_call(
        functools.partial(_merge_kernel, alpha=alpha, n_sub=n_sub),
        grid=(bsz, n // tm),
        in_specs=[tok(d), pl.BlockSpec((1, N_MOD, d), lambda b, i: (b, 0, 0)), full(ln_g), full(ln_b),
                  tok(NA_W), tok(GDN_W), tok(GDN_W), tok(GDN_W), tok(2 * d), full(norm_w),
                  full(w_pa), full(w_pb), full(w_o)],
        out_specs=tok(d),
        out_shape=jax.ShapeDtypeStruct((bsz, n, d), F32),
        compiler_params=_cparams("parallel", "parallel"),
        name="branch_merge",
    )(x, mod3, ln_g, ln_b, o_a, o_f, o_b, z, gab, norm_w, w_pa, w_pb, w_o)


def _rope_tables(n_tok):
    n_freq = GDN_HEAD_DIM // 4
    freqs = ROPE_THETA ** (-jnp.arange(n_freq, dtype=F32) / n_freq)
    t = jnp.arange(n_tok)
    pos = jnp.stack([t // GRID_W, t % GRID_W], axis=-1).astype(F32)
    ang = pos[:, :, None] * freqs
    cos = jnp.cos(ang)
    sin = jnp.sin(ang)
    cos_t = jnp.concatenate([cos, cos], axis=-1).reshape(n_tok, GDN_HEAD_DIM)
    sin_t = jnp.concatenate([-sin, sin], axis=-1).reshape(n_tok, GDN_HEAD_DIM)
    return cos_t, sin_t


def _pick_tile(n, pref):
    t = min(n, pref)
    assert n % t == 0
    return t


def kernel(x, c, ctx, c_ctx, w_ada, b_ada, ln_g, ln_b, ffn1_w_in, ffn1_w_out, w_in, na_rpb, gdn_conv_w,
           gdn_a_log, gdn_dt_bias, gdn_norm_w, w_pa, w_pb, w_o, ffn2_w_in, ffn2_w_out):
    depth = w_ada.shape[0]
    assert depth == 1, "context-update path of non-final layers is not implemented"
    bsz, n, d = x.shape
    lc = ctx.shape[1]
    alpha = (2 * depth) ** 0.25
    layer = 0
    f = ffn1_w_out.shape[1]

    n_rows = -(-(bsz + 1) // SUBLANES) * SUBLANES
    cc = jnp.zeros((n_rows, d), F32).at[:bsz].set(c).at[bsz].set(c_ctx)
    mod3 = _mod_call(cc, w_ada[layer], b_ada[layer], tn=1152).reshape(n_rows, N_MOD, d)

    g_ln, b_ln = ln_g[layer], ln_b[layer]
    w1u, w1d = ffn1_w_in[layer].astype(BF16), ffn1_w_out[layer].astype(BF16)
    x1 = _ffn_call(x, mod3, g_ln, b_ln, w1u, w1d, sub=0, alpha=alpha, mod_row=None, tm=_pick_tile(n, 512), n_sub=2)
    ctx1 = _ffn_call(ctx, mod3, g_ln, b_ln, w1u, w1d, sub=0, alpha=alpha, mod_row=bsz, tm=_pick_tile(lc, 512),
                     n_sub=2)

    wi = w_in[layer]
    o0 = 3 * NA_W
    o1 = o0 + 3 * GDN_W
    o2 = o1 + GDN_W
    o3 = o2 + 4 * GDN_HEADS
    w_a = wi[:, :o0].astype(BF16)
    w_g = wi[:, o0:o1].astype(BF16)
    w_z = wi[:, o1:o2].astype(BF16)
    w_ab = jnp.zeros((d, LANES), F32).at[:, :4 * GDN_HEADS].set(wi[:, o2:o3]).astype(BF16)
    w_gab = wi[:, o3:].astype(BF16)
    qkv_a, qkv_g, z, gab, ab = _proj_call(x1, mod3, (w_a, w_g, w_z, w_gab, w_ab), (BF16, F32, F32, F32, F32),
                                          mod_row=None, tm=_pick_tile(n, 512), n_sub=2)
    kv_a_c, qkv_g_c, ab_c = _proj_call(ctx1, mod3, (w_a[:, NA_W:], w_g, w_ab), (BF16, F32, F32),
                                       mod_row=bsz, tm=_pick_tile(lc, 512), n_sub=2)

    o_a = _attn_call(qkv_a, kv_a_c, _na_bias_table(na_rpb[layer]), tr=8, rpi=2)

    nea = jnp.zeros((1, LANES), F32).at[0, :2 * GDN_HEADS].set(-jnp.exp(gdn_a_log[layer].astype(F32)).reshape(-1))
    dtb = jnp.zeros((1, LANES), F32).at[0, :2 * GDN_HEADS].set(gdn_dt_bias[layer].astype(F32).reshape(-1))
    cw = gdn_conv_w[layer]
    cos_t, sin_t = _rope_tables(n)
    tp = _pick_tile(n, 256)
    tpc = _pick_tile(lc, 256)
    q_l, k_l, v_l, gb_l, gt_l = _gdn_prep_call(qkv_g, ab, cw, nea, dtb, cos_t, sin_t, use_rope=True, t=tp,
                                                chunk=GDN_CHUNK)
    q_c, k_c, v_c, gb_c, gt_c = _gdn_prep_call(qkv_g_c, ab_c, cw, nea, dtb, cos_t[:lc], sin_t[:lc], use_rope=False,
                                                t=tpc, chunk=GDN_CHUNK)
    s0 = jnp.zeros((bsz, 2, GDN_HEADS, GDN_HEAD_DIM, GDN_HEAD_DIM), F32)
    s_ctx = _gdn_scan_call(q_c, k_c, v_c, gb_c, gt_c, s0, tb=tpc, chunk=GDN_CHUNK, with_output=False)
    o_f, o_b = _gdn_scan_call(q_l, k_l, v_l, gb_l, gt_l, s_ctx, tb=tp, chunk=GDN_CHUNK, with_output=True)

    x2 = _merge_call(x1, mod3, g_ln, b_ln, o_a, o_f, o_b, z, gab, gdn_norm_w[layer].reshape(1, GDN_HEAD_DIM),
                     w_pa[layer].astype(BF16), w_pb[layer].astype(BF16), w_o[layer].astype(BF16),
                     alpha=alpha, tm=_pick_tile(n, 512), n_sub=2)

    w2u, w2d = ffn2_w_in[layer].astype(BF16), ffn2_w_out[layer].astype(BF16)
    return _ffn_call(x2, mod3, g_ln, b_ln, w2u, w2d, sub=2, alpha=alpha, mod_row=None, tm=_pick_tile(n, 512), n_sub=2)
```

```python
import functools
import math

import numpy as np
import jax
import jax.numpy as jnp
from jax import lax
from jax.experimental import pallas as pl
from jax.experimental.pallas import tpu as pltpu

F32 = jnp.float32
BF16 = jnp.bfloat16

GRID_W = 64
NA_HEADS = 8
NA_HEAD_DIM = 64
NA_WIN_ROWS = 8
NA_WIN_COLS = 16
GDN_HEADS = 4
GDN_HEAD_DIM = 128
GDN_CONV = 5
N_MOD = 9
ROPE_THETA = 10000.0
LN_EPS = 1e-6
NORM_EPS = 1e-6
NA_W = NA_HEADS * NA_HEAD_DIM
GDN_W = GDN_HEADS * GDN_HEAD_DIM

LANES = 128
SUBLANES = 8
VMEM_LIMIT_BYTES = 56 * 1024 * 1024
MASK_VALUE = -1e30
GDN_CHUNK = 128


def _cparams(*sem):
    return pltpu.CompilerParams(dimension_semantics=sem, vmem_limit_bytes=VMEM_LIMIT_BYTES)


def _ln(x):
    mu = jnp.mean(x, axis=-1, keepdims=True)
    xc = x - mu
    var = jnp.mean(xc * xc, axis=-1, keepdims=True)
    return xc * lax.rsqrt(var + LN_EPS)


def _silu(x):
    return x * jax.nn.sigmoid(x)


def _dot(a, b):
    return jnp.dot(a, b, preferred_element_type=F32)


def _dot_nt(a, b):
    return lax.dot_general(a, b, (((1,), (1,)), ((), ())), preferred_element_type=F32)


def _dot_tn(a, b):
    return lax.dot_general(a, b, (((0,), (0,)), ((), ())), preferred_element_type=F32)


def _dot_exact_lhs(a_bf16, x):
    hi = x.astype(BF16)
    r1 = x - hi.astype(F32)
    mid = r1.astype(BF16)
    lo = (r1 - mid.astype(F32)).astype(BF16)
    return _dot(a_bf16, hi) + _dot(a_bf16, mid) + _dot(a_bf16, lo)


def _mod_kernel(c_ref, w_ref, b_ref, o_ref):
    s = _silu(c_ref[...]).astype(BF16)
    o_ref[...] = _dot(s, w_ref[...].astype(BF16)) + b_ref[...]


def _mod_call(cc, w_ada, b_ada, *, tn):
    r, d = cc.shape
    n = w_ada.shape[1]
    return pl.pallas_call(
        _mod_kernel,
        grid=(n // tn,),
        in_specs=[
            pl.BlockSpec((r, d), lambda j: (0, 0)),
            pl.BlockSpec((d, tn), lambda j: (0, j)),
            pl.BlockSpec((1, tn), lambda j: (0, j)),
        ],
        out_specs=pl.BlockSpec((r, tn), lambda j: (0, j)),
        out_shape=jax.ShapeDtypeStruct((r, n), F32),
        compiler_params=_cparams("arbitrary"),
        name="adaln_modulation",
    )(cc, w_ada, b_ada.reshape(1, n))


def _ffn_kernel(x_ref, m_ref, g_ref, b_ref, wa_ref, wb_ref, wd_ref, o_ref, *, sub, alpha, n_sub):
    shift = m_ref[0, 3 * sub:3 * sub + 1, :]
    scale = m_ref[0, 3 * sub + 1:3 * sub + 2, :]
    gate = 0.5 * m_ref[0, 3 * sub + 2:3 * sub + 3, :]
    ts = x_ref.shape[1] // n_sub
    rows = [slice(i * ts, (i + 1) * ts) for i in range(n_sub)]
    u = [(_ln(x_ref[0, r, :]) * (1.0 + scale) + shift).astype(BF16) for r in rows]
    a = [_dot(ui, wa_ref[...]) for ui in u]
    b = [_dot(ui, wb_ref[...]) for ui in u]
    h = [(_silu(ai) * bi).astype(BF16) for ai, bi in zip(a, b)]
    y = [_dot(hi, wd_ref[...]) for hi in h]
    for r, yi in zip(rows, y):
        o_ref[0, r, :] = _ln(alpha * x_ref[0, r, :] + gate * yi) * g_ref[sub:sub + 1, :] + b_ref[sub:sub + 1, :]


def _ffn_call(x, mod3, ln_g, ln_b, w_up, w_down, *, sub, alpha, mod_row, tm, n_sub):
    bsz, n, d = x.shape
    f = w_down.shape[0]
    if mod_row is None:
        mod_idx = lambda b, i: (b, 0, 0)
    else:
        mod_idx = lambda b, i: (mod_row, 0, 0)
    once = pl.Buffered(1)
    return pl.pallas_call(
        functools.partial(_ffn_kernel, sub=sub, alpha=alpha, n_sub=n_sub),
        grid=(bsz, n // tm),
        in_specs=[
            pl.BlockSpec((1, tm, d), lambda b, i: (b, i, 0)),
            pl.BlockSpec((1, N_MOD, d), mod_idx),
            pl.BlockSpec(ln_g.shape, lambda b, i: (0, 0)),
            pl.BlockSpec(ln_b.shape, lambda b, i: (0, 0)),
            pl.BlockSpec((d, f), lambda b, i: (0, 0), pipeline_mode=once),
            pl.BlockSpec((d, f), lambda b, i: (0, 1), pipeline_mode=once),
            pl.BlockSpec((f, d), lambda b, i: (0, 0), pipeline_mode=once),
        ],
        out_specs=pl.BlockSpec((1, tm, d), lambda b, i: (b, i, 0)),
        out_shape=jax.ShapeDtypeStruct((bsz, n, d), F32),
        compiler_params=_cparams("parallel", "parallel"),
        name=f"ffn_sublayer_{sub}",
    )(x, mod3, ln_g, ln_b, w_up, w_up, w_down)


def _proj_kernel(x_ref, m_ref, *refs, n_sub):
    n_out = len(refs) // 2
    shift = m_ref[0, 3:4, :]
    scale = m_ref[0, 4:5, :]
    ts = x_ref.shape[1] // n_sub
    rows = [slice(i * ts, (i + 1) * ts) for i in range(n_sub)]
    u = [(_ln(x_ref[0, r, :]) * (1.0 + scale) + shift).astype(BF16) for r in rows]
    for w_ref, o_ref in zip(refs[:n_out], refs[n_out:]):
        for r, ui in zip(rows, u):
            o_ref[0, r, :] = _dot(ui, w_ref[...]).astype(o_ref.dtype)


def _proj_call(x, mod3, ws, dts, *, mod_row, tm, n_sub):
    bsz, n, d = x.shape
    if mod_row is None:
        mod_idx = lambda b, i: (b, 0, 0)
    else:
        mod_idx = lambda b, i: (mod_row, 0, 0)
    return pl.pallas_call(
        functools.partial(_proj_kernel, n_sub=n_sub),
        grid=(bsz, n // tm),
        in_specs=[pl.BlockSpec((1, tm, d), lambda b, i: (b, i, 0)),
                  pl.BlockSpec((1, N_MOD, d), mod_idx)]
                 + [pl.BlockSpec(w.shape, lambda b, i: (0, 0), pipeline_mode=pl.Buffered(1)) for w in ws],
        out_specs=[pl.BlockSpec((1, tm, w.shape[1]), lambda b, i: (b, i, 0)) for w in ws],
        out_shape=[jax.ShapeDtypeStruct((bsz, n, w.shape[1]), dt) for w, dt in zip(ws, dts)],
        compiler_params=_cparams("parallel", "parallel"),
        name="mixer_in_proj",
    )(x, mod3, *ws)


def _attn_kernel(q_ref, k_ref, v_ref, kc_ref, vc_ref, bias_ref, o_ref, *, tr, rows, rpi):
    i = pl.program_id(1)
    n_pairs = NA_HEADS // 2
    pw = 2 * NA_HEAD_DIM
    win = NA_WIN_ROWS * GRID_W
    lane = lax.broadcasted_iota(jnp.int32, (GRID_W, pw), 1)
    first_head = lane < NA_HEAD_DIM
    scale = NA_HEAD_DIM ** -0.5

    def row_body(it, carry):
        chains = [(jr, p) for jr in range(rpi) for p in range(n_pairs)]
        cs = [slice(p * pw, (p + 1) * pw) for jr, p in chains]
        q0, tok0, b0 = [], [], []
        for jr in range(rpi):
            rl = it * rpi + jr
            r = i * tr + rl
            r0 = jnp.clip(r - NA_WIN_ROWS // 2, 0, rows - NA_WIN_ROWS)
            tok0 += [pl.multiple_of(r0 * GRID_W, GRID_W)] * n_pairs
            q0 += [pl.multiple_of(rl * GRID_W, GRID_W)] * n_pairs
            b0 += [r0 - r + NA_WIN_ROWS - 1] * n_pairs
        ids = range(len(chains))
        qs = []
        for c in ids:
            q2 = q_ref[0, pl.ds(q0[c], GRID_W), cs[c]] * scale
            zero = jnp.zeros_like(q2)
            qs.append(jnp.concatenate([jnp.where(first_head, q2, zero), jnp.where(first_head, zero, q2)], axis=0))
        s_loc = [_dot_nt(qs[c], k_ref[0, pl.ds(tok0[c], win), cs[c]]) for c in ids]
        nq = 2 * GRID_W
        pcs = cs[:n_pairs]

        def per_chain(stacked):
            return [stacked[p][jr * nq:(jr + 1) * nq] for jr, p in chains]

        s_ctx = per_chain([_dot_nt(jnp.concatenate([qs[jr * n_pairs + p] for jr in range(rpi)], axis=0),
                                   kc_ref[0, :, pcs[p]]) for p in range(n_pairs)])
        s_loc = [s_loc[c] + jnp.concatenate([bias_ref[b0[c] + 2 * m, chains[c][1]] for m in range(NA_WIN_ROWS // 2)],
                                            axis=1) for c in ids]
        mx = [jnp.maximum(jnp.max(s_loc[c], axis=-1, keepdims=True), jnp.max(s_ctx[c], axis=-1, keepdims=True))
              for c in ids]
        p_loc = [jnp.exp(s_loc[c] - mx[c]) for c in ids]
        p_ctx = [jnp.exp(s_ctx[c] - mx[c]) for c in ids]
        den = [jnp.sum(p_loc[c], axis=-1, keepdims=True) + jnp.sum(p_ctx[c], axis=-1, keepdims=True) for c in ids]
        o_ctx = per_chain([_dot(jnp.concatenate([p_ctx[jr * n_pairs + p] for jr in range(rpi)], axis=0).astype(BF16),
                                vc_ref[0, :, pcs[p]]) for p in range(n_pairs)])
        o = [_dot(p_loc[c].astype(BF16), v_ref[0, pl.ds(tok0[c], win), cs[c]]) + o_ctx[c] for c in ids]
        for c in ids:
            on = o[c] / den[c]
            o_ref[0, pl.ds(q0[c], GRID_W), cs[c]] = jnp.where(first_head, on[:GRID_W], on[GRID_W:]).astype(o_ref.dtype)
        return carry

    lax.fori_loop(0, tr // rpi, row_body, 0)


def _attn_call(qkv, kv_ctx, bias_tab, *, tr, rpi):
    bsz, n, _ = qkv.shape
    rows = n // GRID_W
    lc = kv_ctx.shape[1]
    assert rows >= NA_WIN_ROWS and rows % tr == 0 and tr % rpi == 0
    return pl.pallas_call(
        functools.partial(_attn_kernel, tr=tr, rows=rows, rpi=rpi),
        grid=(bsz, rows // tr),
        in_specs=[
            pl.BlockSpec((1, tr * GRID_W, NA_W), lambda b, i: (b, i, 0)),
            pl.BlockSpec((1, n, NA_W), lambda b, i: (b, 0, 1)),
            pl.BlockSpec((1, n, NA_W), lambda b, i: (b, 0, 2)),
            pl.BlockSpec((1, lc, NA_W), lambda b, i: (b, 0, 0)),
            pl.BlockSpec((1, lc, NA_W), lambda b, i: (b, 0, 1)),
            pl.BlockSpec(bias_tab.shape, lambda b, i: (0, 0, 0, 0)),
        ],
        out_specs=pl.BlockSpec((1, tr * GRID_W, NA_W), lambda b, i: (b, i, 0)),
        out_shape=jax.ShapeDtypeStruct((bsz, n, NA_W), BF16),
        compiler_params=_cparams("parallel", "arbitrary"),
        name="neighbourhood_attention",
    )(qkv, qkv, qkv, kv_ctx, kv_ctx, bias_tab)


def _na_bias_table(rpb):
    col = np.arange(GRID_W)
    c0 = np.clip(col - NA_WIN_COLS // 2, 0, GRID_W - NA_WIN_COLS)
    rel = col[None, :] - col[:, None] + NA_WIN_COLS - 1
    valid = (col[None, :] >= c0[:, None]) & (col[None, :] < c0[:, None] + NA_WIN_COLS)
    n_rel = 2 * NA_WIN_COLS - 1
    onehot = ((np.arange(n_rel)[:, None, None] == rel[None]) & valid[None]).astype(np.float32)
    dense = jnp.einsum("rhk,kwc->rhwc", rpb.astype(F32).transpose(1, 0, 2), onehot,
                       precision=lax.Precision.HIGHEST)
    dense = jnp.where(valid, dense, MASK_VALUE)
    two = jnp.concatenate([dense[:-1], dense[1:]], axis=-1)
    nr, h = two.shape[0], two.shape[1]
    return two.reshape(nr, h // 2, 2 * GRID_W, 2 * GRID_W)


def _gdn_prep_kernel(x_ref, xp_ref, xn_ref, ab_ref, cw_ref, nea_ref, dtb_ref, cos_ref, sin_ref, mats_ref,
                     q_ref, k_ref, v_ref, gb_ref, gt_ref, *, t, use_rope):
    i = pl.program_id(1)
    halo = SUBLANES
    pad = GDN_CONV // 2
    hd = GDN_HEAD_DIM
    taps = [j for j in range(GDN_CONV) if j != pad]
    n_tap = len(taps)
    shift_stack = mats_ref[0:n_tap].reshape(n_tap * t, t)
    first = i > 0
    last = i < pl.num_programs(1) - 1
    zeros = jnp.zeros((halo, hd), F32)
    lane = lax.broadcasted_iota(jnp.int32, (t, hd), 1)
    half0 = (lane % (hd // 2)) < (hd // 4)
    quarter = hd // 4
    shifted2 = None
    for cb in range(3 * GDN_HEADS):
        cs = slice(cb * hd, (cb + 1) * hd)
        xb = x_ref[0, :, cs]
        cw = cw_ref[:, cs]
        if cb % 2 == 0:
            shifted2 = _dot(shift_stack, x_ref[0, :, cb * hd:(cb + 2) * hd].astype(BF16))
        shifted = shifted2[:, (cb % 2) * hd:(cb % 2 + 1) * hd]
        acc = cw[pad:pad + 1, :] * xb
        for n, j in enumerate(taps):
            acc = acc + cw[j:j + 1, :] * shifted[n * t:(n + 1) * t]
        top = jnp.concatenate([jnp.where(first, xp_ref[0, :, cs], 0.0), zeros], axis=0)
        bot = jnp.concatenate([zeros, jnp.where(last, xn_ref[0, :, cs], 0.0)], axis=0)
        e_top = sum(cw[j:j + 1, :] * top[halo - pad + j:2 * halo - pad + j, :] for j in range(pad))
        e_bot = sum(cw[j:j + 1, :] * bot[j - pad:j - pad + halo, :] for j in range(pad + 1, GDN_CONV))
        acc = jnp.concatenate([acc[:halo] + e_top, acc[halo:t - halo], acc[t - halo:] + e_bot], axis=0)
        y = _silu(acc)
        which, h = divmod(cb, GDN_HEADS)
        if which < 2:
            y = y * lax.rsqrt(jnp.sum(y * y, axis=-1, keepdims=True) + NORM_EPS)
            if use_rope:
                partner = jnp.where(half0, pltpu.roll(y, hd - quarter, 1), pltpu.roll(y, quarter, 1))
                y = y * cos_ref[...] + partner * sin_ref[...]
        if which == 0:
            y = y * (hd ** -0.5)
        (q_ref, k_ref, v_ref)[which][0, :, h * hd:(h + 1) * hd] = y

    ab = ab_ref[0]
    nh2 = 2 * GDN_HEADS
    za = ab + dtb_ref[...]
    softplus = jnp.maximum(za, 0.0) + jnp.log(1.0 + jnp.exp(-jnp.abs(za)))
    g = nea_ref[...] * softplus
    beta = jax.nn.sigmoid(ab)
    n_tap = GDN_CONV - 1
    tri_f, tri_b, same = mats_ref[n_tap], mats_ref[n_tap + 1], mats_ref[n_tap + 2]
    lane_g = lax.broadcasted_iota(jnp.int32, (t, LANES), 1)
    gcum = jnp.where(lane_g < GDN_HEADS, _dot_exact_lhs(tri_f, g), _dot_exact_lhs(tri_b, g))
    gtot = pltpu.roll(_dot_exact_lhs(same, g), 2 * nh2, 1)
    gb_ref[0] = jnp.where(lane_g < nh2, gcum,
                          jnp.where(lane_g < 2 * nh2, beta, jnp.where(lane_g < 3 * nh2, gtot, 0.0)))
    gt_ref[0] = jnp.transpose(jnp.where(lane_g < nh2, gcum, 0.0))[0:nh2, :]


def _prep_matrices(t, chunk):
    r = np.arange(t)[:, None]
    c = np.arange(t)[None, :]
    pad = GDN_CONV // 2
    shifts = [c == r + (j - pad) for j in range(GDN_CONV) if j != pad]
    same = (r // chunk) == (c // chunk)
    mats = shifts + [same & (c <= r), same & (c >= r), same]
    return jnp.asarray(np.stack(mats).astype(np.float32), dtype=BF16)


def _gdn_prep_call(qkv_g, ab, conv_w, nea_row, dtb_row, cos_t, sin_t, *, use_rope, t, chunk):
    bsz, n, c3 = qkv_g.shape
    nb8 = n // SUBLANES
    tb8 = t // SUBLANES
    f3 = jax.ShapeDtypeStruct((bsz, n, GDN_W), F32)
    mats = _prep_matrices(t, chunk)
    return pl.pallas_call(
        functools.partial(_gdn_prep_kernel, t=t, use_rope=use_rope),
        grid=(bsz, n // t),
        in_specs=[
            pl.BlockSpec((1, t, c3), lambda b, i: (b, i, 0)),
            pl.BlockSpec((1, SUBLANES, c3), lambda b, i: (b, jnp.maximum(i * tb8 - 1, 0), 0)),
            pl.BlockSpec((1, SUBLANES, c3), lambda b, i: (b, jnp.minimum((i + 1) * tb8, nb8 - 1), 0)),
            pl.BlockSpec((1, t, LANES), lambda b, i: (b, i, 0)),
            pl.BlockSpec(conv_w.shape, lambda b, i: (0, 0)),
            pl.BlockSpec((1, LANES), lambda b, i: (0, 0)),
            pl.BlockSpec((1, LANES), lambda b, i: (0, 0)),
            pl.BlockSpec((t, GDN_HEAD_DIM), lambda b, i: (i, 0)),
            pl.BlockSpec((t, GDN_HEAD_DIM), lambda b, i: (i, 0)),
            pl.BlockSpec(mats.shape, lambda b, i: (0, 0, 0)),
        ],
        out_specs=[
            pl.BlockSpec((1, t, GDN_W), lambda b, i: (b, i, 0)),
            pl.BlockSpec((1, t, GDN_W), lambda b, i: (b, i, 0)),
            pl.BlockSpec((1, t, GDN_W), lambda b, i: (b, i, 0)),
            pl.BlockSpec((1, t, LANES), lambda b, i: (b, i, 0)),
            pl.BlockSpec((1, 2 * GDN_HEADS, t), lambda b, i: (b, 0, i)),
        ],
        out_shape=[f3, f3, f3,
                   jax.ShapeDtypeStruct((bsz, n, LANES), F32),
                   jax.ShapeDtypeStruct((bsz, 2 * GDN_HEADS, n), F32)],
        compiler_params=_cparams("parallel", "parallel"),
        name="gdn_prepare",
    )(qkv_g, qkv_g, qkv_g, ab, conv_w, nea_row, dtb_row, cos_t, sin_t, mats)


TRI_BASE = 8


def _slabs(x, s, second):
    start = s if second else 0
    return jnp.concatenate([x[i:i + s] for i in range(start, x.shape[0], 2 * s)], axis=0)


def _interleave(upd, other, s, second):
    parts = []
    for p in range(other.shape[0] // (2 * s)):
        u = upd[s * p:s * (p + 1)]
        if second:
            parts += [other[2 * s * p:2 * s * p + s], u]
        else:
            parts += [u, other[2 * s * p + s:2 * s * (p + 1)]]
    return jnp.concatenate(parts, axis=0)


def _unit_tri_solve(ms, rhss, lower):
    c = ms[0].shape[0]
    n = len(ms)
    row = lax.broadcasted_iota(jnp.int32, (c, c), 0)
    col = lax.broadcasted_iota(jnp.int32, (c, c), 1)
    eye = (row == col).astype(F32)
    same = lambda s: (row // s) == (col // s)
    base = same(TRI_BASE)
    ds = [jnp.where(base, m, 0.0) for m in ms]
    ps = [eye - d for d in ds]
    dbs = [d.astype(BF16) for d in ds]
    qs = [_dot(db, db) for db in dbs]
    n_iter = int(math.log2(TRI_BASE)) - 1
    for it in range(n_iter):
        qbs = [q.astype(BF16) for q in qs]
        if it < n_iter - 1:
            yqs = [_dot(qb, jnp.concatenate([p, q], axis=1).astype(BF16)) for qb, p, q in zip(qbs, ps, qs)]
            ps = [p + yq[:, :c] for p, yq in zip(ps, yqs)]
            qs = [yq[:, c:] for yq in yqs]
        else:
            ys = [_dot(qb, p.astype(BF16)) for qb, p in zip(qbs, ps)]
            ps = [p + y for p, y in zip(ps, ys)]
    ts = ps
    zeros = jnp.zeros((c, c), F32)
    s = TRI_BASE
    while 2 * s < c:
        off_diag = same(2 * s) & jnp.logical_not(same(s))
        l_rows = [_slabs(jnp.where(off_diag, m, 0.0), s, lo).astype(BF16) for m, lo in zip(ms, lower)]
        t_rows = [_slabs(t, s, lo) for t, lo in zip(ts, lower)]
        ys = [_dot(l, t.astype(BF16)) for l, t in zip(l_rows, ts)]
        y_full = [_interleave(y, zeros, s, lo).astype(BF16) for y, lo in zip(ys, lower)]
        zs = [_dot(tr.astype(BF16), yf) for tr, yf in zip(t_rows, y_full)]
        ts = [_interleave(tr - z, t, s, lo) for tr, z, t, lo in zip(t_rows, zs, ts, lower)]
        s *= 2
    hc = c // 2
    off_diag = jnp.logical_not(same(hc))
    ind = [slice(0, hc) if lo else slice(hc, c) for lo in lower]
    dep = [slice(hc, c) if lo else slice(0, hc) for lo in lower]

    def place(x_ind, x_dep, lo):
        return jnp.concatenate([x_ind, x_dep] if lo else [x_dep, x_ind], axis=0)

    x1 = [_dot(t[i].astype(BF16), r.astype(BF16)) for t, i, r in zip(ts, ind, rhss)]
    mixed = [place(x, r[dp], lo).astype(BF16) for x, r, dp, lo in zip(x1, rhss, dep, lower)]
    ys = [_dot(jnp.where(off_diag, m, 0.0)[dp].astype(BF16), mx) for m, dp, mx in zip(ms, dep, mixed)]
    mixed2 = [place(r[i], r[dp] - y, lo).astype(BF16) for r, i, dp, y, lo in zip(rhss, ind, dep, ys, lower)]
    x2 = [_dot(t[dp].astype(BF16), mx) for t, dp, mx in zip(ts, dep, mixed2)]
    return [place(a, b, lo) for a, b, lo in zip(x1, x2, lower)]


def _gdn_scan_kernel(*refs, tb, chunk, with_output):
    (qf, kf, vf, gf, gtf, qb, kb, vb, gb, gtb, s0_ref) = refs[:11]
    if with_output:
        of_ref, ob_ref, s_ref = refs[11:]
        o_refs = (of_ref, ob_ref)
    else:
        sfin_ref, s_ref = refs[11:]
    ins = ((qf, kf, vf, gf, gtf), (qb, kb, vb, gb, gtb))
    j = pl.program_id(1)
    c = chunk
    nch = tb // c
    hd = GDN_HEAD_DIM

    @pl.when(j == 0)
    def _():
        s_ref[...] = s0_ref[0]

    row = lax.broadcasted_iota(jnp.int32, (c, c), 0)
    col = lax.broadcasted_iota(jnp.int32, (c, c), 1)
    incl = (col <= row, col >= row)
    strict = (col < row, col > row)
    nh2 = 2 * GDN_HEADS
    heads = [(d, h) for d in range(2) for h in range(GDN_HEADS)]
    units = [(ci, d, h) for ci in range(nch) for d, h in heads]
    off = lambda ci, d: (ci if d == 0 else nch - 1 - ci) * c
    assert c <= LANES

    gblk = {(ci, d): ins[d][3][0, off(ci, d):off(ci, d) + c, :] for ci in range(nch) for d in range(2)}
    gtblk = {(ci, d): ins[d][4][0, :, off(ci, d):off(ci, d) + c] for ci in range(nch) for d in range(2)}

    def lane_bcast(ci, d, h, o):
        dh = d * GDN_HEADS + h
        return jnp.broadcast_to(gblk[ci, d][:, o + dh:o + dh + 1], (c, LANES))

    g_b = [lane_bcast(ci, d, h, 0) for ci, d, h in units]
    beta = [lane_bcast(ci, d, h, nh2) for ci, d, h in units]
    g_last = [lane_bcast(ci, d, h, 2 * nh2) for ci, d, h in units]
    g_row = [gtblk[ci, d][d * GDN_HEADS + h:d * GDN_HEADS + h + 1, :] for ci, d, h in units]
    ld = lambda idx: [ins[d][idx][0, off(ci, d):off(ci, d) + c, h * hd:(h + 1) * hd] for ci, d, h in units]
    q, k, v = ld(0), ld(1), ld(2)
    decay = [jnp.exp(jnp.where(incl[d], gb_[:, :c] - gr, MASK_VALUE)) for (ci, d, h), gb_, gr in zip(units, g_b, g_row)]
    e_g = [jnp.exp(x) for x in g_b]
    k_beta = [a * b for a, b in zip(k, beta)]
    k16 = [a.astype(BF16) for a in k]
    lhs = [jnp.concatenate([a, b], axis=0) for a, b in zip(k_beta, q)] if with_output else k_beta
    kq = [_dot_nt(a.astype(BF16), b) for a, b in zip(lhs, k16)]
    m = [jnp.where(strict[d], a[:c] * b, 0.0) for (ci, d, h), a, b in zip(units, kq, decay)]
    rhs = [jnp.concatenate([a * b, kb_ * e], axis=1) for a, b, kb_, e in zip(v, beta, k_beta, e_g)]
    uw = _unit_tri_solve(m, rhs, [d == 0 for ci, d, h in units])
    k_tail = [a * jnp.exp(gl - gb_) for a, gl, gb_ in zip(k, g_last, g_b)]
    g_tot = [jnp.exp(gl[0:1, :]) for gl in g_last]

    s = [s_ref[d, h] for d, h in heads]
    nhd = len(heads)
    for ci in range(nch):
        u0 = ci * nhd
        sel = lambda xs_: xs_[u0:u0 + nhd]
        s16 = [a.astype(BF16) for a in s]
        lhs = ([jnp.concatenate([a[:, hd:], b * e], axis=0) for a, b, e in zip(sel(uw), sel(q), sel(e_g))]
               if with_output else [a[:, hd:] for a in sel(uw)])
        xs = [_dot(a.astype(BF16), b) for a, b in zip(lhs, s16)]
        v_new = [a[:, :hd] - b[:c] for a, b in zip(sel(uw), xs)]
        if with_output:
            o_local = [_dot((a[c:] * dcy).astype(BF16), b.astype(BF16))
                       for a, dcy, b in zip(sel(kq), sel(decay), v_new)]
            for (d, h), a, b in zip(heads, xs, o_local):
                o_refs[d][0, off(ci, d):off(ci, d) + c, h * hd:(h + 1) * hd] = (a[c:] + b).astype(o_refs[d].dtype)
        ktv = [_dot_tn(a.astype(BF16), b.astype(BF16)) for a, b in zip(sel(k_tail), v_new)]
        s = [a * gt_ + b for a, gt_, b in zip(s, sel(g_tot), ktv)]
    for (d, h), a in zip(heads, s):
        s_ref[d, h] = a

    if not with_output:
        @pl.when(j == pl.num_programs(1) - 1)
        def _():
            sfin_ref[0] = s_ref[...]


def _gdn_scan_call(q, k, v, gbeta, gt, s0, *, tb, chunk, with_output):
    bsz, n, _ = q.shape
    nblk = n // tb
    fwd = lambda b, j: (b, j, 0)
    bwd = lambda b, j: (b, nblk - 1 - j, 0)
    fwd_t = lambda b, j: (b, 0, j)
    bwd_t = lambda b, j: (b, 0, nblk - 1 - j)

    def specs(idx, idx_t):
        return [pl.BlockSpec((1, tb, GDN_W), idx)] * 3 + [pl.BlockSpec((1, tb, LANES), idx),
                                                          pl.BlockSpec((1, 2 * GDN_HEADS, tb), idx_t)]

    state_spec = pl.BlockSpec((1, 2, GDN_HEADS, GDN_HEAD_DIM, GDN_HEAD_DIM), lambda b, j: (b, 0, 0, 0, 0))
    if with_output:
        out_specs = [pl.BlockSpec((1, tb, GDN_W), fwd), pl.BlockSpec((1, tb, GDN_W), bwd)]
        out_shape = [jax.ShapeDtypeStruct((bsz, n, GDN_W), BF16)] * 2
    else:
        out_specs = state_spec
        out_shape = jax.ShapeDtypeStruct(s0.shape, F32)
    return pl.pallas_call(
        functools.partial(_gdn_scan_kernel, tb=tb, chunk=chunk, with_output=with_output),
        grid=(bsz, nblk),
        in_specs=specs(fwd, fwd_t) + specs(bwd, bwd_t) + [state_spec],
        out_specs=out_specs,
        out_shape=out_shape,
        scratch_shapes=[pltpu.VMEM((2, GDN_HEADS, GDN_HEAD_DIM, GDN_HEAD_DIM), F32)],
        compiler_params=_cparams("parallel", "arbitrary"),
        name="gdn_scan_latent" if with_output else "gdn_scan_context",
    )(q, k, v, gbeta, gt, q, k, v, gbeta, gt, s0)


def _merge_kernel(x_ref, m_ref, g_ref, b_ref, oa_ref, of_ref, ob_ref, z_ref, gab_ref, nw_ref,
                  wpa_ref, wpb_ref, wo_ref, o_ref, *, alpha, n_sub):
    nw = nw_ref[...]
    d = x_ref.shape[-1]
    gate = m_ref[0, 5:6, :]
    ts = x_ref.shape[1] // n_sub
    rows = [slice(i * ts, (i + 1) * ts) for i in range(n_sub)]

    def gdn_out(r):
        o = of_ref[0, r, :].astype(F32) + ob_ref[0, r, :].astype(F32)
        parts = []
        for h in range(GDN_HEADS):
            oh = o[:, h * GDN_HEAD_DIM:(h + 1) * GDN_HEAD_DIM]
            parts.append(oh * lax.rsqrt(jnp.mean(oh * oh, axis=-1, keepdims=True) + NORM_EPS) * nw)
        return (jnp.concatenate(parts, axis=1) * _silu(z_ref[0, r, :].astype(F32))).astype(BF16)

    o_b = [gdn_out(r) for r in rows]
    ya = [_dot(oa_ref[0, r, :], wpa_ref[...]) for r in rows]
    yb = [_dot(ob, wpb_ref[...]) for ob in o_b]
    y = [(jax.nn.sigmoid(gab_ref[0, r, :d].astype(F32)) * a
          + jax.nn.sigmoid(gab_ref[0, r, d:].astype(F32)) * b).astype(BF16) for r, a, b in zip(rows, ya, yb)]
    yo = [_dot(yi, wo_ref[...]) for yi in y]
    for r, yoi in zip(rows, yo):
        o_ref[0, r, :] = _ln(alpha * x_ref[0, r, :] + gate * yoi) * g_ref[1:2, :] + b_ref[1:2, :]


def _merge_call(x, mod3, ln_g, ln_b, o_a, o_f, o_b, z, gab, norm_w, w_pa, w_pb, w_o, *, alpha, tm, n_sub):
    bsz, n, d = x.shape
    tok = lambda w: pl.BlockSpec((1, tm, w), lambda b, i: (b, i, 0))
    full = lambda a: pl.BlockSpec(a.shape, lambda b, i: (0,) * a.ndim)
    return pl.pallas_call(
        functools.partial(_merge_kernel, alpha=alpha, n_sub=n_sub),
        grid=(bsz, n // tm),
        in_specs=[tok(d), pl.BlockSpec((1, N_MOD, d), lambda b, i: (b, 0, 0)), full(ln_g), full(ln_b),
                  tok(NA_W), tok(GDN_W), tok(GDN_W), tok(GDN_W), tok(2 * d), full(norm_w),
                  full(w_pa), full(w_pb), full(w_o)],
        out_specs=tok(d),
        out_shape=jax.ShapeDtypeStruct((bsz, n, d), F32),
        compiler_params=_cparams("parallel", "parallel"),
        name="branch_merge",
    )(x, mod3, ln_g, ln_b, o_a, o_f, o_b, z, gab, norm_w, w_pa, w_pb, w_o)


def _rope_tables(n_tok):
    n_freq = GDN_HEAD_DIM // 4
    freqs = ROPE_THETA ** (-jnp.arange(n_freq, dtype=F32) / n_freq)
    t = jnp.arange(n_tok)
    pos = jnp.stack([t // GRID_W, t % GRID_W], axis=-1).astype(F32)
    ang = pos[:, :, None] * freqs
    cos = jnp.cos(ang)
    sin = jnp.sin(ang)
    cos_t = jnp.concatenate([cos, cos], axis=-1).reshape(n_tok, GDN_HEAD_DIM)
    sin_t = jnp.concatenate([-sin, sin], axis=-1).reshape(n_tok, GDN_HEAD_DIM)
    return cos_t, sin_t


def _pick_tile(n, pref):
    t = min(n, pref)
    assert n % t == 0
    return t


def kernel(x, c, ctx, c_ctx, w_ada, b_ada, ln_g, ln_b, ffn1_w_in, ffn1_w_out, w_in, na_rpb, gdn_conv_w,
           gdn_a_log, gdn_dt_bias, gdn_norm_w, w_pa, w_pb, w_o, ffn2_w_in, ffn2_w_out):
    depth = w_ada.shape[0]
    assert depth == 1, "context-update path of non-final layers is not implemented"
    bsz, n, d = x.shape
    lc = ctx.shape[1]
    alpha = (2 * depth) ** 0.25
    layer = 0
    f = ffn1_w_out.shape[1]

    n_rows = -(-(bsz + 1) // SUBLANES) * SUBLANES
    cc = jnp.zeros((n_rows, d), F32).at[:bsz].set(c).at[bsz].set(c_ctx)
    mod3 = _mod_call(cc, w_ada[layer], b_ada[layer], tn=1152).reshape(n_rows, N_MOD, d)

    g_ln, b_ln = ln_g[layer], ln_b[layer]
    w1u, w1d = ffn1_w_in[layer].astype(BF16), ffn1_w_out[layer].astype(BF16)
    ffn_tiles = dict(tm=_pick_tile(n, 512), n_sub=2)
    x1 = _ffn_call(x, mod3, g_ln, b_ln, w1u, w1d, sub=0, alpha=alpha, mod_row=None, **ffn_tiles)
    ctx_flat = ctx.reshape(1, bsz * lc, d)
    ctx1 = _ffn_call(ctx_flat, mod3, g_ln, b_ln, w1u, w1d, sub=0, alpha=alpha, mod_row=bsz,
                     tm=_pick_tile(bsz * lc, 512), n_sub=2)

    wi = w_in[layer]
    o0 = 3 * NA_W
    o1 = o0 + 3 * GDN_W
    o2 = o1 + GDN_W
    o3 = o2 + 4 * GDN_HEADS
    w_a = wi[:, :o0].astype(BF16)
    w_g = wi[:, o0:o1].astype(BF16)
    w_z = wi[:, o1:o2].astype(BF16)
    w_ab = jnp.zeros((d, LANES), F32).at[:, :4 * GDN_HEADS].set(wi[:, o2:o3]).astype(BF16)
    w_gab = wi[:, o3:].astype(BF16)
    qkv_a, qkv_g, z, gab, ab = _proj_call(x1, mod3, (w_a, w_g, w_z, w_gab, w_ab), (BF16, F32, BF16, BF16, F32),
                                          mod_row=None, tm=_pick_tile(n, 512), n_sub=2)
    kv_a_c, qkv_g_c, ab_c = (t.reshape(bsz, lc, t.shape[-1]) for t in _proj_call(
        ctx1, mod3, (w_a[:, NA_W:], w_g, w_ab), (BF16, F32, F32), mod_row=bsz, tm=_pick_tile(bsz * lc, 512), n_sub=2))

    o_a = _attn_call(qkv_a, kv_a_c, _na_bias_table(na_rpb[layer]), tr=8, rpi=4)

    nea = jnp.zeros((1, LANES), F32).at[0, :2 * GDN_HEADS].set(-jnp.exp(gdn_a_log[layer].astype(F32)).reshape(-1))
    dtb = jnp.zeros((1, LANES), F32).at[0, :2 * GDN_HEADS].set(gdn_dt_bias[layer].astype(F32).reshape(-1))
    cw = gdn_conv_w[layer]
    cos_t, sin_t = _rope_tables(n)
    tp = _pick_tile(n, 256)
    tpc = _pick_tile(lc, 256)
    q_l, k_l, v_l, gb_l, gt_l = _gdn_prep_call(qkv_g, ab, cw, nea, dtb, cos_t, sin_t, use_rope=True, t=tp,
                                                chunk=GDN_CHUNK)
    q_c, k_c, v_c, gb_c, gt_c = _gdn_prep_call(qkv_g_c, ab_c, cw, nea, dtb, cos_t[:lc], sin_t[:lc], use_rope=False,
                                                t=tpc, chunk=GDN_CHUNK)
    s0 = jnp.zeros((bsz, 2, GDN_HEADS, GDN_HEAD_DIM, GDN_HEAD_DIM), F32)
    s_ctx = _gdn_scan_call(q_c, k_c, v_c, gb_c, gt_c, s0, tb=tpc, chunk=GDN_CHUNK, with_output=False)
    o_f, o_b = _gdn_scan_call(q_l, k_l, v_l, gb_l, gt_l, s_ctx, tb=tp, chunk=GDN_CHUNK, with_output=True)

    x2 = _merge_call(x1, mod3, g_ln, b_ln, o_a, o_f, o_b, z, gab, gdn_norm_w[layer].reshape(1, GDN_HEAD_DIM),
                     w_pa[layer].astype(BF16), w_pb[layer].astype(BF16), w_o[layer].astype(BF16),
                     alpha=alpha, tm=_pick_tile(n, 512), n_sub=2)

    w2u, w2d = ffn2_w_in[layer].astype(BF16), ffn2_w_out[layer].astype(BF16)
    return _ffn_call(x2, mod3, g_ln, b_ln, w2u, w2d, sub=2, alpha=alpha, mod_row=None, **ffn_tiles)
```

```python
import functools
import math

import numpy as np
import jax
import jax.numpy as jnp
from jax import lax
from jax.experimental import pallas as pl
from jax.experimental.pallas import tpu as pltpu

F32 = jnp.float32
BF16 = jnp.bfloat16

GRID_W = 64
NA_HEADS = 8
NA_HEAD_DIM = 64
NA_WIN_ROWS = 8
NA_WIN_COLS = 16
GDN_HEADS = 4
GDN_HEAD_DIM = 128
GDN_CONV = 5
N_MOD = 9
ROPE_THETA = 10000.0
LN_EPS = 1e-6
NORM_EPS = 1e-6
NA_W = NA_HEADS * NA_HEAD_DIM
GDN_W = GDN_HEADS * GDN_HEAD_DIM

LANES = 128
SUBLANES = 8
VMEM_LIMIT_BYTES = 56 * 1024 * 1024
MASK_VALUE = -1e30
GDN_CHUNK = 128

DENSE_TOKENS = 512
DENSE_SUBTILES = 2
GDN_TOKENS = 2 * GDN_CHUNK
ATTN_ROWS = 8
ATTN_ROWS_PER_ITER = 4
MOD_COLS = 9 * LANES


def _cparams(*sem):
    return pltpu.CompilerParams(dimension_semantics=sem, vmem_limit_bytes=VMEM_LIMIT_BYTES)


def _ln(x):
    mu = jnp.mean(x, axis=-1, keepdims=True)
    xc = x - mu
    var = jnp.mean(xc * xc, axis=-1, keepdims=True)
    return xc * lax.rsqrt(var + LN_EPS)


def _sigmoid(x):
    return 0.5 * jnp.tanh(0.5 * x) + 0.5


def _silu(x):
    h = 0.5 * x
    return h + h * jnp.tanh(h)


def _dot(a, b):
    return jnp.dot(a, b, preferred_element_type=F32)


def _dot_nt(a, b):
    return lax.dot_general(a, b, (((1,), (1,)), ((), ())), preferred_element_type=F32)


def _dot_tn(a, b):
    return lax.dot_general(a, b, (((0,), (0,)), ((), ())), preferred_element_type=F32)


def _dot_exact_lhs(a_bf16, x):
    hi = x.astype(BF16)
    r1 = x - hi.astype(F32)
    mid = r1.astype(BF16)
    lo = (r1 - mid.astype(F32)).astype(BF16)
    return _dot(a_bf16, hi) + _dot(a_bf16, mid) + _dot(a_bf16, lo)


def _mod_kernel(c_ref, w_ref, b_ref, o_ref):
    s = _silu(c_ref[...]).astype(BF16)
    o_ref[...] = _dot(s, w_ref[...].astype(BF16)) + b_ref[...]


def _mod_call(cc, w_ada, b_ada, *, tn):
    r, d = cc.shape
    n = w_ada.shape[1]
    return pl.pallas_call(
        _mod_kernel,
        grid=(n // tn,),
        in_specs=[
            pl.BlockSpec((r, d), lambda j: (0, 0)),
            pl.BlockSpec((d, tn), lambda j: (0, j)),
            pl.BlockSpec((1, tn), lambda j: (0, j)),
        ],
        out_specs=pl.BlockSpec((r, tn), lambda j: (0, j)),
        out_shape=jax.ShapeDtypeStruct((r, n), F32),
        compiler_params=_cparams("arbitrary"),
        name="adaln_modulation",
    )(cc, w_ada, b_ada.reshape(1, n))


def _ffn_kernel(x_ref, m_ref, g_ref, b_ref, wa_ref, wb_ref, wd_ref, o_ref, *, sub, alpha, n_sub):
    shift = m_ref[0, 3 * sub:3 * sub + 1, :]
    scale = m_ref[0, 3 * sub + 1:3 * sub + 2, :]
    gate = 0.5 * m_ref[0, 3 * sub + 2:3 * sub + 3, :]
    ts = x_ref.shape[1] // n_sub
    rows = [slice(i * ts, (i + 1) * ts) for i in range(n_sub)]
    u = [(_ln(x_ref[0, r, :]) * (1.0 + scale) + shift).astype(BF16) for r in rows]
    a = [_dot(ui, wa_ref[...]) for ui in u]
    b = [_dot(ui, wb_ref[...]) for ui in u]
    h = [(_silu(ai) * bi).astype(BF16) for ai, bi in zip(a, b)]
    y = [_dot(hi, wd_ref[...]) for hi in h]
    for r, yi in zip(rows, y):
        o_ref[0, r, :] = _ln(alpha * x_ref[0, r, :] + gate * yi) * g_ref[sub:sub + 1, :] + b_ref[sub:sub + 1, :]


def _ffn_call(x, mod3, ln_g, ln_b, w_up, w_down, *, sub, alpha, mod_row, tm, n_sub):
    bsz, n, d = x.shape
    f = w_down.shape[0]
    if mod_row is None:
        mod_idx = lambda b, i: (b, 0, 0)
    else:
        mod_idx = lambda b, i: (mod_row, 0, 0)
    once = pl.Buffered(1)
    return pl.pallas_call(
        functools.partial(_ffn_kernel, sub=sub, alpha=alpha, n_sub=n_sub),
        grid=(bsz, n // tm),
        in_specs=[
            pl.BlockSpec((1, tm, d), lambda b, i: (b, i, 0)),
            pl.BlockSpec((1, N_MOD, d), mod_idx),
            pl.BlockSpec(ln_g.shape, lambda b, i: (0, 0)),
            pl.BlockSpec(ln_b.shape, lambda b, i: (0, 0)),
            pl.BlockSpec((d, f), lambda b, i: (0, 0), pipeline_mode=once),
            pl.BlockSpec((d, f), lambda b, i: (0, 1), pipeline_mode=once),
            pl.BlockSpec((f, d), lambda b, i: (0, 0), pipeline_mode=once),
        ],
        out_specs=pl.BlockSpec((1, tm, d), lambda b, i: (b, i, 0)),
        out_shape=jax.ShapeDtypeStruct((bsz, n, d), F32),
        compiler_params=_cparams("parallel", "parallel"),
        name=f"ffn_sublayer_{sub}",
    )(x, mod3, ln_g, ln_b, w_up, w_up, w_down)


def _proj_kernel(x_ref, m_ref, *refs, n_sub):
    n_out = len(refs) // 2
    shift = m_ref[0, 3:4, :]
    scale = m_ref[0, 4:5, :]
    ts = x_ref.shape[1] // n_sub
    rows = [slice(i * ts, (i + 1) * ts) for i in range(n_sub)]
    u = [(_ln(x_ref[0, r, :]) * (1.0 + scale) + shift).astype(BF16) for r in rows]
    for w_ref, o_ref in zip(refs[:n_out], refs[n_out:]):
        for r, ui in zip(rows, u):
            o_ref[0, r, :] = _dot(ui, w_ref[...]).astype(o_ref.dtype)


def _proj_call(x, mod3, ws, dts, *, mod_row, tm, n_sub):
    bsz, n, d = x.shape
    if mod_row is None:
        mod_idx = lambda b, i: (b, 0, 0)
    else:
        mod_idx = lambda b, i: (mod_row, 0, 0)
    return pl.pallas_call(
        functools.partial(_proj_kernel, n_sub=n_sub),
        grid=(bsz, n // tm),
        in_specs=[pl.BlockSpec((1, tm, d), lambda b, i: (b, i, 0)),
                  pl.BlockSpec((1, N_MOD, d), mod_idx)]
                 + [pl.BlockSpec(w.shape, lambda b, i: (0, 0), pipeline_mode=pl.Buffered(1)) for w in ws],
        out_specs=[pl.BlockSpec((1, tm, w.shape[1]), lambda b, i: (b, i, 0)) for w in ws],
        out_shape=[jax.ShapeDtypeStruct((bsz, n, w.shape[1]), dt) for w, dt in zip(ws, dts)],
        compiler_params=_cparams("parallel", "parallel"),
        name="mixer_in_proj",
    )(x, mod3, *ws)


def _attn_kernel(q_ref, k_ref, v_ref, kc_ref, vc_ref, bias_ref, o_ref, *, tr, rows, rpi):
    i = pl.program_id(1)
    n_pairs = NA_HEADS // 2
    pw = 2 * NA_HEAD_DIM
    win = NA_WIN_ROWS * GRID_W
    lane = lax.broadcasted_iota(jnp.int32, (GRID_W, pw), 1)
    first_head = lane < NA_HEAD_DIM
    scale = NA_HEAD_DIM ** -0.5

    def row_body(it, carry):
        chains = [(jr, p) for jr in range(rpi) for p in range(n_pairs)]
        cs = [slice(p * pw, (p + 1) * pw) for jr, p in chains]
        q0, tok0, b0 = [], [], []
        for jr in range(rpi):
            rl = it * rpi + jr
            r = i * tr + rl
            r0 = jnp.clip(r - NA_WIN_ROWS // 2, 0, rows - NA_WIN_ROWS)
            tok0 += [pl.multiple_of(r0 * GRID_W, GRID_W)] * n_pairs
            q0 += [pl.multiple_of(rl * GRID_W, GRID_W)] * n_pairs
            b0 += [r0 - r + NA_WIN_ROWS - 1] * n_pairs
        ids = range(len(chains))
        qs = []
        for c in ids:
            q2 = q_ref[0, pl.ds(q0[c], GRID_W), cs[c]] * scale
            zero = jnp.zeros_like(q2)
            qs.append(jnp.concatenate([jnp.where(first_head, q2, zero), jnp.where(first_head, zero, q2)], axis=0))
        s_loc = [_dot_nt(qs[c], k_ref[0, pl.ds(tok0[c], win), cs[c]]) for c in ids]
        nq = 2 * GRID_W
        pcs = cs[:n_pairs]

        def per_chain(stacked):
            return [stacked[p][jr * nq:(jr + 1) * nq] for jr, p in chains]

        s_ctx = per_chain([_dot_nt(jnp.concatenate([qs[jr * n_pairs + p] for jr in range(rpi)], axis=0),
                                   kc_ref[0, :, pcs[p]]) for p in range(n_pairs)])
        s_loc = [s_loc[c] + jnp.concatenate([bias_ref[b0[c] + 2 * m, chains[c][1]] for m in range(NA_WIN_ROWS // 2)],
                                            axis=1) for c in ids]
        mx = [jnp.maximum(jnp.max(s_loc[c], axis=-1, keepdims=True), jnp.max(s_ctx[c], axis=-1, keepdims=True))
              for c in ids]
        p_loc = [jnp.exp(s_loc[c] - mx[c]) for c in ids]
        p_ctx = [jnp.exp(s_ctx[c] - mx[c]) for c in ids]
        den = [jnp.sum(p_loc[c], axis=-1, keepdims=True) + jnp.sum(p_ctx[c], axis=-1, keepdims=True) for c in ids]
        o_ctx = per_chain([_dot(jnp.concatenate([p_ctx[jr * n_pairs + p] for jr in range(rpi)], axis=0).astype(BF16),
                                vc_ref[0, :, pcs[p]]) for p in range(n_pairs)])
        o = [_dot(p_loc[c].astype(BF16), v_ref[0, pl.ds(tok0[c], win), cs[c]]) + o_ctx[c] for c in ids]
        for c in ids:
            on = o[c] / den[c]
            o_ref[0, pl.ds(q0[c], GRID_W), cs[c]] = jnp.where(first_head, on[:GRID_W], on[GRID_W:]).astype(o_ref.dtype)
        return carry

    lax.fori_loop(0, tr // rpi, row_body, 0)


def _attn_call(qkv, kv_ctx, bias_tab, *, tr, rpi):
    bsz, n, _ = qkv.shape
    rows = n // GRID_W
    lc = kv_ctx.shape[1]
    assert rows >= NA_WIN_ROWS and rows % tr == 0 and tr % rpi == 0
    return pl.pallas_call(
        functools.partial(_attn_kernel, tr=tr, rows=rows, rpi=rpi),
        grid=(bsz, rows // tr),
        in_specs=[
            pl.BlockSpec((1, tr * GRID_W, NA_W), lambda b, i: (b, i, 0)),
            pl.BlockSpec((1, n, NA_W), lambda b, i: (b, 0, 1)),
            pl.BlockSpec((1, n, NA_W), lambda b, i: (b, 0, 2)),
            pl.BlockSpec((1, lc, NA_W), lambda b, i: (b, 0, 0)),
            pl.BlockSpec((1, lc, NA_W), lambda b, i: (b, 0, 1)),
            pl.BlockSpec(bias_tab.shape, lambda b, i: (0, 0, 0, 0)),
        ],
        out_specs=pl.BlockSpec((1, tr * GRID_W, NA_W), lambda b, i: (b, i, 0)),
        out_shape=jax.ShapeDtypeStruct((bsz, n, NA_W), BF16),
        compiler_params=_cparams("parallel", "arbitrary"),
        name="neighbourhood_attention",
    )(qkv, qkv, qkv, kv_ctx, kv_ctx, bias_tab)


def _na_bias_table(rpb):
    col = np.arange(GRID_W)
    c0 = np.clip(col - NA_WIN_COLS // 2, 0, GRID_W - NA_WIN_COLS)
    rel = col[None, :] - col[:, None] + NA_WIN_COLS - 1
    valid = (col[None, :] >= c0[:, None]) & (col[None, :] < c0[:, None] + NA_WIN_COLS)
    n_rel = 2 * NA_WIN_COLS - 1
    onehot = ((np.arange(n_rel)[:, None, None] == rel[None]) & valid[None]).astype(np.float32)
    dense = jnp.einsum("rhk,kwc->rhwc", rpb.astype(F32).transpose(1, 0, 2), onehot,
                       precision=lax.Precision.HIGHEST)
    dense = jnp.where(valid, dense, MASK_VALUE)
    two = jnp.concatenate([dense[:-1], dense[1:]], axis=-1)
    nr, h = two.shape[0], two.shape[1]
    return two.reshape(nr, h // 2, 2 * GRID_W, 2 * GRID_W)


def _gdn_prep_kernel(x_ref, xp_ref, xn_ref, ab_ref, cw_ref, nea_ref, dtb_ref, cos_ref, sin_ref, mats_ref,
                     q_ref, k_ref, v_ref, gb_ref, gt_ref, *, t, use_rope):
    i = pl.program_id(1)
    halo = SUBLANES
    pad = GDN_CONV // 2
    hd = GDN_HEAD_DIM
    taps = [j for j in range(GDN_CONV) if j != pad]
    n_tap = len(taps)
    shift_stack = mats_ref[0:n_tap].reshape(n_tap * t, t)
    first = i > 0
    last = i < pl.num_programs(1) - 1
    zeros = jnp.zeros((halo, hd), F32)
    lane = lax.broadcasted_iota(jnp.int32, (t, hd), 1)
    half0 = (lane % (hd // 2)) < (hd // 4)
    quarter = hd // 4
    shifted2 = None
    for cb in range(3 * GDN_HEADS):
        cs = slice(cb * hd, (cb + 1) * hd)
        xb = x_ref[0, :, cs]
        cw = cw_ref[:, cs]
        if cb % 2 == 0:
            shifted2 = _dot(shift_stack, x_ref[0, :, cb * hd:(cb + 2) * hd].astype(BF16))
        shifted = shifted2[:, (cb % 2) * hd:(cb % 2 + 1) * hd]
        acc = cw[pad:pad + 1, :] * xb
        for n, j in enumerate(taps):
            acc = acc + cw[j:j + 1, :] * shifted[n * t:(n + 1) * t]
        top = jnp.concatenate([jnp.where(first, xp_ref[0, :, cs], 0.0), zeros], axis=0)
        bot = jnp.concatenate([zeros, jnp.where(last, xn_ref[0, :, cs], 0.0)], axis=0)
        e_top = sum(cw[j:j + 1, :] * top[halo - pad + j:2 * halo - pad + j, :] for j in range(pad))
        e_bot = sum(cw[j:j + 1, :] * bot[j - pad:j - pad + halo, :] for j in range(pad + 1, GDN_CONV))
        acc = jnp.concatenate([acc[:halo] + e_top, acc[halo:t - halo], acc[t - halo:] + e_bot], axis=0)
        y = _silu(acc)
        which, h = divmod(cb, GDN_HEADS)
        if which < 2:
            y = y * lax.rsqrt(jnp.sum(y * y, axis=-1, keepdims=True) + NORM_EPS)
            if use_rope:
                partner = jnp.where(half0, pltpu.roll(y, hd - quarter, 1), pltpu.roll(y, quarter, 1))
                y = y * cos_ref[...] + partner * sin_ref[...]
        if which == 0:
            y = y * (hd ** -0.5)
        (q_ref, k_ref, v_ref)[which][0, :, h * hd:(h + 1) * hd] = y

    ab = ab_ref[0]
    nh2 = 2 * GDN_HEADS
    za = ab + dtb_ref[...]
    softplus = jnp.maximum(za, 0.0) + jnp.log(1.0 + jnp.exp(-jnp.abs(za)))
    g = nea_ref[...] * softplus
    beta = _sigmoid(ab)
    n_tap = GDN_CONV - 1
    tri_f, tri_b, same = mats_ref[n_tap], mats_ref[n_tap + 1], mats_ref[n_tap + 2]
    lane_g = lax.broadcasted_iota(jnp.int32, (t, LANES), 1)
    gcum = jnp.where(lane_g < GDN_HEADS, _dot_exact_lhs(tri_f, g), _dot_exact_lhs(tri_b, g))
    gtot = pltpu.roll(_dot_exact_lhs(same, g), 2 * nh2, 1)
    gb_ref[0] = jnp.where(lane_g < nh2, gcum,
                          jnp.where(lane_g < 2 * nh2, beta, jnp.where(lane_g < 3 * nh2, gtot, 0.0)))
    gt_ref[0] = jnp.transpose(jnp.where(lane_g < nh2, gcum, 0.0))[0:nh2, :]


def _prep_matrices(t, chunk):
    r = np.arange(t)[:, None]
    c = np.arange(t)[None, :]
    pad = GDN_CONV // 2
    shifts = [c == r + (j - pad) for j in range(GDN_CONV) if j != pad]
    same = (r // chunk) == (c // chunk)
    mats = shifts + [same & (c <= r), same & (c >= r), same]
    return jnp.asarray(np.stack(mats).astype(np.float32), dtype=BF16)


def _gdn_prep_call(qkv_g, ab, conv_w, nea_row, dtb_row, cos_t, sin_t, *, use_rope, t, chunk):
    bsz, n, c3 = qkv_g.shape
    nb8 = n // SUBLANES
    tb8 = t // SUBLANES
    f3 = jax.ShapeDtypeStruct((bsz, n, GDN_W), F32)
    mats = _prep_matrices(t, chunk)
    return pl.pallas_call(
        functools.partial(_gdn_prep_kernel, t=t, use_rope=use_rope),
        grid=(bsz, n // t),
        in_specs=[
            pl.BlockSpec((1, t, c3), lambda b, i: (b, i, 0)),
            pl.BlockSpec((1, SUBLANES, c3), lambda b, i: (b, jnp.maximum(i * tb8 - 1, 0), 0)),
            pl.BlockSpec((1, SUBLANES, c3), lambda b, i: (b, jnp.minimum((i + 1) * tb8, nb8 - 1), 0)),
            pl.BlockSpec((1, t, LANES), lambda b, i: (b, i, 0)),
            pl.BlockSpec(conv_w.shape, lambda b, i: (0, 0)),
            pl.BlockSpec((1, LANES), lambda b, i: (0, 0)),
            pl.BlockSpec((1, LANES), lambda b, i: (0, 0)),
            pl.BlockSpec((t, GDN_HEAD_DIM), lambda b, i: (i, 0)),
            pl.BlockSpec((t, GDN_HEAD_DIM), lambda b, i: (i, 0)),
            pl.BlockSpec(mats.shape, lambda b, i: (0, 0, 0)),
        ],
        out_specs=[
            pl.BlockSpec((1, t, GDN_W), lambda b, i: (b, i, 0)),
            pl.BlockSpec((1, t, GDN_W), lambda b, i: (b, i, 0)),
            pl.BlockSpec((1, t, GDN_W), lambda b, i: (b, i, 0)),
            pl.BlockSpec((1, t, LANES), lambda b, i: (b, i, 0)),
            pl.BlockSpec((1, 2 * GDN_HEADS, t), lambda b, i: (b, 0, i)),
        ],
        out_shape=[f3, f3, f3,
                   jax.ShapeDtypeStruct((bsz, n, LANES), F32),
                   jax.ShapeDtypeStruct((bsz, 2 * GDN_HEADS, n), F32)],
        compiler_params=_cparams("parallel", "parallel"),
        name="gdn_prepare",
    )(qkv_g, qkv_g, qkv_g, ab, conv_w, nea_row, dtb_row, cos_t, sin_t, mats)


TRI_BASE = 8


def _slabs(x, s, second):
    start = s if second else 0
    return jnp.concatenate([x[i:i + s] for i in range(start, x.shape[0], 2 * s)], axis=0)


def _interleave(upd, other, s, second):
    parts = []
    for p in range(other.shape[0] // (2 * s)):
        u = upd[s * p:s * (p + 1)]
        if second:
            parts += [other[2 * s * p:2 * s * p + s], u]
        else:
            parts += [u, other[2 * s * p + s:2 * s * (p + 1)]]
    return jnp.concatenate(parts, axis=0)


def _unit_tri_solve(ms, rhss, lower):
    c = ms[0].shape[0]
    n = len(ms)
    row = lax.broadcasted_iota(jnp.int32, (c, c), 0)
    col = lax.broadcasted_iota(jnp.int32, (c, c), 1)
    eye = (row == col).astype(F32)
    same = lambda s: (row // s) == (col // s)
    base = same(TRI_BASE)
    ds = [jnp.where(base, m, 0.0) for m in ms]
    ps = [eye - d for d in ds]
    dbs = [d.astype(BF16) for d in ds]
    qs = [_dot(db, db) for db in dbs]
    n_iter = int(math.log2(TRI_BASE)) - 1
    for it in range(n_iter):
        qbs = [q.astype(BF16) for q in qs]
        if it < n_iter - 1:
            yqs = [_dot(qb, jnp.concatenate([p, q], axis=1).astype(BF16)) for qb, p, q in zip(qbs, ps, qs)]
            ps = [p + yq[:, :c] for p, yq in zip(ps, yqs)]
            qs = [yq[:, c:] for yq in yqs]
        else:
            ys = [_dot(qb, p.astype(BF16)) for qb, p in zip(qbs, ps)]
            ps = [p + y for p, y in zip(ps, ys)]
    ts = ps
    zeros = jnp.zeros((c, c), F32)
    s = TRI_BASE
    while 2 * s < c:
        off_diag = same(2 * s) & jnp.logical_not(same(s))
        l_rows = [_slabs(jnp.where(off_diag, m, 0.0), s, lo).astype(BF16) for m, lo in zip(ms, lower)]
        t_rows = [_slabs(t, s, lo) for t, lo in zip(ts, lower)]
        ys = [_dot(l, t.astype(BF16)) for l, t in zip(l_rows, ts)]
        y_full = [_interleave(y, zeros, s, lo).astype(BF16) for y, lo in zip(ys, lower)]
        zs = [_dot(tr.astype(BF16), yf) for tr, yf in zip(t_rows, y_full)]
        ts = [_interleave(tr - z, t, s, lo) for tr, z, t, lo in zip(t_rows, zs, ts, lower)]
        s *= 2
    hc = c // 2
    off_diag = jnp.logical_not(same(hc))
    ind = [slice(0, hc) if lo else slice(hc, c) for lo in lower]
    dep = [slice(hc, c) if lo else slice(0, hc) for lo in lower]

    def place(x_ind, x_dep, lo):
        return jnp.concatenate([x_ind, x_dep] if lo else [x_dep, x_ind], axis=0)

    x1 = [_dot(t[i].astype(BF16), r.astype(BF16)) for t, i, r in zip(ts, ind, rhss)]
    mixed = [place(x, r[dp], lo).astype(BF16) for x, r, dp, lo in zip(x1, rhss, dep, lower)]
    ys = [_dot(jnp.where(off_diag, m, 0.0)[dp].astype(BF16), mx) for m, dp, mx in zip(ms, dep, mixed)]
    mixed2 = [place(r[i], r[dp] - y, lo).astype(BF16) for r, i, dp, y, lo in zip(rhss, ind, dep, ys, lower)]
    x2 = [_dot(t[dp].astype(BF16), mx) for t, dp, mx in zip(ts, dep, mixed2)]
    return [place(a, b, lo) for a, b, lo in zip(x1, x2, lower)]


def _gdn_scan_kernel(*refs, tb, chunk, with_output):
    (qf, kf, vf, gf, gtf, qb, kb, vb, gb, gtb, s0_ref) = refs[:11]
    if with_output:
        of_ref, ob_ref, s_ref = refs[11:]
        o_refs = (of_ref, ob_ref)
    else:
        sfin_ref, s_ref = refs[11:]
    ins = ((qf, kf, vf, gf, gtf), (qb, kb, vb, gb, gtb))
    j = pl.program_id(1)
    c = chunk
    nch = tb // c
    hd = GDN_HEAD_DIM

    @pl.when(j == 0)
    def _():
        s_ref[...] = s0_ref[0]

    row = lax.broadcasted_iota(jnp.int32, (c, c), 0)
    col = lax.broadcasted_iota(jnp.int32, (c, c), 1)
    incl = (col <= row, col >= row)
    strict = (col < row, col > row)
    nh2 = 2 * GDN_HEADS
    heads = [(d, h) for d in range(2) for h in range(GDN_HEADS)]
    units = [(ci, d, h) for ci in range(nch) for d, h in heads]
    off = lambda ci, d: (ci if d == 0 else nch - 1 - ci) * c
    assert c <= LANES

    gblk = {(ci, d): ins[d][3][0, off(ci, d):off(ci, d) + c, :] for ci in range(nch) for d in range(2)}
    gtblk = {(ci, d): ins[d][4][0, :, off(ci, d):off(ci, d) + c] for ci in range(nch) for d in range(2)}

    def lane_bcast(ci, d, h, o):
        dh = d * GDN_HEADS + h
        return jnp.broadcast_to(gblk[ci, d][:, o + dh:o + dh + 1], (c, LANES))

    g_b = [lane_bcast(ci, d, h, 0) for ci, d, h in units]
    beta = [lane_bcast(ci, d, h, nh2) for ci, d, h in units]
    g_last = [lane_bcast(ci, d, h, 2 * nh2) for ci, d, h in units]
    g_row = [gtblk[ci, d][d * GDN_HEADS + h:d * GDN_HEADS + h + 1, :] for ci, d, h in units]
    ld = lambda idx: [ins[d][idx][0, off(ci, d):off(ci, d) + c, h * hd:(h + 1) * hd] for ci, d, h in units]
    q, k, v = ld(0), ld(1), ld(2)
    decay = [jnp.exp(jnp.where(incl[d], gb_[:, :c] - gr, MASK_VALUE)) for (ci, d, h), gb_, gr in zip(units, g_b, g_row)]
    e_g = [jnp.exp(x) for x in g_b]
    k_beta = [a * b for a, b in zip(k, beta)]
    k16 = [a.astype(BF16) for a in k]
    lhs = [jnp.concatenate([a, b], axis=0) for a, b in zip(k_beta, q)] if with_output else k_beta
    kq = [_dot_nt(a.astype(BF16), b) for a, b in zip(lhs, k16)]
    m = [jnp.where(strict[d], a[:c] * b, 0.0) for (ci, d, h), a, b in zip(units, kq, decay)]
    rhs = [jnp.concatenate([a * b, kb_ * e], axis=1) for a, b, kb_, e in zip(v, beta, k_beta, e_g)]
    uw = _unit_tri_solve(m, rhs, [d == 0 for ci, d, h in units])
    k_tail = [a * jnp.exp(gl - gb_) for a, gl, gb_ in zip(k, g_last, g_b)]
    g_tot = [jnp.exp(gl[0:1, :]) for gl in g_last]

    s = [s_ref[d, h] for d, h in heads]
    nhd = len(heads)
    for ci in range(nch):
        u0 = ci * nhd
        sel = lambda xs_: xs_[u0:u0 + nhd]
        s16 = [a.astype(BF16) for a in s]
        lhs = ([jnp.concatenate([a[:, hd:], b * e], axis=0) for a, b, e in zip(sel(uw), sel(q), sel(e_g))]
               if with_output else [a[:, hd:] for a in sel(uw)])
        xs = [_dot(a.astype(BF16), b) for a, b in zip(lhs, s16)]
        v_new = [a[:, :hd] - b[:c] for a, b in zip(sel(uw), xs)]
        if with_output:
            o_local = [_dot((a[c:] * dcy).astype(BF16), b.astype(BF16))
                       for a, dcy, b in zip(sel(kq), sel(decay), v_new)]
            for (d, h), a, b in zip(heads, xs, o_local):
                o_refs[d][0, off(ci, d):off(ci, d) + c, h * hd:(h + 1) * hd] = (a[c:] + b).astype(o_refs[d].dtype)
        ktv = [_dot_tn(a.astype(BF16), b.astype(BF16)) for a, b in zip(sel(k_tail), v_new)]
        s = [a * gt_ + b for a, gt_, b in zip(s, sel(g_tot), ktv)]
    for (d, h), a in zip(heads, s):
        s_ref[d, h] = a

    if not with_output:
        @pl.when(j == pl.num_programs(1) - 1)
        def _():
            sfin_ref[0] = s_ref[...]


def _gdn_scan_call(q, k, v, gbeta, gt, s0, *, tb, chunk, with_output):
    bsz, n, _ = q.shape
    nblk = n // tb
    fwd = lambda b, j: (b, j, 0)
    bwd = lambda b, j: (b, nblk - 1 - j, 0)
    fwd_t = lambda b, j: (b, 0, j)
    bwd_t = lambda b, j: (b, 0, nblk - 1 - j)

    def specs(idx, idx_t):
        return [pl.BlockSpec((1, tb, GDN_W), idx)] * 3 + [pl.BlockSpec((1, tb, LANES), idx),
                                                          pl.BlockSpec((1, 2 * GDN_HEADS, tb), idx_t)]

    state_spec = pl.BlockSpec((1, 2, GDN_HEADS, GDN_HEAD_DIM, GDN_HEAD_DIM), lambda b, j: (b, 0, 0, 0, 0))
    if with_output:
        out_specs = [pl.BlockSpec((1, tb, GDN_W), fwd), pl.BlockSpec((1, tb, GDN_W), bwd)]
        out_shape = [jax.ShapeDtypeStruct((bsz, n, GDN_W), BF16)] * 2
    else:
        out_specs = state_spec
        out_shape = jax.ShapeDtypeStruct(s0.shape, F32)
    return pl.pallas_call(
        functools.partial(_gdn_scan_kernel, tb=tb, chunk=chunk, with_output=with_output),
        grid=(bsz, nblk),
        in_specs=specs(fwd, fwd_t) + specs(bwd, bwd_t) + [state_spec],
        out_specs=out_specs,
        out_shape=out_shape,
        scratch_shapes=[pltpu.VMEM((2, GDN_HEADS, GDN_HEAD_DIM, GDN_HEAD_DIM), F32)],
        compiler_params=_cparams("parallel", "arbitrary"),
        name="gdn_scan_latent" if with_output else "gdn_scan_context",
    )(q, k, v, gbeta, gt, q, k, v, gbeta, gt, s0)


def _merge_kernel(x_ref, m_ref, g_ref, b_ref, oa_ref, of_ref, ob_ref, z_ref, gab_ref, nw_ref,
                  wpa_ref, wpb_ref, wo_ref, o_ref, *, alpha, n_sub):
    nw = nw_ref[...]
    d = x_ref.shape[-1]
    gate = m_ref[0, 5:6, :]
    ts = x_ref.shape[1] // n_sub
    rows = [slice(i * ts, (i + 1) * ts) for i in range(n_sub)]

    def gdn_out(r):
        o = of_ref[0, r, :].astype(F32) + ob_ref[0, r, :].astype(F32)
        parts = []
        for h in range(GDN_HEADS):
            oh = o[:, h * GDN_HEAD_DIM:(h + 1) * GDN_HEAD_DIM]
            parts.append(oh * lax.rsqrt(jnp.mean(oh * oh, axis=-1, keepdims=True) + NORM_EPS) * nw)
        return (jnp.concatenate(parts, axis=1) * _silu(z_ref[0, r, :].astype(F32))).astype(BF16)

    o_b = [gdn_out(r) for r in rows]
    ya = [_dot(oa_ref[0, r, :], wpa_ref[...]) for r in rows]
    yb = [_dot(ob, wpb_ref[...]) for ob in o_b]
    y = [(_sigmoid(gab_ref[0, r, :d].astype(F32)) * a
          + _sigmoid(gab_ref[0, r, d:].astype(F32)) * b).astype(BF16) for r, a, b in zip(rows, ya, yb)]
    yo = [_dot(yi, wo_ref[...]) for yi in y]
    for r, yoi in zip(rows, yo):
        o_ref[0, r, :] = _ln(alpha * x_ref[0, r, :] + gate * yoi) * g_ref[1:2, :] + b_ref[1:2, :]


def _merge_call(x, mod3, ln_g, ln_b, o_a, o_f, o_b, z, gab, norm_w, w_pa, w_pb, w_o, *, alpha, tm, n_sub):
    bsz, n, d = x.shape
    tok = lambda w: pl.BlockSpec((1, tm, w), lambda b, i: (b, i, 0))
    full = lambda a: pl.BlockSpec(a.shape, lambda b, i: (0,) * a.ndim)
    return pl.pallas_call(
        functools.partial(_merge_kernel, alpha=alpha, n_sub=n_sub),
        grid=(bsz, n // tm),
        in_specs=[tok(d), pl.BlockSpec((1, N_MOD, d), lambda b, i: (b, 0, 0)), full(ln_g), full(ln_b),
                  tok(NA_W), tok(GDN_W), tok(GDN_W), tok(GDN_W), tok(2 * d), full(norm_w),
                  full(w_pa), full(w_pb), full(w_o)],
        out_specs=tok(d),
        out_shape=jax.ShapeDtypeStruct((bsz, n, d), F32),
        compiler_params=_cparams("parallel", "parallel"),
        name="branch_merge",
    )(x, mod3, ln_g, ln_b, o_a, o_f, o_b, z, gab, norm_w, w_pa, w_pb, w_o)


def _rope_tables(n_tok):
    n_freq = GDN_HEAD_DIM // 4
    freqs = ROPE_THETA ** (-jnp.arange(n_freq, dtype=F32) / n_freq)
    t = jnp.arange(n_tok)
    pos = jnp.stack([t // GRID_W, t % GRID_W], axis=-1).astype(F32)
    ang = pos[:, :, None] * freqs
    cos = jnp.cos(ang)
    sin = jnp.sin(ang)
    cos_t = jnp.concatenate([cos, cos], axis=-1).reshape(n_tok, GDN_HEAD_DIM)
    sin_t = jnp.concatenate([-sin, sin], axis=-1).reshape(n_tok, GDN_HEAD_DIM)
    return cos_t, sin_t


def _pick_tile(n, pref):
    t = min(n, pref)
    assert n % t == 0
    return t


def kernel(x, c, ctx, c_ctx, w_ada, b_ada, ln_g, ln_b, ffn1_w_in, ffn1_w_out, w_in, na_rpb, gdn_conv_w,
           gdn_a_log, gdn_dt_bias, gdn_norm_w, w_pa, w_pb, w_o, ffn2_w_in, ffn2_w_out):
    depth = w_ada.shape[0]
    assert depth == 1, "context-update path of non-final layers is not implemented"
    bsz, n, d = x.shape
    lc = ctx.shape[1]
    alpha = (2 * depth) ** 0.25
    layer = 0

    n_rows = -(-(bsz + 1) // SUBLANES) * SUBLANES
    cc = jnp.zeros((n_rows, d), F32).at[:bsz].set(c).at[bsz].set(c_ctx)
    mod3 = _mod_call(cc, w_ada[layer], b_ada[layer], tn=MOD_COLS).reshape(n_rows, N_MOD, d)

    g_ln, b_ln = ln_g[layer], ln_b[layer]
    w1u, w1d = ffn1_w_in[layer].astype(BF16), ffn1_w_out[layer].astype(BF16)
    dense = dict(tm=_pick_tile(n, DENSE_TOKENS), n_sub=DENSE_SUBTILES)
    dense_ctx = dict(tm=_pick_tile(bsz * lc, DENSE_TOKENS), n_sub=DENSE_SUBTILES)
    x1 = _ffn_call(x, mod3, g_ln, b_ln, w1u, w1d, sub=0, alpha=alpha, mod_row=None, **dense)
    ctx_flat = ctx.reshape(1, bsz * lc, d)
    ctx1 = _ffn_call(ctx_flat, mod3, g_ln, b_ln, w1u, w1d, sub=0, alpha=alpha, mod_row=bsz, **dense_ctx)

    wi = w_in[layer]
    o0 = 3 * NA_W
    o1 = o0 + 3 * GDN_W
    o2 = o1 + GDN_W
    o3 = o2 + 4 * GDN_HEADS
    w_a = wi[:, :o0].astype(BF16)
    w_g = wi[:, o0:o1].astype(BF16)
    w_z = wi[:, o1:o2].astype(BF16)
    w_ab = jnp.zeros((d, LANES), F32).at[:, :4 * GDN_HEADS].set(wi[:, o2:o3]).astype(BF16)
    w_gab = wi[:, o3:].astype(BF16)
    qkv_a, qkv_g, z, gab, ab = _proj_call(x1, mod3, (w_a, w_g, w_z, w_gab, w_ab), (BF16, F32, BF16, BF16, F32),
                                          mod_row=None, **dense)
    kv_a_c, qkv_g_c, ab_c = (t.reshape(bsz, lc, t.shape[-1]) for t in _proj_call(
        ctx1, mod3, (w_a[:, NA_W:], w_g, w_ab), (BF16, F32, F32), mod_row=bsz, **dense_ctx))

    o_a = _attn_call(qkv_a, kv_a_c, _na_bias_table(na_rpb[layer]), tr=ATTN_ROWS, rpi=ATTN_ROWS_PER_ITER)

    nea = jnp.zeros((1, LANES), F32).at[0, :2 * GDN_HEADS].set(-jnp.exp(gdn_a_log[layer].astype(F32)).reshape(-1))
    dtb = jnp.zeros((1, LANES), F32).at[0, :2 * GDN_HEADS].set(gdn_dt_bias[layer].astype(F32).reshape(-1))
    cw = gdn_conv_w[layer]
    cos_t, sin_t = _rope_tables(n)
    tp = _pick_tile(n, GDN_TOKENS)
    tpc = _pick_tile(lc, GDN_TOKENS)
    q_l, k_l, v_l, gb_l, gt_l = _gdn_prep_call(qkv_g, ab, cw, nea, dtb, cos_t, sin_t, use_rope=True, t=tp,
                                                chunk=GDN_CHUNK)
    q_c, k_c, v_c, gb_c, gt_c = _gdn_prep_call(qkv_g_c, ab_c, cw, nea, dtb, cos_t[:lc], sin_t[:lc], use_rope=False,
                                                t=tpc, chunk=GDN_CHUNK)
    s0 = jnp.zeros((bsz, 2, GDN_HEADS, GDN_HEAD_DIM, GDN_HEAD_DIM), F32)
    s_ctx = _gdn_scan_call(q_c, k_c, v_c, gb_c, gt_c, s0, tb=tpc, chunk=GDN_CHUNK, with_output=False)
    o_f, o_b = _gdn_scan_call(q_l, k_l, v_l, gb_l, gt_l, s_ctx, tb=tp, chunk=GDN_CHUNK, with_output=True)

    x2 = _merge_call(x1, mod3, g_ln, b_ln, o_a, o_f, o_b, z, gab, gdn_norm_w[layer].reshape(1, GDN_HEAD_DIM),
                     w_pa[layer].astype(BF16), w_pb[layer].astype(BF16), w_o[layer].astype(BF16),
                     alpha=alpha, **dense)

    w2u, w2d = ffn2_w_in[layer].astype(BF16), ffn2_w_out[layer].astype(BF16)
    return _ffn_call(x2, mod3, g_ln, b_ln, w2u, w2d, sub=2, alpha=alpha, mod_row=None, **dense)
```

```python
import functools
import math

import numpy as np
import jax
import jax.numpy as jnp
from jax import lax
from jax.experimental import pallas as pl
from jax.experimental.pallas import tpu as pltpu

F32 = jnp.float32
BF16 = jnp.bfloat16

GRID_W = 64
NA_HEADS = 8
NA_HEAD_DIM = 64
NA_WIN_ROWS = 8
NA_WIN_COLS = 16
GDN_HEADS = 4
GDN_HEAD_DIM = 128
GDN_CONV = 5
N_MOD = 9
ROPE_THETA = 10000.0
LN_EPS = 1e-6
NORM_EPS = 1e-6
NA_W = NA_HEADS * NA_HEAD_DIM
GDN_W = GDN_HEADS * GDN_HEAD_DIM

LANES = 128
SUBLANES = 8
VMEM_LIMIT_BYTES = 56 * 1024 * 1024
MASK_VALUE = -1e30
GDN_CHUNK = 128

DENSE_TOKENS = 512
DENSE_SUBTILES = 2
GDN_TOKENS = 2 * GDN_CHUNK
GDN_SCAN_TOKENS = 4 * GDN_CHUNK
ATTN_ROWS = 8
ATTN_ROWS_PER_ITER = 4
MOD_COLS = 9 * LANES


def _cparams(*sem):
    return pltpu.CompilerParams(dimension_semantics=sem, vmem_limit_bytes=VMEM_LIMIT_BYTES)


def _ln(x):
    mu = jnp.mean(x, axis=-1, keepdims=True)
    xc = x - mu
    var = jnp.mean(xc * xc, axis=-1, keepdims=True)
    return xc * lax.rsqrt(var + LN_EPS)


def _sigmoid(x):
    return 0.5 * jnp.tanh(0.5 * x) + 0.5


def _silu(x):
    h = 0.5 * x
    return h + h * jnp.tanh(h)


def _dot(a, b):
    return jnp.dot(a, b, preferred_element_type=F32)


def _dot_nt(a, b):
    return lax.dot_general(a, b, (((1,), (1,)), ((), ())), preferred_element_type=F32)


def _dot_tn(a, b):
    return lax.dot_general(a, b, (((0,), (0,)), ((), ())), preferred_element_type=F32)


def _dot_exact_lhs(a_bf16, x):
    hi = x.astype(BF16)
    r1 = x - hi.astype(F32)
    mid = r1.astype(BF16)
    lo = (r1 - mid.astype(F32)).astype(BF16)
    return _dot(a_bf16, hi) + _dot(a_bf16, mid) + _dot(a_bf16, lo)


def _mod_kernel(c_ref, w_ref, b_ref, o_ref):
    s = _silu(c_ref[...]).astype(BF16)
    o_ref[...] = _dot(s, w_ref[...].astype(BF16)) + b_ref[...]


def _mod_call(cc, w_ada, b_ada, *, tn):
    r, d = cc.shape
    n = w_ada.shape[1]
    return pl.pallas_call(
        _mod_kernel,
        grid=(n // tn,),
        in_specs=[
            pl.BlockSpec((r, d), lambda j: (0, 0)),
            pl.BlockSpec((d, tn), lambda j: (0, j)),
            pl.BlockSpec((1, tn), lambda j: (0, j)),
        ],
        out_specs=pl.BlockSpec((r, tn), lambda j: (0, j)),
        out_shape=jax.ShapeDtypeStruct((r, n), F32),
        compiler_params=_cparams("arbitrary"),
        name="adaln_modulation",
    )(cc, w_ada, b_ada.reshape(1, n))


def _ffn_kernel(x_ref, m_ref, g_ref, b_ref, wa_ref, wb_ref, wd_ref, o_ref, *, sub, alpha, n_sub):
    shift = m_ref[0, 3 * sub:3 * sub + 1, :]
    scale = m_ref[0, 3 * sub + 1:3 * sub + 2, :]
    gate = 0.5 * m_ref[0, 3 * sub + 2:3 * sub + 3, :]
    ts = x_ref.shape[1] // n_sub
    rows = [slice(i * ts, (i + 1) * ts) for i in range(n_sub)]
    u = [(_ln(x_ref[0, r, :]) * (1.0 + scale) + shift).astype(BF16) for r in rows]
    a = [_dot(ui, wa_ref[...]) for ui in u]
    b = [_dot(ui, wb_ref[...]) for ui in u]
    h = [(_silu(ai) * bi).astype(BF16) for ai, bi in zip(a, b)]
    y = [_dot(hi, wd_ref[...]) for hi in h]
    for r, yi in zip(rows, y):
        o_ref[0, r, :] = _ln(alpha * x_ref[0, r, :] + gate * yi) * g_ref[sub:sub + 1, :] + b_ref[sub:sub + 1, :]


def _ffn_call(x, mod3, ln_g, ln_b, w_up, w_down, *, sub, alpha, mod_row, tm, n_sub):
    bsz, n, d = x.shape
    f = w_down.shape[0]
    if mod_row is None:
        mod_idx = lambda b, i: (b, 0, 0)
    else:
        mod_idx = lambda b, i: (mod_row, 0, 0)
    once = pl.Buffered(1)
    return pl.pallas_call(
        functools.partial(_ffn_kernel, sub=sub, alpha=alpha, n_sub=n_sub),
        grid=(bsz, n // tm),
        in_specs=[
            pl.BlockSpec((1, tm, d), lambda b, i: (b, i, 0)),
            pl.BlockSpec((1, N_MOD, d), mod_idx),
            pl.BlockSpec(ln_g.shape, lambda b, i: (0, 0)),
            pl.BlockSpec(ln_b.shape, lambda b, i: (0, 0)),
            pl.BlockSpec((d, f), lambda b, i: (0, 0), pipeline_mode=once),
            pl.BlockSpec((d, f), lambda b, i: (0, 1), pipeline_mode=once),
            pl.BlockSpec((f, d), lambda b, i: (0, 0), pipeline_mode=once),
        ],
        out_specs=pl.BlockSpec((1, tm, d), lambda b, i: (b, i, 0)),
        out_shape=jax.ShapeDtypeStruct((bsz, n, d), F32),
        compiler_params=_cparams("parallel", "parallel"),
        name=f"ffn_sublayer_{sub}",
    )(x, mod3, ln_g, ln_b, w_up, w_up, w_down)


_O_NA = 3 * NA_W
_O_GDN = _O_NA + 3 * GDN_W
_O_Z = _O_GDN + GDN_W
_O_AB = _O_Z + 4 * GDN_HEADS


def _regroup_kernel(w_ref, wa_ref, wg_ref, wz_ref, wgab_ref, wab_ref):
    w = w_ref[...]
    wa_ref[...] = w[:, :_O_NA].astype(BF16)
    wg_ref[...] = w[:, _O_NA:_O_GDN].astype(BF16)
    wz_ref[...] = w[:, _O_GDN:_O_Z].astype(BF16)
    wgab_ref[...] = w[:, _O_AB:].astype(BF16)
    lane = lax.broadcasted_iota(jnp.int32, (w.shape[0], LANES), 1)
    wab_ref[...] = jnp.where(lane < _O_AB - _O_Z, w[:, _O_Z:_O_Z + LANES], 0.0).astype(BF16)


def _regroup_w_in(w_in, *, tr=128):
    d, n_in = w_in.shape
    widths = (_O_NA, _O_GDN - _O_NA, _O_Z - _O_GDN, n_in - _O_AB, LANES)
    return pl.pallas_call(
        _regroup_kernel,
        grid=(d // tr,),
        in_specs=[pl.BlockSpec((tr, n_in), lambda i: (i, 0))],
        out_specs=[pl.BlockSpec((tr, w), lambda i: (i, 0)) for w in widths],
        out_shape=[jax.ShapeDtypeStruct((d, w), BF16) for w in widths],
        compiler_params=_cparams("parallel"),
        name="regroup_w_in",
    )(w_in)

def _proj_kernel(x_ref, m_ref, *refs, n_sub):
    n_out = len(refs) // 2
    shift = m_ref[0, 3:4, :]
    scale = m_ref[0, 4:5, :]
    ts = x_ref.shape[1] // n_sub
    rows = [slice(i * ts, (i + 1) * ts) for i in range(n_sub)]
    u = [(_ln(x_ref[0, r, :]) * (1.0 + scale) + shift).astype(BF16) for r in rows]
    for w_ref, o_ref in zip(refs[:n_out], refs[n_out:]):
        for r, ui in zip(rows, u):
            o_ref[0, r, :] = _dot(ui, w_ref[...]).astype(o_ref.dtype)


def _proj_call(x, mod3, ws, dts, *, mod_row, tm, n_sub):
    bsz, n, d = x.shape
    if mod_row is None:
        mod_idx = lambda b, i: (b, 0, 0)
    else:
        mod_idx = lambda b, i: (mod_row, 0, 0)
    return pl.pallas_call(
        functools.partial(_proj_kernel, n_sub=n_sub),
        grid=(bsz, n // tm),
        in_specs=[pl.BlockSpec((1, tm, d), lambda b, i: (b, i, 0)),
                  pl.BlockSpec((1, N_MOD, d), mod_idx)]
                 + [pl.BlockSpec(w.shape, lambda b, i: (0, 0), pipeline_mode=pl.Buffered(1)) for w in ws],
        out_specs=[pl.BlockSpec((1, tm, w.shape[1]), lambda b, i: (b, i, 0)) for w in ws],
        out_shape=[jax.ShapeDtypeStruct((bsz, n, w.shape[1]), dt) for w, dt in zip(ws, dts)],
        compiler_params=_cparams("parallel", "parallel"),
        name="mixer_in_proj",
    )(x, mod3, *ws)


def _attn_kernel(q_ref, k_ref, v_ref, kc_ref, vc_ref, bias_ref, o_ref, *, tr, rows, rpi):
    i = pl.program_id(1)
    n_pairs = NA_HEADS // 2
    pw = 2 * NA_HEAD_DIM
    win = NA_WIN_ROWS * GRID_W
    lane = lax.broadcasted_iota(jnp.int32, (GRID_W, pw), 1)
    first_head = lane < NA_HEAD_DIM
    scale = NA_HEAD_DIM ** -0.5

    def row_body(it, carry):
        chains = [(jr, p) for jr in range(rpi) for p in range(n_pairs)]
        cs = [slice(p * pw, (p + 1) * pw) for jr, p in chains]
        q0, tok0, b0 = [], [], []
        for jr in range(rpi):
            rl = it * rpi + jr
            r = i * tr + rl
            r0 = jnp.clip(r - NA_WIN_ROWS // 2, 0, rows - NA_WIN_ROWS)
            tok0 += [pl.multiple_of(r0 * GRID_W, GRID_W)] * n_pairs
            q0 += [pl.multiple_of(rl * GRID_W, GRID_W)] * n_pairs
            b0 += [r0 - r + NA_WIN_ROWS - 1] * n_pairs
        ids = range(len(chains))
        qs = []
        for c in ids:
            q2 = q_ref[0, pl.ds(q0[c], GRID_W), cs[c]] * scale
            zero = jnp.zeros_like(q2)
            qs.append(jnp.concatenate([jnp.where(first_head, q2, zero), jnp.where(first_head, zero, q2)], axis=0))
        s_loc = [_dot_nt(qs[c], k_ref[0, pl.ds(tok0[c], win), cs[c]]) for c in ids]
        nq = 2 * GRID_W
        pcs = cs[:n_pairs]

        def per_chain(stacked):
            return [stacked[p][jr * nq:(jr + 1) * nq] for jr, p in chains]

        s_ctx = per_chain([_dot_nt(jnp.concatenate([qs[jr * n_pairs + p] for jr in range(rpi)], axis=0),
                                   kc_ref[0, :, pcs[p]]) for p in range(n_pairs)])
        s_loc = [s_loc[c] + jnp.concatenate([bias_ref[b0[c] + 2 * m, chains[c][1]] for m in range(NA_WIN_ROWS // 2)],
                                            axis=1) for c in ids]
        mx = [jnp.maximum(jnp.max(s_loc[c], axis=-1, keepdims=True), jnp.max(s_ctx[c], axis=-1, keepdims=True))
              for c in ids]
        p_loc = [jnp.exp(s_loc[c] - mx[c]) for c in ids]
        p_ctx = [jnp.exp(s_ctx[c] - mx[c]) for c in ids]
        den = [jnp.sum(p_loc[c], axis=-1, keepdims=True) + jnp.sum(p_ctx[c], axis=-1, keepdims=True) for c in ids]
        o_ctx = per_chain([_dot(jnp.concatenate([p_ctx[jr * n_pairs + p] for jr in range(rpi)], axis=0).astype(BF16),
                                vc_ref[0, :, pcs[p]]) for p in range(n_pairs)])
        o = [_dot(p_loc[c].astype(BF16), v_ref[0, pl.ds(tok0[c], win), cs[c]]) + o_ctx[c] for c in ids]
        for c in ids:
            on = o[c] / den[c]
            o_ref[0, pl.ds(q0[c], GRID_W), cs[c]] = jnp.where(first_head, on[:GRID_W], on[GRID_W:]).astype(o_ref.dtype)
        return carry

    lax.fori_loop(0, tr // rpi, row_body, 0)


def _attn_call(qkv, kv_ctx, bias_tab, *, tr, rpi):
    bsz, n, _ = qkv.shape
    rows = n // GRID_W
    lc = kv_ctx.shape[1]
    assert rows >= NA_WIN_ROWS and rows % tr == 0 and tr % rpi == 0
    return pl.pallas_call(
        functools.partial(_attn_kernel, tr=tr, rows=rows, rpi=rpi),
        grid=(bsz, rows // tr),
        in_specs=[
            pl.BlockSpec((1, tr * GRID_W, NA_W), lambda b, i: (b, i, 0)),
            pl.BlockSpec((1, n, NA_W), lambda b, i: (b, 0, 1)),
            pl.BlockSpec((1, n, NA_W), lambda b, i: (b, 0, 2)),
            pl.BlockSpec((1, lc, NA_W), lambda b, i: (b, 0, 0)),
            pl.BlockSpec((1, lc, NA_W), lambda b, i: (b, 0, 1)),
            pl.BlockSpec(bias_tab.shape, lambda b, i: (0, 0, 0, 0)),
        ],
        out_specs=pl.BlockSpec((1, tr * GRID_W, NA_W), lambda b, i: (b, i, 0)),
        out_shape=jax.ShapeDtypeStruct((bsz, n, NA_W), BF16),
        compiler_params=_cparams("parallel", "arbitrary"),
        name="neighbourhood_attention",
    )(qkv, qkv, qkv, kv_ctx, kv_ctx, bias_tab)


def _na_bias_table(rpb):
    col = np.arange(GRID_W)
    c0 = np.clip(col - NA_WIN_COLS // 2, 0, GRID_W - NA_WIN_COLS)
    rel = col[None, :] - col[:, None] + NA_WIN_COLS - 1
    valid = (col[None, :] >= c0[:, None]) & (col[None, :] < c0[:, None] + NA_WIN_COLS)
    n_rel = 2 * NA_WIN_COLS - 1
    onehot = ((np.arange(n_rel)[:, None, None] == rel[None]) & valid[None]).astype(np.float32)
    dense = jnp.einsum("rhk,kwc->rhwc", rpb.astype(F32).transpose(1, 0, 2), onehot,
                       precision=lax.Precision.HIGHEST)
    dense = jnp.where(valid, dense, MASK_VALUE)
    two = jnp.concatenate([dense[:-1], dense[1:]], axis=-1)
    nr, h = two.shape[0], two.shape[1]
    return two.reshape(nr, h // 2, 2 * GRID_W, 2 * GRID_W)


def _gdn_prep_kernel(x_ref, xp_ref, xn_ref, ab_ref, cw_ref, nea_ref, dtb_ref, cos_ref, sin_ref, mats_ref,
                     q_ref, k_ref, v_ref, gb_ref, gt_ref, *, t, use_rope):
    i = pl.program_id(1)
    halo = SUBLANES
    pad = GDN_CONV // 2
    hd = GDN_HEAD_DIM
    taps = [j for j in range(GDN_CONV) if j != pad]
    n_tap = len(taps)
    shift_stack = mats_ref[0:n_tap].reshape(n_tap * t, t)
    first = i > 0
    last = i < pl.num_programs(1) - 1
    zeros = jnp.zeros((halo, hd), F32)
    lane = lax.broadcasted_iota(jnp.int32, (t, hd), 1)
    half0 = (lane % (hd // 2)) < (hd // 4)
    quarter = hd // 4
    shifted2 = None
    for cb in range(3 * GDN_HEADS):
        cs = slice(cb * hd, (cb + 1) * hd)
        xb = x_ref[0, :, cs]
        cw = cw_ref[:, cs]
        if cb % 2 == 0:
            shifted2 = _dot(shift_stack, x_ref[0, :, cb * hd:(cb + 2) * hd].astype(BF16))
        shifted = shifted2[:, (cb % 2) * hd:(cb % 2 + 1) * hd]
        acc = cw[pad:pad + 1, :] * xb
        for n, j in enumerate(taps):
            acc = acc + cw[j:j + 1, :] * shifted[n * t:(n + 1) * t]
        top = jnp.concatenate([jnp.where(first, xp_ref[0, :, cs], 0.0), zeros], axis=0)
        bot = jnp.concatenate([zeros, jnp.where(last, xn_ref[0, :, cs], 0.0)], axis=0)
        e_top = sum(cw[j:j + 1, :] * top[halo - pad + j:2 * halo - pad + j, :] for j in range(pad))
        e_bot = sum(cw[j:j + 1, :] * bot[j - pad:j - pad + halo, :] for j in range(pad + 1, GDN_CONV))
        acc = jnp.concatenate([acc[:halo] + e_top, acc[halo:t - halo], acc[t - halo:] + e_bot], axis=0)
        y = _silu(acc)
        which, h = divmod(cb, GDN_HEADS)
        if which < 2:
            y = y * lax.rsqrt(jnp.sum(y * y, axis=-1, keepdims=True) + NORM_EPS)
            if use_rope:
                partner = jnp.where(half0, pltpu.roll(y, hd - quarter, 1), pltpu.roll(y, quarter, 1))
                y = y * cos_ref[...] + partner * sin_ref[...]
        if which == 0:
            y = y * (hd ** -0.5)
        (q_ref, k_ref, v_ref)[which][0, :, h * hd:(h + 1) * hd] = y

    ab = ab_ref[0]
    nh2 = 2 * GDN_HEADS
    za = ab + dtb_ref[...]
    softplus = jnp.maximum(za, 0.0) + jnp.log(1.0 + jnp.exp(-jnp.abs(za)))
    g = nea_ref[...] * softplus
    beta = _sigmoid(ab)
    n_tap = GDN_CONV - 1
    tri_f, tri_b, same = mats_ref[n_tap], mats_ref[n_tap + 1], mats_ref[n_tap + 2]
    lane_g = lax.broadcasted_iota(jnp.int32, (t, LANES), 1)
    gcum = jnp.where(lane_g < GDN_HEADS, _dot_exact_lhs(tri_f, g), _dot_exact_lhs(tri_b, g))
    gtot = pltpu.roll(_dot_exact_lhs(same, g), 2 * nh2, 1)
    gb_ref[0] = jnp.where(lane_g < nh2, gcum,
                          jnp.where(lane_g < 2 * nh2, beta, jnp.where(lane_g < 3 * nh2, gtot, 0.0)))
    gt_ref[0] = jnp.transpose(jnp.where(lane_g < nh2, gcum, 0.0))[0:nh2, :]


def _prep_matrices(t, chunk):
    r = np.arange(t)[:, None]
    c = np.arange(t)[None, :]
    pad = GDN_CONV // 2
    shifts = [c == r + (j - pad) for j in range(GDN_CONV) if j != pad]
    same = (r // chunk) == (c // chunk)
    mats = shifts + [same & (c <= r), same & (c >= r), same]
    return jnp.asarray(np.stack(mats).astype(np.float32), dtype=BF16)


def _gdn_prep_call(qkv_g, ab, conv_w, nea_row, dtb_row, cos_t, sin_t, *, use_rope, t, chunk):
    bsz, n, c3 = qkv_g.shape
    nb8 = n // SUBLANES
    tb8 = t // SUBLANES
    f3 = jax.ShapeDtypeStruct((bsz, n, GDN_W), F32)
    mats = _prep_matrices(t, chunk)
    return pl.pallas_call(
        functools.partial(_gdn_prep_kernel, t=t, use_rope=use_rope),
        grid=(bsz, n // t),
        in_specs=[
            pl.BlockSpec((1, t, c3), lambda b, i: (b, i, 0)),
            pl.BlockSpec((1, SUBLANES, c3), lambda b, i: (b, jnp.maximum(i * tb8 - 1, 0), 0)),
            pl.BlockSpec((1, SUBLANES, c3), lambda b, i: (b, jnp.minimum((i + 1) * tb8, nb8 - 1), 0)),
            pl.BlockSpec((1, t, LANES), lambda b, i: (b, i, 0)),
            pl.BlockSpec(conv_w.shape, lambda b, i: (0, 0)),
            pl.BlockSpec((1, LANES), lambda b, i: (0, 0)),
            pl.BlockSpec((1, LANES), lambda b, i: (0, 0)),
            pl.BlockSpec((t, GDN_HEAD_DIM), lambda b, i: (i, 0)),
            pl.BlockSpec((t, GDN_HEAD_DIM), lambda b, i: (i, 0)),
            pl.BlockSpec(mats.shape, lambda b, i: (0, 0, 0)),
        ],
        out_specs=[
            pl.BlockSpec((1, t, GDN_W), lambda b, i: (b, i, 0)),
            pl.BlockSpec((1, t, GDN_W), lambda b, i: (b, i, 0)),
            pl.BlockSpec((1, t, GDN_W), lambda b, i: (b, i, 0)),
            pl.BlockSpec((1, t, LANES), lambda b, i: (b, i, 0)),
            pl.BlockSpec((1, 2 * GDN_HEADS, t), lambda b, i: (b, 0, i)),
        ],
        out_shape=[f3, f3, f3,
                   jax.ShapeDtypeStruct((bsz, n, LANES), F32),
                   jax.ShapeDtypeStruct((bsz, 2 * GDN_HEADS, n), F32)],
        compiler_params=_cparams("parallel", "parallel"),
        name="gdn_prepare",
    )(qkv_g, qkv_g, qkv_g, ab, conv_w, nea_row, dtb_row, cos_t, sin_t, mats)


TRI_BASE = 8


def _slabs(x, s, second):
    start = s if second else 0
    return jnp.concatenate([x[i:i + s] for i in range(start, x.shape[0], 2 * s)], axis=0)


def _interleave(upd, other, s, second):
    parts = []
    for p in range(other.shape[0] // (2 * s)):
        u = upd[s * p:s * (p + 1)]
        if second:
            parts += [other[2 * s * p:2 * s * p + s], u]
        else:
            parts += [u, other[2 * s * p + s:2 * s * (p + 1)]]
    return jnp.concatenate(parts, axis=0)


def _unit_tri_solve(ms, rhss, lower):
    c = ms[0].shape[0]
    n = len(ms)
    row = lax.broadcasted_iota(jnp.int32, (c, c), 0)
    col = lax.broadcasted_iota(jnp.int32, (c, c), 1)
    eye = (row == col).astype(F32)
    same = lambda s: (row // s) == (col // s)
    base = same(TRI_BASE)
    ds = [jnp.where(base, m, 0.0) for m in ms]
    ps = [eye - d for d in ds]
    dbs = [d.astype(BF16) for d in ds]
    qs = [_dot(db, db) for db in dbs]
    n_iter = int(math.log2(TRI_BASE)) - 1
    for it in range(n_iter):
        qbs = [q.astype(BF16) for q in qs]
        if it < n_iter - 1:
            yqs = [_dot(qb, jnp.concatenate([p, q], axis=1).astype(BF16)) for qb, p, q in zip(qbs, ps, qs)]
            ps = [p + yq[:, :c] for p, yq in zip(ps, yqs)]
            qs = [yq[:, c:] for yq in yqs]
        else:
            ys = [_dot(qb, p.astype(BF16)) for qb, p in zip(qbs, ps)]
            ps = [p + y for p, y in zip(ps, ys)]
    ts = ps
    zeros = jnp.zeros((c, c), F32)
    s = TRI_BASE
    while 2 * s < c:
        off_diag = same(2 * s) & jnp.logical_not(same(s))
        l_rows = [_slabs(jnp.where(off_diag, m, 0.0), s, lo).astype(BF16) for m, lo in zip(ms, lower)]
        t_rows = [_slabs(t, s, lo) for t, lo in zip(ts, lower)]
        ys = [_dot(l, t.astype(BF16)) for l, t in zip(l_rows, ts)]
        y_full = [_interleave(y, zeros, s, lo).astype(BF16) for y, lo in zip(ys, lower)]
        zs = [_dot(tr.astype(BF16), yf) for tr, yf in zip(t_rows, y_full)]
        ts = [_interleave(tr - z, t, s, lo) for tr, z, t, lo in zip(t_rows, zs, ts, lower)]
        s *= 2
    hc = c // 2
    off_diag = jnp.logical_not(same(hc))
    ind = [slice(0, hc) if lo else slice(hc, c) for lo in lower]
    dep = [slice(hc, c) if lo else slice(0, hc) for lo in lower]

    def place(x_ind, x_dep, lo):
        return jnp.concatenate([x_ind, x_dep] if lo else [x_dep, x_ind], axis=0)

    x1 = [_dot(t[i].astype(BF16), r.astype(BF16)) for t, i, r in zip(ts, ind, rhss)]
    mixed = [place(x, r[dp], lo).astype(BF16) for x, r, dp, lo in zip(x1, rhss, dep, lower)]
    ys = [_dot(jnp.where(off_diag, m, 0.0)[dp].astype(BF16), mx) for m, dp, mx in zip(ms, dep, mixed)]
    mixed2 = [place(r[i], r[dp] - y, lo).astype(BF16) for r, i, dp, y, lo in zip(rhss, ind, dep, ys, lower)]
    x2 = [_dot(t[dp].astype(BF16), mx) for t, dp, mx in zip(ts, dep, mixed2)]
    return [place(a, b, lo) for a, b, lo in zip(x1, x2, lower)]


def _gdn_scan_kernel(*refs, tb, chunk, with_output):
    (qf, kf, vf, gf, gtf, qb, kb, vb, gb, gtb, s0_ref) = refs[:11]
    if with_output:
        of_ref, ob_ref, s_ref = refs[11:]
        o_refs = (of_ref, ob_ref)
    else:
        sfin_ref, s_ref = refs[11:]
    ins = ((qf, kf, vf, gf, gtf), (qb, kb, vb, gb, gtb))
    j = pl.program_id(1)
    c = chunk
    nch = tb // c
    hd = GDN_HEAD_DIM

    @pl.when(j == 0)
    def _():
        s_ref[...] = s0_ref[0]

    row = lax.broadcasted_iota(jnp.int32, (c, c), 0)
    col = lax.broadcasted_iota(jnp.int32, (c, c), 1)
    incl = (col <= row, col >= row)
    strict = (col < row, col > row)
    nh2 = 2 * GDN_HEADS
    heads = [(d, h) for d in range(2) for h in range(GDN_HEADS)]
    units = [(ci, d, h) for ci in range(nch) for d, h in heads]
    off = lambda ci, d: (ci if d == 0 else nch - 1 - ci) * c
    assert c <= LANES

    gblk = {(ci, d): ins[d][3][0, off(ci, d):off(ci, d) + c, :] for ci in range(nch) for d in range(2)}
    gtblk = {(ci, d): ins[d][4][0, :, off(ci, d):off(ci, d) + c] for ci in range(nch) for d in range(2)}

    def lane_bcast(ci, d, h, o):
        dh = d * GDN_HEADS + h
        return jnp.broadcast_to(gblk[ci, d][:, o + dh:o + dh + 1], (c, LANES))

    g_b = [lane_bcast(ci, d, h, 0) for ci, d, h in units]
    beta = [lane_bcast(ci, d, h, nh2) for ci, d, h in units]
    g_last = [lane_bcast(ci, d, h, 2 * nh2) for ci, d, h in units]
    g_row = [gtblk[ci, d][d * GDN_HEADS + h:d * GDN_HEADS + h + 1, :] for ci, d, h in units]
    ld = lambda idx: [ins[d][idx][0, off(ci, d):off(ci, d) + c, h * hd:(h + 1) * hd] for ci, d, h in units]
    q, k, v = ld(0), ld(1), ld(2)
    decay = [jnp.exp(jnp.where(incl[d], gb_[:, :c] - gr, MASK_VALUE)) for (ci, d, h), gb_, gr in zip(units, g_b, g_row)]
    e_g = [jnp.exp(x) for x in g_b]
    k_beta = [a * b for a, b in zip(k, beta)]
    k16 = [a.astype(BF16) for a in k]
    lhs = [jnp.concatenate([a, b], axis=0) for a, b in zip(k_beta, q)] if with_output else k_beta
    kq = [_dot_nt(a.astype(BF16), b) for a, b in zip(lhs, k16)]
    m = [jnp.where(strict[d], a[:c] * b, 0.0) for (ci, d, h), a, b in zip(units, kq, decay)]
    rhs = [jnp.concatenate([a * b, kb_ * e], axis=1) for a, b, kb_, e in zip(v, beta, k_beta, e_g)]
    uw = _unit_tri_solve(m, rhs, [d == 0 for ci, d, h in units])
    k_tail = [a * jnp.exp(gl - gb_) for a, gl, gb_ in zip(k, g_last, g_b)]
    g_tot = [jnp.exp(gl[0:1, :]) for gl in g_last]

    s = [s_ref[d, h] for d, h in heads]
    nhd = len(heads)
    for ci in range(nch):
        u0 = ci * nhd
        sel = lambda xs_: xs_[u0:u0 + nhd]
        s16 = [a.astype(BF16) for a in s]
        lhs = ([jnp.concatenate([a[:, hd:], b * e], axis=0) for a, b, e in zip(sel(uw), sel(q), sel(e_g))]
               if with_output else [a[:, hd:] for a in sel(uw)])
        xs = [_dot(a.astype(BF16), b) for a, b in zip(lhs, s16)]
        v_new = [a[:, :hd] - b[:c] for a, b in zip(sel(uw), xs)]
        if with_output:
            o_local = [_dot((a[c:] * dcy).astype(BF16), b.astype(BF16))
                       for a, dcy, b in zip(sel(kq), sel(decay), v_new)]
            for (d, h), a, b in zip(heads, xs, o_local):
                o_refs[d][0, off(ci, d):off(ci, d) + c, h * hd:(h + 1) * hd] = (a[c:] + b).astype(o_refs[d].dtype)
        ktv = [_dot_tn(a.astype(BF16), b.astype(BF16)) for a, b in zip(sel(k_tail), v_new)]
        s = [a * gt_ + b for a, gt_, b in zip(s, sel(g_tot), ktv)]
    for (d, h), a in zip(heads, s):
        s_ref[d, h] = a

    if not with_output:
        @pl.when(j == pl.num_programs(1) - 1)
        def _():
            sfin_ref[0] = s_ref[...]


def _gdn_scan_call(q, k, v, gbeta, gt, s0, *, tb, chunk, with_output):
    bsz, n, _ = q.shape
    nblk = n // tb
    fwd = lambda b, j: (b, j, 0)
    bwd = lambda b, j: (b, nblk - 1 - j, 0)
    fwd_t = lambda b, j: (b, 0, j)
    bwd_t = lambda b, j: (b, 0, nblk - 1 - j)

    def specs(idx, idx_t):
        return [pl.BlockSpec((1, tb, GDN_W), idx)] * 3 + [pl.BlockSpec((1, tb, LANES), idx),
                                                          pl.BlockSpec((1, 2 * GDN_HEADS, tb), idx_t)]

    state_spec = pl.BlockSpec((1, 2, GDN_HEADS, GDN_HEAD_DIM, GDN_HEAD_DIM), lambda b, j: (b, 0, 0, 0, 0))
    if with_output:
        out_specs = [pl.BlockSpec((1, tb, GDN_W), fwd), pl.BlockSpec((1, tb, GDN_W), bwd)]
        out_shape = [jax.ShapeDtypeStruct((bsz, n, GDN_W), BF16)] * 2
    else:
        out_specs = state_spec
        out_shape = jax.ShapeDtypeStruct(s0.shape, F32)
    return pl.pallas_call(
        functools.partial(_gdn_scan_kernel, tb=tb, chunk=chunk, with_output=with_output),
        grid=(bsz, nblk),
        in_specs=specs(fwd, fwd_t) + specs(bwd, bwd_t) + [state_spec],
        out_specs=out_specs,
        out_shape=out_shape,
        scratch_shapes=[pltpu.VMEM((2, GDN_HEADS, GDN_HEAD_DIM, GDN_HEAD_DIM), F32)],
        compiler_params=_cparams("parallel", "arbitrary"),
        name="gdn_scan_latent" if with_output else "gdn_scan_context",
    )(q, k, v, gbeta, gt, q, k, v, gbeta, gt, s0)


def _merge_kernel(x_ref, m_ref, g_ref, b_ref, oa_ref, of_ref, ob_ref, z_ref, gab_ref, nw_ref,
                  wpa_ref, wpb_ref, wo_ref, o_ref, *, alpha, n_sub):
    nw = nw_ref[...]
    d = x_ref.shape[-1]
    gate = m_ref[0, 5:6, :]
    ts = x_ref.shape[1] // n_sub
    rows = [slice(i * ts, (i + 1) * ts) for i in range(n_sub)]

    def gdn_out(r):
        o = of_ref[0, r, :].astype(F32) + ob_ref[0, r, :].astype(F32)
        parts = []
        for h in range(GDN_HEADS):
            oh = o[:, h * GDN_HEAD_DIM:(h + 1) * GDN_HEAD_DIM]
            parts.append(oh * lax.rsqrt(jnp.mean(oh * oh, axis=-1, keepdims=True) + NORM_EPS) * nw)
        return (jnp.concatenate(parts, axis=1) * _silu(z_ref[0, r, :].astype(F32))).astype(BF16)

    o_b = [gdn_out(r) for r in rows]
    ya = [_dot(oa_ref[0, r, :], wpa_ref[...]) for r in rows]
    yb = [_dot(ob, wpb_ref[...]) for ob in o_b]
    y = [(_sigmoid(gab_ref[0, r, :d].astype(F32)) * a
          + _sigmoid(gab_ref[0, r, d:].astype(F32)) * b).astype(BF16) for r, a, b in zip(rows, ya, yb)]
    yo = [_dot(yi, wo_ref[...]) for yi in y]
    for r, yoi in zip(rows, yo):
        o_ref[0, r, :] = _ln(alpha * x_ref[0, r, :] + gate * yoi) * g_ref[1:2, :] + b_ref[1:2, :]


def _merge_call(x, mod3, ln_g, ln_b, o_a, o_f, o_b, z, gab, norm_w, w_pa, w_pb, w_o, *, alpha, tm, n_sub):
    bsz, n, d = x.shape
    tok = lambda w: pl.BlockSpec((1, tm, w), lambda b, i: (b, i, 0))
    full = lambda a: pl.BlockSpec(a.shape, lambda b, i: (0,) * a.ndim)
    return pl.pallas_call(
        functools.partial(_merge_kernel, alpha=alpha, n_sub=n_sub),
        grid=(bsz, n // tm),
        in_specs=[tok(d), pl.BlockSpec((1, N_MOD, d), lambda b, i: (b, 0, 0)), full(ln_g), full(ln_b),
                  tok(NA_W), tok(GDN_W), tok(GDN_W), tok(GDN_W), tok(2 * d), full(norm_w),
                  full(w_pa), full(w_pb), full(w_o)],
        out_specs=tok(d),
        out_shape=jax.ShapeDtypeStruct((bsz, n, d), F32),
        compiler_params=_cparams("parallel", "parallel"),
        name="branch_merge",
    )(x, mod3, ln_g, ln_b, o_a, o_f, o_b, z, gab, norm_w, w_pa, w_pb, w_o)


def _rope_tables(n_tok):
    n_freq = GDN_HEAD_DIM // 4
    freqs = ROPE_THETA ** (-np.arange(n_freq, dtype=np.float64) / n_freq)
    t = np.arange(n_tok)
    pos = np.stack([t // GRID_W, t % GRID_W], axis=-1).astype(np.float64)
    ang = pos[:, :, None] * freqs
    cos = np.cos(ang)
    sin = np.sin(ang)
    cos_t = np.concatenate([cos, cos], axis=-1).reshape(n_tok, GDN_HEAD_DIM)
    sin_t = np.concatenate([-sin, sin], axis=-1).reshape(n_tok, GDN_HEAD_DIM)
    return jnp.asarray(cos_t, F32), jnp.asarray(sin_t, F32)


def _pick_tile(n, pref):
    t = min(n, pref)
    assert n % t == 0
    return t


def kernel(x, c, ctx, c_ctx, w_ada, b_ada, ln_g, ln_b, ffn1_w_in, ffn1_w_out, w_in, na_rpb, gdn_conv_w,
           gdn_a_log, gdn_dt_bias, gdn_norm_w, w_pa, w_pb, w_o, ffn2_w_in, ffn2_w_out):
    depth = w_ada.shape[0]
    assert depth == 1, "context-update path of non-final layers is not implemented"
    bsz, n, d = x.shape
    lc = ctx.shape[1]
    alpha = (2 * depth) ** 0.25
    layer = 0

    n_rows = -(-(bsz + 1) // SUBLANES) * SUBLANES
    cc = jnp.zeros((n_rows, d), F32).at[:bsz].set(c).at[bsz].set(c_ctx)
    mod3 = _mod_call(cc, w_ada[layer], b_ada[layer], tn=MOD_COLS).reshape(n_rows, N_MOD, d)

    g_ln, b_ln = ln_g[layer], ln_b[layer]
    w1u, w1d = ffn1_w_in[layer].astype(BF16), ffn1_w_out[layer].astype(BF16)
    dense = dict(tm=_pick_tile(n, DENSE_TOKENS), n_sub=DENSE_SUBTILES)
    dense_ctx = dict(tm=_pick_tile(bsz * lc, DENSE_TOKENS), n_sub=DENSE_SUBTILES)
    x1 = _ffn_call(x, mod3, g_ln, b_ln, w1u, w1d, sub=0, alpha=alpha, mod_row=None, **dense)
    ctx_flat = ctx.reshape(1, bsz * lc, d)
    ctx1 = _ffn_call(ctx_flat, mod3, g_ln, b_ln, w1u, w1d, sub=0, alpha=alpha, mod_row=bsz, **dense_ctx)

    w_a, w_g, w_z, w_gab, w_ab = _regroup_w_in(w_in[layer])
    qkv_a, qkv_g, z, gab, ab = _proj_call(x1, mod3, (w_a, w_g, w_z, w_gab, w_ab), (BF16, F32, BF16, BF16, F32),
                                          mod_row=None, **dense)
    kv_a_c, qkv_g_c, ab_c = (t.reshape(bsz, lc, t.shape[-1]) for t in _proj_call(
        ctx1, mod3, (w_a[:, NA_W:], w_g, w_ab), (BF16, F32, F32), mod_row=bsz, **dense_ctx))

    o_a = _attn_call(qkv_a, kv_a_c, _na_bias_table(na_rpb[layer]), tr=ATTN_ROWS, rpi=ATTN_ROWS_PER_ITER)

    nea = jnp.zeros((1, LANES), F32).at[0, :2 * GDN_HEADS].set(-jnp.exp(gdn_a_log[layer].astype(F32)).reshape(-1))
    dtb = jnp.zeros((1, LANES), F32).at[0, :2 * GDN_HEADS].set(gdn_dt_bias[layer].astype(F32).reshape(-1))
    cw = gdn_conv_w[layer]
    cos_t, sin_t = _rope_tables(n)
    tp = _pick_tile(n, GDN_TOKENS)
    tpc = _pick_tile(lc, GDN_TOKENS)
    q_l, k_l, v_l, gb_l, gt_l = _gdn_prep_call(qkv_g, ab, cw, nea, dtb, cos_t, sin_t, use_rope=True, t=tp,
                                                chunk=GDN_CHUNK)
    q_c, k_c, v_c, gb_c, gt_c = _gdn_prep_call(qkv_g_c, ab_c, cw, nea, dtb, cos_t[:lc], sin_t[:lc], use_rope=False,
                                                t=tpc, chunk=GDN_CHUNK)
    s0 = jnp.zeros((bsz, 2, GDN_HEADS, GDN_HEAD_DIM, GDN_HEAD_DIM), F32)
    s_ctx = _gdn_scan_call(q_c, k_c, v_c, gb_c, gt_c, s0, tb=tpc, chunk=GDN_CHUNK, with_output=False)
    o_f, o_b = _gdn_scan_call(q_l, k_l, v_l, gb_l, gt_l, s_ctx, tb=_pick_tile(n, GDN_SCAN_TOKENS), chunk=GDN_CHUNK,
                              with_output=True)

    x2 = _merge_call(x1, mod3, g_ln, b_ln, o_a, o_f, o_b, z, gab, gdn_norm_w[layer].reshape(1, GDN_HEAD_DIM),
                     w_pa[layer].astype(BF16), w_pb[layer].astype(BF16), w_o[layer].astype(BF16),
                     alpha=alpha, **dense)

    w2u, w2d = ffn2_w_in[layer].astype(BF16), ffn2_w_out[layer].astype(BF16)
    return _ffn_call(x2, mod3, g_ln, b_ln, w2u, w2d, sub=2, alpha=alpha, mod_row=None, **dense)
```

```python
import functools
import math

import numpy as np
import jax
import jax.numpy as jnp
from jax import lax
from jax.experimental import pallas as pl
from jax.experimental.pallas import tpu as pltpu

F32 = jnp.float32
BF16 = jnp.bfloat16

GRID_W = 64
NA_HEADS = 8
NA_HEAD_DIM = 64
NA_WIN_ROWS = 8
NA_WIN_COLS = 16
GDN_HEADS = 4
GDN_HEAD_DIM = 128
GDN_CONV = 5
N_MOD = 9
ROPE_THETA = 10000.0
LN_EPS = 1e-6
NORM_EPS = 1e-6
NA_W = NA_HEADS * NA_HEAD_DIM
GDN_W = GDN_HEADS * GDN_HEAD_DIM

LANES = 128
SUBLANES = 8
VMEM_LIMIT_BYTES = 56 * 1024 * 1024
MASK_VALUE = -1e30
GDN_CHUNK = 128

DENSE_TOKENS = 512
DENSE_SUBTILES = 2
GDN_TOKENS = 2 * GDN_CHUNK
GDN_SCAN_TOKENS = 4 * GDN_CHUNK
ATTN_ROWS = 8
ATTN_ROWS_PER_ITER = 4
MOD_COLS = 9 * LANES


def _cparams(*sem):
    return pltpu.CompilerParams(dimension_semantics=sem, vmem_limit_bytes=VMEM_LIMIT_BYTES)


def _ln(x):
    mu = jnp.mean(x, axis=-1, keepdims=True)
    xc = x - mu
    var = jnp.mean(xc * xc, axis=-1, keepdims=True)
    return xc * lax.rsqrt(var + LN_EPS)


def _sigmoid(x):
    return 0.5 * jnp.tanh(0.5 * x) + 0.5


def _silu(x):
    h = 0.5 * x
    return h + h * jnp.tanh(h)


def _dot(a, b):
    return jnp.dot(a, b, preferred_element_type=F32)


def _dot_nt(a, b):
    return lax.dot_general(a, b, (((1,), (1,)), ((), ())), preferred_element_type=F32)


def _dot_tn(a, b):
    return lax.dot_general(a, b, (((0,), (0,)), ((), ())), preferred_element_type=F32)


def _dot_exact_lhs(a_bf16, x):
    hi = x.astype(BF16)
    r1 = x - hi.astype(F32)
    mid = r1.astype(BF16)
    lo = (r1 - mid.astype(F32)).astype(BF16)
    return _dot(a_bf16, hi) + _dot(a_bf16, mid) + _dot(a_bf16, lo)


def _mod_kernel(c_ref, w_ref, b_ref, o_ref):
    s = _silu(c_ref[...]).astype(BF16)
    o_ref[...] = _dot(s, w_ref[...].astype(BF16)) + b_ref[...]


def _mod_call(cc, w_ada, b_ada, *, tn):
    r, d = cc.shape
    n = w_ada.shape[1]
    return pl.pallas_call(
        _mod_kernel,
        grid=(n // tn,),
        in_specs=[
            pl.BlockSpec((r, d), lambda j: (0, 0)),
            pl.BlockSpec((d, tn), lambda j: (0, j)),
            pl.BlockSpec((1, tn), lambda j: (0, j)),
        ],
        out_specs=pl.BlockSpec((r, tn), lambda j: (0, j)),
        out_shape=jax.ShapeDtypeStruct((r, n), F32),
        compiler_params=_cparams("arbitrary"),
        name="adaln_modulation",
    )(cc, w_ada, b_ada.reshape(1, n))


def _ffn_kernel(x_ref, m_ref, g_ref, b_ref, wa_ref, wb_ref, wd_ref, o_ref, *, sub, alpha, n_sub):
    shift = m_ref[0, 3 * sub:3 * sub + 1, :]
    scale = m_ref[0, 3 * sub + 1:3 * sub + 2, :]
    gate = 0.5 * m_ref[0, 3 * sub + 2:3 * sub + 3, :]
    ts = x_ref.shape[1] // n_sub
    rows = [slice(i * ts, (i + 1) * ts) for i in range(n_sub)]
    u = [(_ln(x_ref[0, r, :]) * (1.0 + scale) + shift).astype(BF16) for r in rows]
    a = [_dot(ui, wa_ref[...]) for ui in u]
    b = [_dot(ui, wb_ref[...]) for ui in u]
    h = [(_silu(ai) * bi).astype(BF16) for ai, bi in zip(a, b)]
    y = [_dot(hi, wd_ref[...]) for hi in h]
    for r, yi in zip(rows, y):
        o_ref[0, r, :] = _ln(alpha * x_ref[0, r, :] + gate * yi) * g_ref[sub:sub + 1, :] + b_ref[sub:sub + 1, :]


def _ffn_call(x, mod3, ln_g, ln_b, w_up, w_down, *, sub, alpha, mod_row, tm, n_sub):
    bsz, n, d = x.shape
    f = w_down.shape[0]
    if mod_row is None:
        mod_idx = lambda b, i: (b, 0, 0)
    else:
        mod_idx = lambda b, i: (mod_row, 0, 0)
    once = pl.Buffered(1)
    return pl.pallas_call(
        functools.partial(_ffn_kernel, sub=sub, alpha=alpha, n_sub=n_sub),
        grid=(bsz, n // tm),
        in_specs=[
            pl.BlockSpec((1, tm, d), lambda b, i: (b, i, 0)),
            pl.BlockSpec((1, N_MOD, d), mod_idx),
            pl.BlockSpec(ln_g.shape, lambda b, i: (0, 0)),
            pl.BlockSpec(ln_b.shape, lambda b, i: (0, 0)),
            pl.BlockSpec((d, f), lambda b, i: (0, 0), pipeline_mode=once),
            pl.BlockSpec((d, f), lambda b, i: (0, 1), pipeline_mode=once),
            pl.BlockSpec((f, d), lambda b, i: (0, 0), pipeline_mode=once),
        ],
        out_specs=pl.BlockSpec((1, tm, d), lambda b, i: (b, i, 0)),
        out_shape=jax.ShapeDtypeStruct((bsz, n, d), F32),
        compiler_params=_cparams("parallel", "parallel"),
        name=f"ffn_sublayer_{sub}",
    )(x, mod3, ln_g, ln_b, w_up, w_up, w_down)


_O_NA = 3 * NA_W
_O_GDN = _O_NA + 3 * GDN_W
_O_Z = _O_GDN + GDN_W
_O_AB = _O_Z + 4 * GDN_HEADS


def _regroup_kernel(w_ref, wa_ref, wg_ref, wz_ref, wgab_ref, wab_ref):
    w = w_ref[...]
    wa_ref[...] = w[:, :_O_NA].astype(BF16)
    wg_ref[...] = w[:, _O_NA:_O_GDN].astype(BF16)
    wz_ref[...] = w[:, _O_GDN:_O_Z].astype(BF16)
    wgab_ref[...] = w[:, _O_AB:].astype(BF16)
    lane = lax.broadcasted_iota(jnp.int32, (w.shape[0], LANES), 1)
    wab_ref[...] = jnp.where(lane < _O_AB - _O_Z, w[:, _O_Z:_O_Z + LANES], 0.0).astype(BF16)


def _regroup_w_in(w_in, *, tr=128):
    d, n_in = w_in.shape
    widths = (_O_NA, _O_GDN - _O_NA, _O_Z - _O_GDN, n_in - _O_AB, LANES)
    return pl.pallas_call(
        _regroup_kernel,
        grid=(d // tr,),
        in_specs=[pl.BlockSpec((tr, n_in), lambda i: (i, 0))],
        out_specs=[pl.BlockSpec((tr, w), lambda i: (i, 0)) for w in widths],
        out_shape=[jax.ShapeDtypeStruct((d, w), BF16) for w in widths],
        compiler_params=_cparams("parallel"),
        name="regroup_w_in",
    )(w_in)

def _proj_kernel(x_ref, m_ref, *refs, n_sub):
    n_out = len(refs) // 2
    shift = m_ref[0, 3:4, :]
    scale = m_ref[0, 4:5, :]
    ts = x_ref.shape[1] // n_sub
    rows = [slice(i * ts, (i + 1) * ts) for i in range(n_sub)]
    u = [(_ln(x_ref[0, r, :]) * (1.0 + scale) + shift).astype(BF16) for r in rows]
    for w_ref, o_ref in zip(refs[:n_out], refs[n_out:]):
        for r, ui in zip(rows, u):
            o_ref[0, r, :] = _dot(ui, w_ref[...]).astype(o_ref.dtype)


def _proj_call(x, mod3, ws, dts, *, mod_row, tm, n_sub):
    bsz, n, d = x.shape
    if mod_row is None:
        mod_idx = lambda b, i: (b, 0, 0)
    else:
        mod_idx = lambda b, i: (mod_row, 0, 0)
    return pl.pallas_call(
        functools.partial(_proj_kernel, n_sub=n_sub),
        grid=(bsz, n // tm),
        in_specs=[pl.BlockSpec((1, tm, d), lambda b, i: (b, i, 0)),
                  pl.BlockSpec((1, N_MOD, d), mod_idx)]
                 + [pl.BlockSpec(w.shape, lambda b, i: (0, 0), pipeline_mode=pl.Buffered(1)) for w in ws],
        out_specs=[pl.BlockSpec((1, tm, w.shape[1]), lambda b, i: (b, i, 0)) for w in ws],
        out_shape=[jax.ShapeDtypeStruct((bsz, n, w.shape[1]), dt) for w, dt in zip(ws, dts)],
        compiler_params=_cparams("parallel", "parallel"),
        name="mixer_in_proj",
    )(x, mod3, *ws)


def _attn_kernel(q_ref, k_ref, v_ref, kc_ref, vc_ref, bias_ref, o_ref, *, tr, rows, rpi):
    i = pl.program_id(1)
    n_pairs = NA_HEADS // 2
    pw = 2 * NA_HEAD_DIM
    win = NA_WIN_ROWS * GRID_W
    lane = lax.broadcasted_iota(jnp.int32, (GRID_W, pw), 1)
    first_head = lane < NA_HEAD_DIM
    scale = NA_HEAD_DIM ** -0.5

    def row_body(it, carry):
        chains = [(jr, p) for jr in range(rpi) for p in range(n_pairs)]
        cs = [slice(p * pw, (p + 1) * pw) for jr, p in chains]
        q0, tok0, b0 = [], [], []
        for jr in range(rpi):
            rl = it * rpi + jr
            r = i * tr + rl
            r0 = jnp.clip(r - NA_WIN_ROWS // 2, 0, rows - NA_WIN_ROWS)
            tok0 += [pl.multiple_of(r0 * GRID_W, GRID_W)] * n_pairs
            q0 += [pl.multiple_of(rl * GRID_W, GRID_W)] * n_pairs
            b0 += [r0 - r + NA_WIN_ROWS - 1] * n_pairs
        ids = range(len(chains))
        qs = []
        for c in ids:
            q2 = q_ref[0, pl.ds(q0[c], GRID_W), cs[c]] * scale
            zero = jnp.zeros_like(q2)
            qs.append(jnp.concatenate([jnp.where(first_head, q2, zero), jnp.where(first_head, zero, q2)], axis=0))
        s_loc = [_dot_nt(qs[c], k_ref[0, pl.ds(tok0[c], win), cs[c]]) for c in ids]
        nq = 2 * GRID_W
        pcs = cs[:n_pairs]

        def per_chain(stacked):
            return [stacked[p][jr * nq:(jr + 1) * nq] for jr, p in chains]

        s_ctx = per_chain([_dot_nt(jnp.concatenate([qs[jr * n_pairs + p] for jr in range(rpi)], axis=0),
                                   kc_ref[0, :, pcs[p]]) for p in range(n_pairs)])
        s_loc = [s_loc[c] + jnp.concatenate([bias_ref[b0[c] + 2 * m, chains[c][1]] for m in range(NA_WIN_ROWS // 2)],
                                            axis=1) for c in ids]
        mx = [jnp.maximum(jnp.max(s_loc[c], axis=-1, keepdims=True), jnp.max(s_ctx[c], axis=-1, keepdims=True))
              for c in ids]
        p_loc = [jnp.exp(s_loc[c] - mx[c]) for c in ids]
        p_ctx = [jnp.exp(s_ctx[c] - mx[c]) for c in ids]
        den = [jnp.sum(p_loc[c], axis=-1, keepdims=True) + jnp.sum(p_ctx[c], axis=-1, keepdims=True) for c in ids]
        o_ctx = per_chain([_dot(jnp.concatenate([p_ctx[jr * n_pairs + p] for jr in range(rpi)], axis=0).astype(BF16),
                                vc_ref[0, :, pcs[p]]) for p in range(n_pairs)])
        o = [_dot(p_loc[c].astype(BF16), v_ref[0, pl.ds(tok0[c], win), cs[c]]) + o_ctx[c] for c in ids]
        for c in ids:
            on = o[c] / den[c]
            o_ref[0, pl.ds(q0[c], GRID_W), cs[c]] = jnp.where(first_head, on[:GRID_W], on[GRID_W:]).astype(o_ref.dtype)
        return carry

    lax.fori_loop(0, tr // rpi, row_body, 0)


def _attn_call(qkv, kv_ctx, bias_tab, *, tr, rpi):
    bsz, n, _ = qkv.shape
    rows = n // GRID_W
    lc = kv_ctx.shape[1]
    assert rows >= NA_WIN_ROWS and rows % tr == 0 and tr % rpi == 0
    return pl.pallas_call(
        functools.partial(_attn_kernel, tr=tr, rows=rows, rpi=rpi),
        grid=(bsz, rows // tr),
        in_specs=[
            pl.BlockSpec((1, tr * GRID_W, NA_W), lambda b, i: (b, i, 0)),
            pl.BlockSpec((1, n, NA_W), lambda b, i: (b, 0, 1)),
            pl.BlockSpec((1, n, NA_W), lambda b, i: (b, 0, 2)),
            pl.BlockSpec((1, lc, NA_W), lambda b, i: (b, 0, 0)),
            pl.BlockSpec((1, lc, NA_W), lambda b, i: (b, 0, 1)),
            pl.BlockSpec(bias_tab.shape, lambda b, i: (0, 0, 0, 0)),
        ],
        out_specs=pl.BlockSpec((1, tr * GRID_W, NA_W), lambda b, i: (b, i, 0)),
        out_shape=jax.ShapeDtypeStruct((bsz, n, NA_W), BF16),
        compiler_params=_cparams("parallel", "arbitrary"),
        name="neighbourhood_attention",
    )(qkv, qkv, qkv, kv_ctx, kv_ctx, bias_tab)


def _na_bias_table(rpb):
    n_heads, n_rel_rows, n_rel = rpb.shape
    rows = jnp.zeros((n_rel_rows, n_heads, LANES), F32).at[:, :, :n_rel].set(rpb.astype(F32).transpose(1, 0, 2))
    return pl.pallas_call(
        _na_bias_kernel,
        grid=(n_rel_rows - 1,),
        in_specs=[pl.BlockSpec(rows.shape, lambda r: (0, 0, 0))],
        out_specs=pl.BlockSpec((1, n_heads // 2, 2 * GRID_W, 2 * GRID_W), lambda r: (r, 0, 0, 0)),
        out_shape=jax.ShapeDtypeStruct((n_rel_rows - 1, n_heads // 2, 2 * GRID_W, 2 * GRID_W), F32),
        compiler_params=_cparams("parallel"),
        name="na_bias_table",
    )(rows)


def _na_bias_kernel(rows_ref, o_ref):
    r = pl.program_id(0)
    w = lax.broadcasted_iota(jnp.int32, (GRID_W, LANES), 0)
    lane = lax.broadcasted_iota(jnp.int32, (GRID_W, LANES), 1)
    c = lane % GRID_W
    c0 = jnp.clip(w - NA_WIN_COLS // 2, 0, GRID_W - NA_WIN_COLS)
    valid = (c >= c0) & (c < c0 + NA_WIN_COLS)
    for p in range(NA_HEADS // 2):
        for h2 in range(2):
            halves = []
            for i2 in range(2):
                v = jnp.broadcast_to(rows_ref[r + i2, 2 * p + h2:2 * p + h2 + 1, :], (GRID_W, LANES))
                shift = (LANES - (NA_WIN_COLS - 1) + i2 * GRID_W) % LANES
                halves.append(pltpu.roll(v, shift, 1, stride=1, stride_axis=0))
            tile = jnp.where(lane < GRID_W, halves[0], halves[1])
            o_ref[0, p, h2 * GRID_W:(h2 + 1) * GRID_W, :] = jnp.where(valid, tile, MASK_VALUE)


def _gdn_prep_kernel(x_ref, xp_ref, xn_ref, ab_ref, cw_ref, nea_ref, dtb_ref, cos_ref, sin_ref, mats_ref,
                     q_ref, k_ref, v_ref, gb_ref, gt_ref, *, t, use_rope):
    i = pl.program_id(1)
    halo = SUBLANES
    pad = GDN_CONV // 2
    hd = GDN_HEAD_DIM
    taps = [j for j in range(GDN_CONV) if j != pad]
    n_tap = len(taps)
    shift_stack = mats_ref[0:n_tap].reshape(n_tap * t, t)
    first = i > 0
    last = i < pl.num_programs(1) - 1
    zeros = jnp.zeros((halo, hd), F32)
    lane = lax.broadcasted_iota(jnp.int32, (t, hd), 1)
    half0 = (lane % (hd // 2)) < (hd // 4)
    quarter = hd // 4
    shifted2 = None
    for cb in range(3 * GDN_HEADS):
        cs = slice(cb * hd, (cb + 1) * hd)
        xb = x_ref[0, :, cs]
        cw = cw_ref[:, cs]
        if cb % 2 == 0:
            shifted2 = _dot(shift_stack, x_ref[0, :, cb * hd:(cb + 2) * hd].astype(BF16))
        shifted = shifted2[:, (cb % 2) * hd:(cb % 2 + 1) * hd]
        acc = cw[pad:pad + 1, :] * xb
        for n, j in enumerate(taps):
            acc = acc + cw[j:j + 1, :] * shifted[n * t:(n + 1) * t]
        top = jnp.concatenate([jnp.where(first, xp_ref[0, :, cs], 0.0), zeros], axis=0)
        bot = jnp.concatenate([zeros, jnp.where(last, xn_ref[0, :, cs], 0.0)], axis=0)
        e_top = sum(cw[j:j + 1, :] * top[halo - pad + j:2 * halo - pad + j, :] for j in range(pad))
        e_bot = sum(cw[j:j + 1, :] * bot[j - pad:j - pad + halo, :] for j in range(pad + 1, GDN_CONV))
        acc = jnp.concatenate([acc[:halo] + e_top, acc[halo:t - halo], acc[t - halo:] + e_bot], axis=0)
        y = _silu(acc)
        which, h = divmod(cb, GDN_HEADS)
        if which < 2:
            y = y * lax.rsqrt(jnp.sum(y * y, axis=-1, keepdims=True) + NORM_EPS)
            if use_rope:
                partner = jnp.where(half0, pltpu.roll(y, hd - quarter, 1), pltpu.roll(y, quarter, 1))
                y = y * cos_ref[...] + partner * sin_ref[...]
        if which == 0:
            y = y * (hd ** -0.5)
        (q_ref, k_ref, v_ref)[which][0, :, h * hd:(h + 1) * hd] = y

    ab = ab_ref[0]
    nh2 = 2 * GDN_HEADS
    za = ab + dtb_ref[...]
    softplus = jnp.maximum(za, 0.0) + jnp.log(1.0 + jnp.exp(-jnp.abs(za)))
    g = nea_ref[...] * softplus
    beta = _sigmoid(ab)
    n_tap = GDN_CONV - 1
    tri_f, tri_b, same = mats_ref[n_tap], mats_ref[n_tap + 1], mats_ref[n_tap + 2]
    lane_g = lax.broadcasted_iota(jnp.int32, (t, LANES), 1)
    gcum = jnp.where(lane_g < GDN_HEADS, _dot_exact_lhs(tri_f, g), _dot_exact_lhs(tri_b, g))
    gtot = pltpu.roll(_dot_exact_lhs(same, g), 2 * nh2, 1)
    gb_ref[0] = jnp.where(lane_g < nh2, gcum,
                          jnp.where(lane_g < 2 * nh2, beta, jnp.where(lane_g < 3 * nh2, gtot, 0.0)))
    gt_ref[0] = jnp.transpose(jnp.where(lane_g < nh2, gcum, 0.0))[0:nh2, :]


def _prep_matrices(t, chunk):
    r = np.arange(t)[:, None]
    c = np.arange(t)[None, :]
    pad = GDN_CONV // 2
    shifts = [c == r + (j - pad) for j in range(GDN_CONV) if j != pad]
    same = (r // chunk) == (c // chunk)
    mats = shifts + [same & (c <= r), same & (c >= r), same]
    return jnp.asarray(np.stack(mats).astype(np.float32), dtype=BF16)


def _gdn_prep_call(qkv_g, ab, conv_w, nea_row, dtb_row, cos_t, sin_t, *, use_rope, t, chunk):
    bsz, n, c3 = qkv_g.shape
    nb8 = n // SUBLANES
    tb8 = t // SUBLANES
    f3 = jax.ShapeDtypeStruct((bsz, n, GDN_W), F32)
    mats = _prep_matrices(t, chunk)
    return pl.pallas_call(
        functools.partial(_gdn_prep_kernel, t=t, use_rope=use_rope),
        grid=(bsz, n // t),
        in_specs=[
            pl.BlockSpec((1, t, c3), lambda b, i: (b, i, 0)),
            pl.BlockSpec((1, SUBLANES, c3), lambda b, i: (b, jnp.maximum(i * tb8 - 1, 0), 0)),
            pl.BlockSpec((1, SUBLANES, c3), lambda b, i: (b, jnp.minimum((i + 1) * tb8, nb8 - 1), 0)),
            pl.BlockSpec((1, t, LANES), lambda b, i: (b, i, 0)),
            pl.BlockSpec(conv_w.shape, lambda b, i: (0, 0)),
            pl.BlockSpec((1, LANES), lambda b, i: (0, 0)),
            pl.BlockSpec((1, LANES), lambda b, i: (0, 0)),
            pl.BlockSpec((t, GDN_HEAD_DIM), lambda b, i: (i, 0)),
            pl.BlockSpec((t, GDN_HEAD_DIM), lambda b, i: (i, 0)),
            pl.BlockSpec(mats.shape, lambda b, i: (0, 0, 0)),
        ],
        out_specs=[
            pl.BlockSpec((1, t, GDN_W), lambda b, i: (b, i, 0)),
            pl.BlockSpec((1, t, GDN_W), lambda b, i: (b, i, 0)),
            pl.BlockSpec((1, t, GDN_W), lambda b, i: (b, i, 0)),
            pl.BlockSpec((1, t, LANES), lambda b, i: (b, i, 0)),
            pl.BlockSpec((1, 2 * GDN_HEADS, t), lambda b, i: (b, 0, i)),
        ],
        out_shape=[f3, f3, f3,
                   jax.ShapeDtypeStruct((bsz, n, LANES), F32),
                   jax.ShapeDtypeStruct((bsz, 2 * GDN_HEADS, n), F32)],
        compiler_params=_cparams("parallel", "parallel"),
        name="gdn_prepare",
    )(qkv_g, qkv_g, qkv_g, ab, conv_w, nea_row, dtb_row, cos_t, sin_t, mats)


TRI_BASE = 8


def _slabs(x, s, second):
    start = s if second else 0
    return jnp.concatenate([x[i:i + s] for i in range(start, x.shape[0], 2 * s)], axis=0)


def _interleave(upd, other, s, second):
    parts = []
    for p in range(other.shape[0] // (2 * s)):
        u = upd[s * p:s * (p + 1)]
        if second:
            parts += [other[2 * s * p:2 * s * p + s], u]
        else:
            parts += [u, other[2 * s * p + s:2 * s * (p + 1)]]
    return jnp.concatenate(parts, axis=0)


def _unit_tri_solve(ms, rhss, lower):
    c = ms[0].shape[0]
    n = len(ms)
    row = lax.broadcasted_iota(jnp.int32, (c, c), 0)
    col = lax.broadcasted_iota(jnp.int32, (c, c), 1)
    eye = (row == col).astype(F32)
    same = lambda s: (row // s) == (col // s)
    base = same(TRI_BASE)
    ds = [jnp.where(base, m, 0.0) for m in ms]
    ps = [eye - d for d in ds]
    dbs = [d.astype(BF16) for d in ds]
    qs = [_dot(db, db) for db in dbs]
    n_iter = int(math.log2(TRI_BASE)) - 1
    for it in range(n_iter):
        qbs = [q.astype(BF16) for q in qs]
        if it < n_iter - 1:
            yqs = [_dot(qb, jnp.concatenate([p, q], axis=1).astype(BF16)) for qb, p, q in zip(qbs, ps, qs)]
            ps = [p + yq[:, :c] for p, yq in zip(ps, yqs)]
            qs = [yq[:, c:] for yq in yqs]
        else:
            ys = [_dot(qb, p.astype(BF16)) for qb, p in zip(qbs, ps)]
            ps = [p + y for p, y in zip(ps, ys)]
    ts = ps
    zeros = jnp.zeros((c, c), F32)
    s = TRI_BASE
    while 2 * s < c:
        off_diag = same(2 * s) & jnp.logical_not(same(s))
        l_rows = [_slabs(jnp.where(off_diag, m, 0.0), s, lo).astype(BF16) for m, lo in zip(ms, lower)]
        t_rows = [_slabs(t, s, lo) for t, lo in zip(ts, lower)]
        ys = [_dot(l, t.astype(BF16)) for l, t in zip(l_rows, ts)]
        y_full = [_interleave(y, zeros, s, lo).astype(BF16) for y, lo in zip(ys, lower)]
        zs = [_dot(tr.astype(BF16), yf) for tr, yf in zip(t_rows, y_full)]
        ts = [_interleave(tr - z, t, s, lo) for tr, z, t, lo in zip(t_rows, zs, ts, lower)]
        s *= 2
    hc = c // 2
    off_diag = jnp.logical_not(same(hc))
    ind = [slice(0, hc) if lo else slice(hc, c) for lo in lower]
    dep = [slice(hc, c) if lo else slice(0, hc) for lo in lower]

    def place(x_ind, x_dep, lo):
        return jnp.concatenate([x_ind, x_dep] if lo else [x_dep, x_ind], axis=0)

    x1 = [_dot(t[i].astype(BF16), r.astype(BF16)) for t, i, r in zip(ts, ind, rhss)]
    mixed = [place(x, r[dp], lo).astype(BF16) for x, r, dp, lo in zip(x1, rhss, dep, lower)]
    ys = [_dot(jnp.where(off_diag, m, 0.0)[dp].astype(BF16), mx) for m, dp, mx in zip(ms, dep, mixed)]
    mixed2 = [place(r[i], r[dp] - y, lo).astype(BF16) for r, i, dp, y, lo in zip(rhss, ind, dep, ys, lower)]
    x2 = [_dot(t[dp].astype(BF16), mx) for t, dp, mx in zip(ts, dep, mixed2)]
    return [place(a, b, lo) for a, b, lo in zip(x1, x2, lower)]


def _gdn_scan_kernel(*refs, tb, chunk, with_output):
    (qf, kf, vf, gf, gtf, qb, kb, vb, gb, gtb, s0_ref) = refs[:11]
    if with_output:
        of_ref, ob_ref, s_ref = refs[11:]
        o_refs = (of_ref, ob_ref)
    else:
        sfin_ref, s_ref = refs[11:]
    ins = ((qf, kf, vf, gf, gtf), (qb, kb, vb, gb, gtb))
    j = pl.program_id(1)
    c = chunk
    nch = tb // c
    hd = GDN_HEAD_DIM

    @pl.when(j == 0)
    def _():
        s_ref[...] = s0_ref[0]

    row = lax.broadcasted_iota(jnp.int32, (c, c), 0)
    col = lax.broadcasted_iota(jnp.int32, (c, c), 1)
    incl = (col <= row, col >= row)
    strict = (col < row, col > row)
    nh2 = 2 * GDN_HEADS
    heads = [(d, h) for d in range(2) for h in range(GDN_HEADS)]
    units = [(ci, d, h) for ci in range(nch) for d, h in heads]
    off = lambda ci, d: (ci if d == 0 else nch - 1 - ci) * c
    assert c <= LANES

    gblk = {(ci, d): ins[d][3][0, off(ci, d):off(ci, d) + c, :] for ci in range(nch) for d in range(2)}
    gtblk = {(ci, d): ins[d][4][0, :, off(ci, d):off(ci, d) + c] for ci in range(nch) for d in range(2)}

    def lane_bcast(ci, d, h, o):
        dh = d * GDN_HEADS + h
        return jnp.broadcast_to(gblk[ci, d][:, o + dh:o + dh + 1], (c, LANES))

    g_b = [lane_bcast(ci, d, h, 0) for ci, d, h in units]
    beta = [lane_bcast(ci, d, h, nh2) for ci, d, h in units]
    g_last = [lane_bcast(ci, d, h, 2 * nh2) for ci, d, h in units]
    g_row = [gtblk[ci, d][d * GDN_HEADS + h:d * GDN_HEADS + h + 1, :] for ci, d, h in units]
    ld = lambda idx: [ins[d][idx][0, off(ci, d):off(ci, d) + c, h * hd:(h + 1) * hd] for ci, d, h in units]
    q, k, v = ld(0), ld(1), ld(2)
    decay = [jnp.exp(jnp.where(incl[d], gb_[:, :c] - gr, MASK_VALUE)) for (ci, d, h), gb_, gr in zip(units, g_b, g_row)]
    e_g = [jnp.exp(x) for x in g_b]
    k_beta = [a * b for a, b in zip(k, beta)]
    k16 = [a.astype(BF16) for a in k]
    lhs = [jnp.concatenate([a, b], axis=0) for a, b in zip(k_beta, q)] if with_output else k_beta
    kq = [_dot_nt(a.astype(BF16), b) for a, b in zip(lhs, k16)]
    m = [jnp.where(strict[d], a[:c] * b, 0.0) for (ci, d, h), a, b in zip(units, kq, decay)]
    rhs = [jnp.concatenate([a * b, kb_ * e], axis=1) for a, b, kb_, e in zip(v, beta, k_beta, e_g)]
    uw = _unit_tri_solve(m, rhs, [d == 0 for ci, d, h in units])
    k_tail = [a * jnp.exp(gl - gb_) for a, gl, gb_ in zip(k, g_last, g_b)]
    g_tot = [jnp.exp(gl[0:1, :]) for gl in g_last]

    s = [s_ref[d, h] for d, h in heads]
    nhd = len(heads)
    for ci in range(nch):
        u0 = ci * nhd
        sel = lambda xs_: xs_[u0:u0 + nhd]
        s16 = [a.astype(BF16) for a in s]
        lhs = ([jnp.concatenate([a[:, hd:], b * e], axis=0) for a, b, e in zip(sel(uw), sel(q), sel(e_g))]
               if with_output else [a[:, hd:] for a in sel(uw)])
        xs = [_dot(a.astype(BF16), b) for a, b in zip(lhs, s16)]
        v_new = [a[:, :hd] - b[:c] for a, b in zip(sel(uw), xs)]
        if with_output:
            o_local = [_dot((a[c:] * dcy).astype(BF16), b.astype(BF16))
                       for a, dcy, b in zip(sel(kq), sel(decay), v_new)]
            for (d, h), a, b in zip(heads, xs, o_local):
                o_refs[d][0, off(ci, d):off(ci, d) + c, h * hd:(h + 1) * hd] = (a[c:] + b).astype(o_refs[d].dtype)
        ktv = [_dot_tn(a.astype(BF16), b.astype(BF16)) for a, b in zip(sel(k_tail), v_new)]
        s = [a * gt_ + b for a, gt_, b in zip(s, sel(g_tot), ktv)]
    for (d, h), a in zip(heads, s):
        s_ref[d, h] = a

    if not with_output:
        @pl.when(j == pl.num_programs(1) - 1)
        def _():
            sfin_ref[0] = s_ref[...]


def _gdn_scan_call(q, k, v, gbeta, gt, s0, *, tb, chunk, with_output):
    bsz, n, _ = q.shape
    nblk = n // tb
    fwd = lambda b, j: (b, j, 0)
    bwd = lambda b, j: (b, nblk - 1 - j, 0)
    fwd_t = lambda b, j: (b, 0, j)
    bwd_t = lambda b, j: (b, 0, nblk - 1 - j)

    def specs(idx, idx_t):
        return [pl.BlockSpec((1, tb, GDN_W), idx)] * 3 + [pl.BlockSpec((1, tb, LANES), idx),
                                                          pl.BlockSpec((1, 2 * GDN_HEADS, tb), idx_t)]

    state_spec = pl.BlockSpec((1, 2, GDN_HEADS, GDN_HEAD_DIM, GDN_HEAD_DIM), lambda b, j: (b, 0, 0, 0, 0))
    if with_output:
        out_specs = [pl.BlockSpec((1, tb, GDN_W), fwd), pl.BlockSpec((1, tb, GDN_W), bwd)]
        out_shape = [jax.ShapeDtypeStruct((bsz, n, GDN_W), BF16)] * 2
    else:
        out_specs = state_spec
        out_shape = jax.ShapeDtypeStruct(s0.shape, F32)
    return pl.pallas_call(
        functools.partial(_gdn_scan_kernel, tb=tb, chunk=chunk, with_output=with_output),
        grid=(bsz, nblk),
        in_specs=specs(fwd, fwd_t) + specs(bwd, bwd_t) + [state_spec],
        out_specs=out_specs,
        out_shape=out_shape,
        scratch_shapes=[pltpu.VMEM((2, GDN_HEADS, GDN_HEAD_DIM, GDN_HEAD_DIM), F32)],
        compiler_params=_cparams("parallel", "arbitrary"),
        name="gdn_scan_latent" if with_output else "gdn_scan_context",
    )(q, k, v, gbeta, gt, q, k, v, gbeta, gt, s0)


def _merge_kernel(x_ref, m_ref, g_ref, b_ref, oa_ref, of_ref, ob_ref, z_ref, gab_ref, nw_ref,
                  wpa_ref, wpb_ref, wo_ref, o_ref, *, alpha, n_sub):
    nw = nw_ref[...]
    d = x_ref.shape[-1]
    gate = m_ref[0, 5:6, :]
    ts = x_ref.shape[1] // n_sub
    rows = [slice(i * ts, (i + 1) * ts) for i in range(n_sub)]

    def gdn_out(r):
        o = of_ref[0, r, :].astype(F32) + ob_ref[0, r, :].astype(F32)
        parts = []
        for h in range(GDN_HEADS):
            oh = o[:, h * GDN_HEAD_DIM:(h + 1) * GDN_HEAD_DIM]
            parts.append(oh * lax.rsqrt(jnp.mean(oh * oh, axis=-1, keepdims=True) + NORM_EPS) * nw)
        return (jnp.concatenate(parts, axis=1) * _silu(z_ref[0, r, :].astype(F32))).astype(BF16)

    o_b = [gdn_out(r) for r in rows]
    ya = [_dot(oa_ref[0, r, :], wpa_ref[...]) for r in rows]
    yb = [_dot(ob, wpb_ref[...]) for ob in o_b]
    y = [(_sigmoid(gab_ref[0, r, :d].astype(F32)) * a
          + _sigmoid(gab_ref[0, r, d:].astype(F32)) * b).astype(BF16) for r, a, b in zip(rows, ya, yb)]
    yo = [_dot(yi, wo_ref[...]) for yi in y]
    for r, yoi in zip(rows, yo):
        o_ref[0, r, :] = _ln(alpha * x_ref[0, r, :] + gate * yoi) * g_ref[1:2, :] + b_ref[1:2, :]


def _merge_call(x, mod3, ln_g, ln_b, o_a, o_f, o_b, z, gab, norm_w, w_pa, w_pb, w_o, *, alpha, tm, n_sub):
    bsz, n, d = x.shape
    tok = lambda w: pl.BlockSpec((1, tm, w), lambda b, i: (b, i, 0))
    full = lambda a: pl.BlockSpec(a.shape, lambda b, i: (0,) * a.ndim)
    return pl.pallas_call(
        functools.partial(_merge_kernel, alpha=alpha, n_sub=n_sub),
        grid=(bsz, n // tm),
        in_specs=[tok(d), pl.BlockSpec((1, N_MOD, d), lambda b, i: (b, 0, 0)), full(ln_g), full(ln_b),
                  tok(NA_W), tok(GDN_W), tok(GDN_W), tok(GDN_W), tok(2 * d), full(norm_w),
                  full(w_pa), full(w_pb), full(w_o)],
        out_specs=tok(d),
        out_shape=jax.ShapeDtypeStruct((bsz, n, d), F32),
        compiler_params=_cparams("parallel", "parallel"),
        name="branch_merge",
    )(x, mod3, ln_g, ln_b, o_a, o_f, o_b, z, gab, norm_w, w_pa, w_pb, w_o)


def _rope_tables(n_tok):
    n_freq = GDN_HEAD_DIM // 4
    freqs = ROPE_THETA ** (-np.arange(n_freq, dtype=np.float64) / n_freq)
    t = np.arange(n_tok)
    pos = np.stack([t // GRID_W, t % GRID_W], axis=-1).astype(np.float64)
    ang = pos[:, :, None] * freqs
    cos = np.cos(ang)
    sin = np.sin(ang)
    cos_t = np.concatenate([cos, cos], axis=-1).reshape(n_tok, GDN_HEAD_DIM)
    sin_t = np.concatenate([-sin, sin], axis=-1).reshape(n_tok, GDN_HEAD_DIM)
    return jnp.asarray(cos_t, F32), jnp.asarray(sin_t, F32)


def _pick_tile(n, pref):
    t = min(n, pref)
    assert n % t == 0
    return t


def kernel(x, c, ctx, c_ctx, w_ada, b_ada, ln_g, ln_b, ffn1_w_in, ffn1_w_out, w_in, na_rpb, gdn_conv_w,
           gdn_a_log, gdn_dt_bias, gdn_norm_w, w_pa, w_pb, w_o, ffn2_w_in, ffn2_w_out):
    depth = w_ada.shape[0]
    assert depth == 1, "context-update path of non-final layers is not implemented"
    bsz, n, d = x.shape
    lc = ctx.shape[1]
    alpha = (2 * depth) ** 0.25
    layer = 0

    n_rows = -(-(bsz + 1) // SUBLANES) * SUBLANES
    cc = jnp.zeros((n_rows, d), F32).at[:bsz].set(c).at[bsz].set(c_ctx)
    mod3 = _mod_call(cc, w_ada[layer], b_ada[layer], tn=MOD_COLS).reshape(n_rows, N_MOD, d)

    g_ln, b_ln = ln_g[layer], ln_b[layer]
    w1u, w1d = ffn1_w_in[layer].astype(BF16), ffn1_w_out[layer].astype(BF16)
    dense = dict(tm=_pick_tile(n, DENSE_TOKENS), n_sub=DENSE_SUBTILES)
    dense_ctx = dict(tm=_pick_tile(bsz * lc, DENSE_TOKENS), n_sub=DENSE_SUBTILES)
    x1 = _ffn_call(x, mod3, g_ln, b_ln, w1u, w1d, sub=0, alpha=alpha, mod_row=None, **dense)
    ctx_flat = ctx.reshape(1, bsz * lc, d)
    ctx1 = _ffn_call(ctx_flat, mod3, g_ln, b_ln, w1u, w1d, sub=0, alpha=alpha, mod_row=bsz, **dense_ctx)

    w_a, w_g, w_z, w_gab, w_ab = _regroup_w_in(w_in[layer])
    qkv_a, qkv_g, z, gab, ab = _proj_call(x1, mod3, (w_a, w_g, w_z, w_gab, w_ab), (BF16, F32, BF16, BF16, F32),
                                          mod_row=None, **dense)
    kv_a_c, qkv_g_c, ab_c = (t.reshape(bsz, lc, t.shape[-1]) for t in _proj_call(
        ctx1, mod3, (w_a[:, NA_W:], w_g, w_ab), (BF16, F32, F32), mod_row=bsz, **dense_ctx))

    o_a = _attn_call(qkv_a, kv_a_c, _na_bias_table(na_rpb[layer]), tr=ATTN_ROWS, rpi=ATTN_ROWS_PER_ITER)

    nea = jnp.zeros((1, LANES), F32).at[0, :2 * GDN_HEADS].set(-jnp.exp(gdn_a_log[layer].astype(F32)).reshape(-1))
    dtb = jnp.zeros((1, LANES), F32).at[0, :2 * GDN_HEADS].set(gdn_dt_bias[layer].astype(F32).reshape(-1))
    cw = gdn_conv_w[layer]
    cos_t, sin_t = _rope_tables(n)
    tp = _pick_tile(n, GDN_TOKENS)
    tpc = _pick_tile(lc, GDN_TOKENS)
    q_l, k_l, v_l, gb_l, gt_l = _gdn_prep_call(qkv_g, ab, cw, nea, dtb, cos_t, sin_t, use_rope=True, t=tp,
                                                chunk=GDN_CHUNK)
    q_c, k_c, v_c, gb_c, gt_c = _gdn_prep_call(qkv_g_c, ab_c, cw, nea, dtb, cos_t[:lc], sin_t[:lc], use_rope=False,
                                                t=tpc, chunk=GDN_CHUNK)
    s0 = jnp.zeros((bsz, 2, GDN_HEADS, GDN_HEAD_DIM, GDN_HEAD_DIM), F32)
    s_ctx = _gdn_scan_call(q_c, k_c, v_c, gb_c, gt_c, s0, tb=tpc, chunk=GDN_CHUNK, with_output=False)
    o_f, o_b = _gdn_scan_call(q_l, k_l, v_l, gb_l, gt_l, s_ctx, tb=_pick_tile(n, GDN_SCAN_TOKENS), chunk=GDN_CHUNK,
                              with_output=True)

    x2 = _merge_call(x1, mod3, g_ln, b_ln, o_a, o_f, o_b, z, gab, gdn_norm_w[layer].reshape(1, GDN_HEAD_DIM),
                     w_pa[layer].astype(BF16), w_pb[layer].astype(BF16), w_o[layer].astype(BF16),
                     alpha=alpha, **dense)

    w2u, w2d = ffn2_w_in[layer].astype(BF16), ffn2_w_out[layer].astype(BF16)
    return _ffn_call(x2, mod3, g_ln, b_ln, w2u, w2d, sub=2, alpha=alpha, mod_row=None, **dense)
```

```python
import functools
import math

import numpy as np
import jax
import jax.numpy as jnp
from jax import lax
from jax.experimental import pallas as pl
from jax.experimental.pallas import tpu as pltpu

F32 = jnp.float32
BF16 = jnp.bfloat16

GRID_W = 64
NA_HEADS = 8
NA_HEAD_DIM = 64
NA_WIN_ROWS = 8
NA_WIN_COLS = 16
GDN_HEADS = 4
GDN_HEAD_DIM = 128
GDN_CONV = 5
N_MOD = 9
ROPE_THETA = 10000.0
LN_EPS = 1e-6
NORM_EPS = 1e-6
NA_W = NA_HEADS * NA_HEAD_DIM
GDN_W = GDN_HEADS * GDN_HEAD_DIM

LANES = 128
SUBLANES = 8
VMEM_LIMIT_BYTES = 56 * 1024 * 1024
MASK_VALUE = -1e30
GDN_CHUNK = 128

DENSE_TOKENS = 512
DENSE_SUBTILES = 2
FFN_TOKENS = 1024
FFN_SUBTILES = 4
GDN_TOKENS = 2 * GDN_CHUNK
GDN_SCAN_TOKENS = 4 * GDN_CHUNK
ATTN_ROWS = 8
ATTN_ROWS_PER_ITER = 4
MOD_COLS = 9 * LANES


def _cparams(*sem):
    return pltpu.CompilerParams(dimension_semantics=sem, vmem_limit_bytes=VMEM_LIMIT_BYTES)


def _ln(x):
    mu = jnp.mean(x, axis=-1, keepdims=True)
    xc = x - mu
    var = jnp.mean(xc * xc, axis=-1, keepdims=True)
    return xc * lax.rsqrt(var + LN_EPS)


def _sigmoid(x):
    return 0.5 * jnp.tanh(0.5 * x) + 0.5


def _silu(x):
    h = 0.5 * x
    return h + h * jnp.tanh(h)


def _dot(a, b):
    return jnp.dot(a, b, preferred_element_type=F32)


def _dot_nt(a, b):
    return lax.dot_general(a, b, (((1,), (1,)), ((), ())), preferred_element_type=F32)


def _dot_tn(a, b):
    return lax.dot_general(a, b, (((0,), (0,)), ((), ())), preferred_element_type=F32)


def _dot_exact_lhs(a_bf16, x):
    hi = x.astype(BF16)
    r1 = x - hi.astype(F32)
    mid = r1.astype(BF16)
    lo = (r1 - mid.astype(F32)).astype(BF16)
    return _dot(a_bf16, hi) + _dot(a_bf16, mid) + _dot(a_bf16, lo)


def _mod_kernel(c_ref, w_ref, b_ref, o_ref):
    s = _silu(c_ref[...]).astype(BF16)
    o_ref[...] = _dot(s, w_ref[...].astype(BF16)) + b_ref[...]


def _mod_call(cc, w_ada, b_ada, *, tn):
    r, d = cc.shape
    n = w_ada.shape[1]
    return pl.pallas_call(
        _mod_kernel,
        grid=(n // tn,),
        in_specs=[
            pl.BlockSpec((r, d), lambda j: (0, 0)),
            pl.BlockSpec((d, tn), lambda j: (0, j)),
            pl.BlockSpec((1, tn), lambda j: (0, j)),
        ],
        out_specs=pl.BlockSpec((r, tn), lambda j: (0, j)),
        out_shape=jax.ShapeDtypeStruct((r, n), F32),
        compiler_params=_cparams("arbitrary"),
        name="adaln_modulation",
    )(cc, w_ada, b_ada.reshape(1, n))


def _ffn_kernel(x_ref, m_ref, g_ref, b_ref, wa_ref, wb_ref, wd_ref, o_ref, *, sub, alpha, n_sub):
    shift = m_ref[0, 3 * sub:3 * sub + 1, :]
    scale = m_ref[0, 3 * sub + 1:3 * sub + 2, :]
    gate = 0.5 * m_ref[0, 3 * sub + 2:3 * sub + 3, :]
    ts = x_ref.shape[1] // n_sub
    rows = [slice(i * ts, (i + 1) * ts) for i in range(n_sub)]

    def modulated(r):
        return (_ln(x_ref[0, r, :]) * (1.0 + scale) + shift).astype(BF16)

    def finish(r, y):
        o_ref[0, r, :] = _ln(alpha * x_ref[0, r, :] + gate * y) * g_ref[sub:sub + 1, :] + b_ref[sub:sub + 1, :]

    u = modulated(rows[0])
    pending = None
    for i, r in enumerate(rows):
        u_next = modulated(rows[i + 1]) if i + 1 < n_sub else None
        h = (_silu(_dot(u, wa_ref[...])) * _dot(u, wb_ref[...])).astype(BF16)
        y = _dot(h, wd_ref[...])
        if pending is not None:
            finish(*pending)
        pending = (r, y)
        u = u_next
    finish(*pending)


def _ffn_call(x, mod3, ln_g, ln_b, w_up, w_down, *, sub, alpha, mod_row, tm, n_sub):
    bsz, n, d = x.shape
    f = w_down.shape[0]
    if mod_row is None:
        mod_idx = lambda b, i: (b, 0, 0)
    else:
        mod_idx = lambda b, i: (mod_row, 0, 0)
    once = pl.Buffered(1)
    return pl.pallas_call(
        functools.partial(_ffn_kernel, sub=sub, alpha=alpha, n_sub=n_sub),
        grid=(bsz, n // tm),
        in_specs=[
            pl.BlockSpec((1, tm, d), lambda b, i: (b, i, 0)),
            pl.BlockSpec((1, N_MOD, d), mod_idx),
            pl.BlockSpec(ln_g.shape, lambda b, i: (0, 0)),
            pl.BlockSpec(ln_b.shape, lambda b, i: (0, 0)),
            pl.BlockSpec((d, f), lambda b, i: (0, 0), pipeline_mode=once),
            pl.BlockSpec((d, f), lambda b, i: (0, 1), pipeline_mode=once),
            pl.BlockSpec((f, d), lambda b, i: (0, 0), pipeline_mode=once),
        ],
        out_specs=pl.BlockSpec((1, tm, d), lambda b, i: (b, i, 0)),
        out_shape=jax.ShapeDtypeStruct((bsz, n, d), F32),
        compiler_params=_cparams("parallel", "parallel"),
        name=f"ffn_sublayer_{sub}",
    )(x, mod3, ln_g, ln_b, w_up, w_up, w_down)


_O_NA = 3 * NA_W
_O_GDN = _O_NA + 3 * GDN_W
_O_Z = _O_GDN + GDN_W
_O_AB = _O_Z + 4 * GDN_HEADS


def _regroup_kernel(w_ref, wa_ref, wg_ref, wz_ref, wgab_ref, wab_ref):
    w = w_ref[...]
    wa_ref[...] = w[:, :_O_NA].astype(BF16)
    wg_ref[...] = w[:, _O_NA:_O_GDN].astype(BF16)
    wz_ref[...] = w[:, _O_GDN:_O_Z].astype(BF16)
    wgab_ref[...] = w[:, _O_AB:].astype(BF16)
    lane = lax.broadcasted_iota(jnp.int32, (w.shape[0], LANES), 1)
    wab_ref[...] = jnp.where(lane < _O_AB - _O_Z, w[:, _O_Z:_O_Z + LANES], 0.0).astype(BF16)


def _regroup_w_in(w_in, *, tr=128):
    d, n_in = w_in.shape
    widths = (_O_NA, _O_GDN - _O_NA, _O_Z - _O_GDN, n_in - _O_AB, LANES)
    return pl.pallas_call(
        _regroup_kernel,
        grid=(d // tr,),
        in_specs=[pl.BlockSpec((tr, n_in), lambda i: (i, 0))],
        out_specs=[pl.BlockSpec((tr, w), lambda i: (i, 0)) for w in widths],
        out_shape=[jax.ShapeDtypeStruct((d, w), BF16) for w in widths],
        compiler_params=_cparams("parallel"),
        name="regroup_w_in",
    )(w_in)

def _proj_kernel(x_ref, m_ref, *refs, n_sub):
    n_out = len(refs) // 2
    shift = m_ref[0, 3:4, :]
    scale = m_ref[0, 4:5, :]
    ts = x_ref.shape[1] // n_sub
    rows = [slice(i * ts, (i + 1) * ts) for i in range(n_sub)]
    u = [(_ln(x_ref[0, r, :]) * (1.0 + scale) + shift).astype(BF16) for r in rows]
    for w_ref, o_ref in zip(refs[:n_out], refs[n_out:]):
        for r, ui in zip(rows, u):
            o_ref[0, r, :] = _dot(ui, w_ref[...]).astype(o_ref.dtype)


def _proj_call(x, mod3, ws, dts, *, mod_row, tm, n_sub):
    bsz, n, d = x.shape
    if mod_row is None:
        mod_idx = lambda b, i: (b, 0, 0)
    else:
        mod_idx = lambda b, i: (mod_row, 0, 0)
    return pl.pallas_call(
        functools.partial(_proj_kernel, n_sub=n_sub),
        grid=(bsz, n // tm),
        in_specs=[pl.BlockSpec((1, tm, d), lambda b, i: (b, i, 0)),
                  pl.BlockSpec((1, N_MOD, d), mod_idx)]
                 + [pl.BlockSpec(w.shape, lambda b, i: (0, 0), pipeline_mode=pl.Buffered(1)) for w in ws],
        out_specs=[pl.BlockSpec((1, tm, w.shape[1]), lambda b, i: (b, i, 0)) for w in ws],
        out_shape=[jax.ShapeDtypeStruct((bsz, n, w.shape[1]), dt) for w, dt in zip(ws, dts)],
        compiler_params=_cparams("parallel", "parallel"),
        name="mixer_in_proj",
    )(x, mod3, *ws)


def _attn_kernel(q_ref, k_ref, v_ref, kc_ref, vc_ref, bias_ref, o_ref, *, tr, rows, rpi):
    i = pl.program_id(1)
    n_pairs = NA_HEADS // 2
    pw = 2 * NA_HEAD_DIM
    win = NA_WIN_ROWS * GRID_W
    lane = lax.broadcasted_iota(jnp.int32, (GRID_W, pw), 1)
    first_head = lane < NA_HEAD_DIM
    scale = NA_HEAD_DIM ** -0.5

    def row_body(it, carry):
        chains = [(jr, p) for jr in range(rpi) for p in range(n_pairs)]
        cs = [slice(p * pw, (p + 1) * pw) for jr, p in chains]
        q0, tok0, b0 = [], [], []
        for jr in range(rpi):
            rl = it * rpi + jr
            r = i * tr + rl
            r0 = jnp.clip(r - NA_WIN_ROWS // 2, 0, rows - NA_WIN_ROWS)
            tok0 += [pl.multiple_of(r0 * GRID_W, GRID_W)] * n_pairs
            q0 += [pl.multiple_of(rl * GRID_W, GRID_W)] * n_pairs
            b0 += [r0 - r + NA_WIN_ROWS - 1] * n_pairs
        ids = range(len(chains))
        qs = []
        for c in ids:
            q2 = q_ref[0, pl.ds(q0[c], GRID_W), cs[c]] * scale
            zero = jnp.zeros_like(q2)
            qs.append(jnp.concatenate([jnp.where(first_head, q2, zero), jnp.where(first_head, zero, q2)], axis=0))
        s_loc = [_dot_nt(qs[c], k_ref[0, pl.ds(tok0[c], win), cs[c]]) for c in ids]
        nq = 2 * GRID_W
        pcs = cs[:n_pairs]

        def per_chain(stacked):
            return [stacked[p][jr * nq:(jr + 1) * nq] for jr, p in chains]

        s_ctx = per_chain([_dot_nt(jnp.concatenate([qs[jr * n_pairs + p] for jr in range(rpi)], axis=0),
                                   kc_ref[0, :, pcs[p]]) for p in range(n_pairs)])
        s_loc = [s_loc[c] + jnp.concatenate([bias_ref[b0[c] + 2 * m, chains[c][1]] for m in range(NA_WIN_ROWS // 2)],
                                            axis=1) for c in ids]
        mx = [jnp.maximum(jnp.max(s_loc[c], axis=-1, keepdims=True), jnp.max(s_ctx[c], axis=-1, keepdims=True))
              for c in ids]
        p_loc = [jnp.exp(s_loc[c] - mx[c]) for c in ids]
        p_ctx = [jnp.exp(s_ctx[c] - mx[c]) for c in ids]
        den = [jnp.sum(p_loc[c], axis=-1, keepdims=True) + jnp.sum(p_ctx[c], axis=-1, keepdims=True) for c in ids]
        o_ctx = per_chain([_dot(jnp.concatenate([p_ctx[jr * n_pairs + p] for jr in range(rpi)], axis=0).astype(BF16),
                                vc_ref[0, :, pcs[p]]) for p in range(n_pairs)])
        o = [_dot(p_loc[c].astype(BF16), v_ref[0, pl.ds(tok0[c], win), cs[c]]) + o_ctx[c] for c in ids]
        for c in ids:
            on = o[c] / den[c]
            o_ref[0, pl.ds(q0[c], GRID_W), cs[c]] = jnp.where(first_head, on[:GRID_W], on[GRID_W:]).astype(o_ref.dtype)
        return carry

    lax.fori_loop(0, tr // rpi, row_body, 0)


def _attn_call(qkv, kv_ctx, bias_tab, *, tr, rpi):
    bsz, n, _ = qkv.shape
    rows = n // GRID_W
    lc = kv_ctx.shape[1]
    assert rows >= NA_WIN_ROWS and rows % tr == 0 and tr % rpi == 0
    return pl.pallas_call(
        functools.partial(_attn_kernel, tr=tr, rows=rows, rpi=rpi),
        grid=(bsz, rows // tr),
        in_specs=[
            pl.BlockSpec((1, tr * GRID_W, NA_W), lambda b, i: (b, i, 0)),
            pl.BlockSpec((1, n, NA_W), lambda b, i: (b, 0, 1)),
            pl.BlockSpec((1, n, NA_W), lambda b, i: (b, 0, 2)),
            pl.BlockSpec((1, lc, NA_W), lambda b, i: (b, 0, 0)),
            pl.BlockSpec((1, lc, NA_W), lambda b, i: (b, 0, 1)),
            pl.BlockSpec(bias_tab.shape, lambda b, i: (0, 0, 0, 0)),
        ],
        out_specs=pl.BlockSpec((1, tr * GRID_W, NA_W), lambda b, i: (b, i, 0)),
        out_shape=jax.ShapeDtypeStruct((bsz, n, NA_W), BF16),
        compiler_params=_cparams("parallel", "arbitrary"),
        name="neighbourhood_attention",
    )(qkv, qkv, qkv, kv_ctx, kv_ctx, bias_tab)


def _na_bias_table(rpb):
    n_heads, n_rel_rows, n_rel = rpb.shape
    rows = jnp.zeros((n_rel_rows, n_heads, LANES), F32).at[:, :, :n_rel].set(rpb.astype(F32).transpose(1, 0, 2))
    return pl.pallas_call(
        _na_bias_kernel,
        grid=(n_rel_rows - 1,),
        in_specs=[pl.BlockSpec(rows.shape, lambda r: (0, 0, 0))],
        out_specs=pl.BlockSpec((1, n_heads // 2, 2 * GRID_W, 2 * GRID_W), lambda r: (r, 0, 0, 0)),
        out_shape=jax.ShapeDtypeStruct((n_rel_rows - 1, n_heads // 2, 2 * GRID_W, 2 * GRID_W), F32),
        compiler_params=_cparams("parallel"),
        name="na_bias_table",
    )(rows)


def _na_bias_kernel(rows_ref, o_ref):
    r = pl.program_id(0)
    w = lax.broadcasted_iota(jnp.int32, (GRID_W, LANES), 0)
    lane = lax.broadcasted_iota(jnp.int32, (GRID_W, LANES), 1)
    c = lane % GRID_W
    c0 = jnp.clip(w - NA_WIN_COLS // 2, 0, GRID_W - NA_WIN_COLS)
    valid = (c >= c0) & (c < c0 + NA_WIN_COLS)
    for p in range(NA_HEADS // 2):
        for h2 in range(2):
            halves = []
            for i2 in range(2):
                v = jnp.broadcast_to(rows_ref[r + i2, 2 * p + h2:2 * p + h2 + 1, :], (GRID_W, LANES))
                shift = (LANES - (NA_WIN_COLS - 1) + i2 * GRID_W) % LANES
                halves.append(pltpu.roll(v, shift, 1, stride=1, stride_axis=0))
            tile = jnp.where(lane < GRID_W, halves[0], halves[1])
            o_ref[0, p, h2 * GRID_W:(h2 + 1) * GRID_W, :] = jnp.where(valid, tile, MASK_VALUE)


def _gdn_prep_kernel(x_ref, xp_ref, xn_ref, ab_ref, cw_ref, nea_ref, dtb_ref, cos_ref, sin_ref, mats_ref,
                     q_ref, k_ref, v_ref, gb_ref, gt_ref, *, t, use_rope):
    i = pl.program_id(1)
    halo = SUBLANES
    pad = GDN_CONV // 2
    hd = GDN_HEAD_DIM
    taps = [j for j in range(GDN_CONV) if j != pad]
    n_tap = len(taps)
    shift_stack = mats_ref[0:n_tap].reshape(n_tap * t, t)
    first = i > 0
    last = i < pl.num_programs(1) - 1
    zeros = jnp.zeros((halo, hd), F32)
    lane = lax.broadcasted_iota(jnp.int32, (t, hd), 1)
    half0 = (lane % (hd // 2)) < (hd // 4)
    quarter = hd // 4
    shifted2 = None
    for cb in range(3 * GDN_HEADS):
        cs = slice(cb * hd, (cb + 1) * hd)
        xb = x_ref[0, :, cs]
        cw = cw_ref[:, cs]
        if cb % 2 == 0:
            shifted2 = _dot(shift_stack, x_ref[0, :, cb * hd:(cb + 2) * hd].astype(BF16))
        shifted = shifted2[:, (cb % 2) * hd:(cb % 2 + 1) * hd]
        acc = cw[pad:pad + 1, :] * xb
        for n, j in enumerate(taps):
            acc = acc + cw[j:j + 1, :] * shifted[n * t:(n + 1) * t]
        top = jnp.concatenate([jnp.where(first, xp_ref[0, :, cs], 0.0), zeros], axis=0)
        bot = jnp.concatenate([zeros, jnp.where(last, xn_ref[0, :, cs], 0.0)], axis=0)
        e_top = sum(cw[j:j + 1, :] * top[halo - pad + j:2 * halo - pad + j, :] for j in range(pad))
        e_bot = sum(cw[j:j + 1, :] * bot[j - pad:j - pad + halo, :] for j in range(pad + 1, GDN_CONV))
        acc = jnp.concatenate([acc[:halo] + e_top, acc[halo:t - halo], acc[t - halo:] + e_bot], axis=0)
        y = _silu(acc)
        which, h = divmod(cb, GDN_HEADS)
        if which < 2:
            y = y * lax.rsqrt(jnp.sum(y * y, axis=-1, keepdims=True) + NORM_EPS)
            if use_rope:
                partner = jnp.where(half0, pltpu.roll(y, hd - quarter, 1), pltpu.roll(y, quarter, 1))
                y = y * cos_ref[...] + partner * sin_ref[...]
        if which == 0:
            y = y * (hd ** -0.5)
        (q_ref, k_ref, v_ref)[which][0, :, h * hd:(h + 1) * hd] = y

    ab = ab_ref[0]
    nh2 = 2 * GDN_HEADS
    za = ab + dtb_ref[...]
    softplus = jnp.maximum(za, 0.0) + jnp.log(1.0 + jnp.exp(-jnp.abs(za)))
    g = nea_ref[...] * softplus
    beta = _sigmoid(ab)
    n_tap = GDN_CONV - 1
    tri_f, tri_b, same = mats_ref[n_tap], mats_ref[n_tap + 1], mats_ref[n_tap + 2]
    lane_g = lax.broadcasted_iota(jnp.int32, (t, LANES), 1)
    gcum = jnp.where(lane_g < GDN_HEADS, _dot_exact_lhs(tri_f, g), _dot_exact_lhs(tri_b, g))
    gtot = pltpu.roll(_dot_exact_lhs(same, g), 2 * nh2, 1)
    gb_ref[0] = jnp.where(lane_g < nh2, gcum,
                          jnp.where(lane_g < 2 * nh2, beta, jnp.where(lane_g < 3 * nh2, gtot, 0.0)))
    gt_ref[0] = jnp.transpose(jnp.where(lane_g < nh2, gcum, 0.0))[0:nh2, :]


def _prep_matrices(t, chunk):
    r = np.arange(t)[:, None]
    c = np.arange(t)[None, :]
    pad = GDN_CONV // 2
    shifts = [c == r + (j - pad) for j in range(GDN_CONV) if j != pad]
    same = (r // chunk) == (c // chunk)
    mats = shifts + [same & (c <= r), same & (c >= r), same]
    return jnp.asarray(np.stack(mats).astype(np.float32), dtype=BF16)


def _gdn_prep_call(qkv_g, ab, conv_w, nea_row, dtb_row, cos_t, sin_t, *, use_rope, t, chunk):
    bsz, n, c3 = qkv_g.shape
    nb8 = n // SUBLANES
    tb8 = t // SUBLANES
    f3 = jax.ShapeDtypeStruct((bsz, n, GDN_W), F32)
    mats = _prep_matrices(t, chunk)
    return pl.pallas_call(
        functools.partial(_gdn_prep_kernel, t=t, use_rope=use_rope),
        grid=(bsz, n // t),
        in_specs=[
            pl.BlockSpec((1, t, c3), lambda b, i: (b, i, 0)),
            pl.BlockSpec((1, SUBLANES, c3), lambda b, i: (b, jnp.maximum(i * tb8 - 1, 0), 0)),
            pl.BlockSpec((1, SUBLANES, c3), lambda b, i: (b, jnp.minimum((i + 1) * tb8, nb8 - 1), 0)),
            pl.BlockSpec((1, t, LANES), lambda b, i: (b, i, 0)),
            pl.BlockSpec(conv_w.shape, lambda b, i: (0, 0)),
            pl.BlockSpec((1, LANES), lambda b, i: (0, 0)),
            pl.BlockSpec((1, LANES), lambda b, i: (0, 0)),
            pl.BlockSpec((t, GDN_HEAD_DIM), lambda b, i: (i, 0)),
            pl.BlockSpec((t, GDN_HEAD_DIM), lambda b, i: (i, 0)),
            pl.BlockSpec(mats.shape, lambda b, i: (0, 0, 0)),
        ],
        out_specs=[
            pl.BlockSpec((1, t, GDN_W), lambda b, i: (b, i, 0)),
            pl.BlockSpec((1, t, GDN_W), lambda b, i: (b, i, 0)),
            pl.BlockSpec((1, t, GDN_W), lambda b, i: (b, i, 0)),
            pl.BlockSpec((1, t, LANES), lambda b, i: (b, i, 0)),
            pl.BlockSpec((1, 2 * GDN_HEADS, t), lambda b, i: (b, 0, i)),
        ],
        out_shape=[f3, f3, f3,
                   jax.ShapeDtypeStruct((bsz, n, LANES), F32),
                   jax.ShapeDtypeStruct((bsz, 2 * GDN_HEADS, n), F32)],
        compiler_params=_cparams("parallel", "parallel"),
        name="gdn_prepare",
    )(qkv_g, qkv_g, qkv_g, ab, conv_w, nea_row, dtb_row, cos_t, sin_t, mats)


TRI_BASE = 8


def _slabs(x, s, second):
    start = s if second else 0
    return jnp.concatenate([x[i:i + s] for i in range(start, x.shape[0], 2 * s)], axis=0)


def _interleave(upd, other, s, second):
    parts = []
    for p in range(other.shape[0] // (2 * s)):
        u = upd[s * p:s * (p + 1)]
        if second:
            parts += [other[2 * s * p:2 * s * p + s], u]
        else:
            parts += [u, other[2 * s * p + s:2 * s * (p + 1)]]
    return jnp.concatenate(parts, axis=0)


def _unit_tri_solve(ms, rhss, lower):
    c = ms[0].shape[0]
    n = len(ms)
    row = lax.broadcasted_iota(jnp.int32, (c, c), 0)
    col = lax.broadcasted_iota(jnp.int32, (c, c), 1)
    eye = (row == col).astype(F32)
    same = lambda s: (row // s) == (col // s)
    base = same(TRI_BASE)
    ds = [jnp.where(base, m, 0.0) for m in ms]
    ps = [eye - d for d in ds]
    dbs = [d.astype(BF16) for d in ds]
    qs = [_dot(db, db) for db in dbs]
    n_iter = int(math.log2(TRI_BASE)) - 1
    for it in range(n_iter):
        qbs = [q.astype(BF16) for q in qs]
        if it < n_iter - 1:
            yqs = [_dot(qb, jnp.concatenate([p, q], axis=1).astype(BF16)) for qb, p, q in zip(qbs, ps, qs)]
            ps = [p + yq[:, :c] for p, yq in zip(ps, yqs)]
            qs = [yq[:, c:] for yq in yqs]
        else:
            ys = [_dot(qb, p.astype(BF16)) for qb, p in zip(qbs, ps)]
            ps = [p + y for p, y in zip(ps, ys)]
    ts = ps
    zeros = jnp.zeros((c, c), F32)
    s = TRI_BASE
    while 2 * s < c:
        off_diag = same(2 * s) & jnp.logical_not(same(s))
        l_rows = [_slabs(jnp.where(off_diag, m, 0.0), s, lo).astype(BF16) for m, lo in zip(ms, lower)]
        t_rows = [_slabs(t, s, lo) for t, lo in zip(ts, lower)]
        ys = [_dot(l, t.astype(BF16)) for l, t in zip(l_rows, ts)]
        y_full = [_interleave(y, zeros, s, lo).astype(BF16) for y, lo in zip(ys, lower)]
        zs = [_dot(tr.astype(BF16), yf) for tr, yf in zip(t_rows, y_full)]
        ts = [_interleave(tr - z, t, s, lo) for tr, z, t, lo in zip(t_rows, zs, ts, lower)]
        s *= 2
    hc = c // 2
    off_diag = jnp.logical_not(same(hc))
    ind = [slice(0, hc) if lo else slice(hc, c) for lo in lower]
    dep = [slice(hc, c) if lo else slice(0, hc) for lo in lower]

    def place(x_ind, x_dep, lo):
        return jnp.concatenate([x_ind, x_dep] if lo else [x_dep, x_ind], axis=0)

    x1 = [_dot(t[i].astype(BF16), r.astype(BF16)) for t, i, r in zip(ts, ind, rhss)]
    mixed = [place(x, r[dp], lo).astype(BF16) for x, r, dp, lo in zip(x1, rhss, dep, lower)]
    ys = [_dot(jnp.where(off_diag, m, 0.0)[dp].astype(BF16), mx) for m, dp, mx in zip(ms, dep, mixed)]
    mixed2 = [place(r[i], r[dp] - y, lo).astype(BF16) for r, i, dp, y, lo in zip(rhss, ind, dep, ys, lower)]
    x2 = [_dot(t[dp].astype(BF16), mx) for t, dp, mx in zip(ts, dep, mixed2)]
    return [place(a, b, lo) for a, b, lo in zip(x1, x2, lower)]


def _gdn_scan_kernel(*refs, tb, chunk, with_output):
    (qf, kf, vf, gf, gtf, qb, kb, vb, gb, gtb, s0_ref) = refs[:11]
    if with_output:
        of_ref, ob_ref, s_ref = refs[11:]
        o_refs = (of_ref, ob_ref)
    else:
        sfin_ref, s_ref = refs[11:]
    ins = ((qf, kf, vf, gf, gtf), (qb, kb, vb, gb, gtb))
    j = pl.program_id(1)
    c = chunk
    nch = tb // c
    hd = GDN_HEAD_DIM

    @pl.when(j == 0)
    def _():
        s_ref[...] = s0_ref[0]

    row = lax.broadcasted_iota(jnp.int32, (c, c), 0)
    col = lax.broadcasted_iota(jnp.int32, (c, c), 1)
    incl = (col <= row, col >= row)
    strict = (col < row, col > row)
    nh2 = 2 * GDN_HEADS
    heads = [(d, h) for d in range(2) for h in range(GDN_HEADS)]
    units = [(ci, d, h) for ci in range(nch) for d, h in heads]
    off = lambda ci, d: (ci if d == 0 else nch - 1 - ci) * c
    assert c <= LANES

    gblk = {(ci, d): ins[d][3][0, off(ci, d):off(ci, d) + c, :] for ci in range(nch) for d in range(2)}
    gtblk = {(ci, d): ins[d][4][0, :, off(ci, d):off(ci, d) + c] for ci in range(nch) for d in range(2)}

    def lane_bcast(ci, d, h, o):
        dh = d * GDN_HEADS + h
        return jnp.broadcast_to(gblk[ci, d][:, o + dh:o + dh + 1], (c, LANES))

    g_b = [lane_bcast(ci, d, h, 0) for ci, d, h in units]
    beta = [lane_bcast(ci, d, h, nh2) for ci, d, h in units]
    g_last = [lane_bcast(ci, d, h, 2 * nh2) for ci, d, h in units]
    g_row = [gtblk[ci, d][d * GDN_HEADS + h:d * GDN_HEADS + h + 1, :] for ci, d, h in units]
    ld = lambda idx: [ins[d][idx][0, off(ci, d):off(ci, d) + c, h * hd:(h + 1) * hd] for ci, d, h in units]
    q, k, v = ld(0), ld(1), ld(2)
    decay = [jnp.exp(jnp.where(incl[d], gb_[:, :c] - gr, MASK_VALUE)) for (ci, d, h), gb_, gr in zip(units, g_b, g_row)]
    e_g = [jnp.exp(x) for x in g_b]
    k_beta = [a * b for a, b in zip(k, beta)]
    k16 = [a.astype(BF16) for a in k]
    lhs = [jnp.concatenate([a, b], axis=0) for a, b in zip(k_beta, q)] if with_output else k_beta
    kq = [_dot_nt(a.astype(BF16), b) for a, b in zip(lhs, k16)]
    m = [jnp.where(strict[d], a[:c] * b, 0.0) for (ci, d, h), a, b in zip(units, kq, decay)]
    rhs = [jnp.concatenate([a * b, kb_ * e], axis=1) for a, b, kb_, e in zip(v, beta, k_beta, e_g)]
    uw = _unit_tri_solve(m, rhs, [d == 0 for ci, d, h in units])
    k_tail = [a * jnp.exp(gl - gb_) for a, gl, gb_ in zip(k, g_last, g_b)]
    g_tot = [jnp.exp(gl[0:1, :]) for gl in g_last]

    s = [s_ref[d, h] for d, h in heads]
    nhd = len(heads)
    for ci in range(nch):
        u0 = ci * nhd
        sel = lambda xs_: xs_[u0:u0 + nhd]
        s16 = [a.astype(BF16) for a in s]
        lhs = ([jnp.concatenate([a[:, hd:], b * e], axis=0) for a, b, e in zip(sel(uw), sel(q), sel(e_g))]
               if with_output else [a[:, hd:] for a in sel(uw)])
        xs = [_dot(a.astype(BF16), b) for a, b in zip(lhs, s16)]
        v_new = [a[:, :hd] - b[:c] for a, b in zip(sel(uw), xs)]
        if with_output:
            o_local = [_dot((a[c:] * dcy).astype(BF16), b.astype(BF16))
                       for a, dcy, b in zip(sel(kq), sel(decay), v_new)]
            for (d, h), a, b in zip(heads, xs, o_local):
                o_refs[d][0, off(ci, d):off(ci, d) + c, h * hd:(h + 1) * hd] = (a[c:] + b).astype(o_refs[d].dtype)
        ktv = [_dot_tn(a.astype(BF16), b.astype(BF16)) for a, b in zip(sel(k_tail), v_new)]
        s = [a * gt_ + b for a, gt_, b in zip(s, sel(g_tot), ktv)]
    for (d, h), a in zip(heads, s):
        s_ref[d, h] = a

    if not with_output:
        @pl.when(j == pl.num_programs(1) - 1)
        def _():
            sfin_ref[0] = s_ref[...]


def _gdn_scan_call(q, k, v, gbeta, gt, s0, *, tb, chunk, with_output):
    bsz, n, _ = q.shape
    nblk = n // tb
    fwd = lambda b, j: (b, j, 0)
    bwd = lambda b, j: (b, nblk - 1 - j, 0)
    fwd_t = lambda b, j: (b, 0, j)
    bwd_t = lambda b, j: (b, 0, nblk - 1 - j)

    def specs(idx, idx_t):
        return [pl.BlockSpec((1, tb, GDN_W), idx)] * 3 + [pl.BlockSpec((1, tb, LANES), idx),
                                                          pl.BlockSpec((1, 2 * GDN_HEADS, tb), idx_t)]

    state_spec = pl.BlockSpec((1, 2, GDN_HEADS, GDN_HEAD_DIM, GDN_HEAD_DIM), lambda b, j: (b, 0, 0, 0, 0))
    if with_output:
        out_specs = [pl.BlockSpec((1, tb, GDN_W), fwd), pl.BlockSpec((1, tb, GDN_W), bwd)]
        out_shape = [jax.ShapeDtypeStruct((bsz, n, GDN_W), BF16)] * 2
    else:
        out_specs = state_spec
        out_shape = jax.ShapeDtypeStruct(s0.shape, F32)
    return pl.pallas_call(
        functools.partial(_gdn_scan_kernel, tb=tb, chunk=chunk, with_output=with_output),
        grid=(bsz, nblk),
        in_specs=specs(fwd, fwd_t) + specs(bwd, bwd_t) + [state_spec],
        out_specs=out_specs,
        out_shape=out_shape,
        scratch_shapes=[pltpu.VMEM((2, GDN_HEADS, GDN_HEAD_DIM, GDN_HEAD_DIM), F32)],
        compiler_params=_cparams("parallel", "arbitrary"),
        name="gdn_scan_latent" if with_output else "gdn_scan_context",
    )(q, k, v, gbeta, gt, q, k, v, gbeta, gt, s0)


def _merge_kernel(x_ref, m_ref, g_ref, b_ref, oa_ref, of_ref, ob_ref, z_ref, gab_ref, nw_ref,
                  wpa_ref, wpb_ref, wo_ref, o_ref, *, alpha, n_sub):
    nw = nw_ref[...]
    d = x_ref.shape[-1]
    gate = m_ref[0, 5:6, :]
    ts = x_ref.shape[1] // n_sub
    rows = [slice(i * ts, (i + 1) * ts) for i in range(n_sub)]

    def gdn_out(r):
        o = of_ref[0, r, :].astype(F32) + ob_ref[0, r, :].astype(F32)
        parts = []
        for h in range(GDN_HEADS):
            oh = o[:, h * GDN_HEAD_DIM:(h + 1) * GDN_HEAD_DIM]
            parts.append(oh * lax.rsqrt(jnp.mean(oh * oh, axis=-1, keepdims=True) + NORM_EPS) * nw)
        return (jnp.concatenate(parts, axis=1) * _silu(z_ref[0, r, :].astype(F32))).astype(BF16)

    o_b = [gdn_out(r) for r in rows]
    ya = [_dot(oa_ref[0, r, :], wpa_ref[...]) for r in rows]
    yb = [_dot(ob, wpb_ref[...]) for ob in o_b]
    y = [(_sigmoid(gab_ref[0, r, :d].astype(F32)) * a
          + _sigmoid(gab_ref[0, r, d:].astype(F32)) * b).astype(BF16) for r, a, b in zip(rows, ya, yb)]
    yo = [_dot(yi, wo_ref[...]) for yi in y]
    for r, yoi in zip(rows, yo):
        o_ref[0, r, :] = _ln(alpha * x_ref[0, r, :] + gate * yoi) * g_ref[1:2, :] + b_ref[1:2, :]


def _merge_call(x, mod3, ln_g, ln_b, o_a, o_f, o_b, z, gab, norm_w, w_pa, w_pb, w_o, *, alpha, tm, n_sub):
    bsz, n, d = x.shape
    tok = lambda w: pl.BlockSpec((1, tm, w), lambda b, i: (b, i, 0))
    full = lambda a: pl.BlockSpec(a.shape, lambda b, i: (0,) * a.ndim)
    return pl.pallas_call(
        functools.partial(_merge_kernel, alpha=alpha, n_sub=n_sub),
        grid=(bsz, n // tm),
        in_specs=[tok(d), pl.BlockSpec((1, N_MOD, d), lambda b, i: (b, 0, 0)), full(ln_g), full(ln_b),
                  tok(NA_W), tok(GDN_W), tok(GDN_W), tok(GDN_W), tok(2 * d), full(norm_w),
                  full(w_pa), full(w_pb), full(w_o)],
        out_specs=tok(d),
        out_shape=jax.ShapeDtypeStruct((bsz, n, d), F32),
        compiler_params=_cparams("parallel", "parallel"),
        name="branch_merge",
    )(x, mod3, ln_g, ln_b, o_a, o_f, o_b, z, gab, norm_w, w_pa, w_pb, w_o)


def _rope_tables(n_tok):
    n_freq = GDN_HEAD_DIM // 4
    freqs = ROPE_THETA ** (-np.arange(n_freq, dtype=np.float64) / n_freq)
    t = np.arange(n_tok)
    pos = np.stack([t // GRID_W, t % GRID_W], axis=-1).astype(np.float64)
    ang = pos[:, :, None] * freqs
    cos = np.cos(ang)
    sin = np.sin(ang)
    cos_t = np.concatenate([cos, cos], axis=-1).reshape(n_tok, GDN_HEAD_DIM)
    sin_t = np.concatenate([-sin, sin], axis=-1).reshape(n_tok, GDN_HEAD_DIM)
    return jnp.asarray(cos_t, F32), jnp.asarray(sin_t, F32)


def _pick_tile(n, pref):
    t = min(n, pref)
    assert n % t == 0
    return t


def kernel(x, c, ctx, c_ctx, w_ada, b_ada, ln_g, ln_b, ffn1_w_in, ffn1_w_out, w_in, na_rpb, gdn_conv_w,
           gdn_a_log, gdn_dt_bias, gdn_norm_w, w_pa, w_pb, w_o, ffn2_w_in, ffn2_w_out):
    depth = w_ada.shape[0]
    assert depth == 1, "context-update path of non-final layers is not implemented"
    bsz, n, d = x.shape
    lc = ctx.shape[1]
    alpha = (2 * depth) ** 0.25
    layer = 0

    n_rows = -(-(bsz + 1) // SUBLANES) * SUBLANES
    cc = jnp.zeros((n_rows, d), F32).at[:bsz].set(c).at[bsz].set(c_ctx)
    mod3 = _mod_call(cc, w_ada[layer], b_ada[layer], tn=MOD_COLS).reshape(n_rows, N_MOD, d)

    g_ln, b_ln = ln_g[layer], ln_b[layer]
    w1u, w1d = ffn1_w_in[layer].astype(BF16), ffn1_w_out[layer].astype(BF16)
    dense = dict(tm=_pick_tile(n, DENSE_TOKENS), n_sub=DENSE_SUBTILES)
    dense_ctx = dict(tm=_pick_tile(bsz * lc, DENSE_TOKENS), n_sub=DENSE_SUBTILES)
    ffn = dict(tm=_pick_tile(n, FFN_TOKENS), n_sub=FFN_SUBTILES)
    x1 = _ffn_call(x, mod3, g_ln, b_ln, w1u, w1d, sub=0, alpha=alpha, mod_row=None, **ffn)
    ctx_flat = ctx.reshape(1, bsz * lc, d)
    ctx1 = _ffn_call(ctx_flat, mod3, g_ln, b_ln, w1u, w1d, sub=0, alpha=alpha, mod_row=bsz, **dense_ctx)

    w_a, w_g, w_z, w_gab, w_ab = _regroup_w_in(w_in[layer])
    qkv_a, qkv_g, z, gab, ab = _proj_call(x1, mod3, (w_a, w_g, w_z, w_gab, w_ab), (BF16, F32, BF16, BF16, F32),
                                          mod_row=None, **dense)
    kv_a_c, qkv_g_c, ab_c = (t.reshape(bsz, lc, t.shape[-1]) for t in _proj_call(
        ctx1, mod3, (w_a[:, NA_W:], w_g, w_ab), (BF16, F32, F32), mod_row=bsz, **dense_ctx))

    o_a = _attn_call(qkv_a, kv_a_c, _na_bias_table(na_rpb[layer]), tr=ATTN_ROWS, rpi=ATTN_ROWS_PER_ITER)

    nea = jnp.zeros((1, LANES), F32).at[0, :2 * GDN_HEADS].set(-jnp.exp(gdn_a_log[layer].astype(F32)).reshape(-1))
    dtb = jnp.zeros((1, LANES), F32).at[0, :2 * GDN_HEADS].set(gdn_dt_bias[layer].astype(F32).reshape(-1))
    cw = gdn_conv_w[layer]
    cos_t, sin_t = _rope_tables(n)
    tp = _pick_tile(n, GDN_TOKENS)
    tpc = _pick_tile(lc, GDN_TOKENS)
    q_l, k_l, v_l, gb_l, gt_l = _gdn_prep_call(qkv_g, ab, cw, nea, dtb, cos_t, sin_t, use_rope=True, t=tp,
                                                chunk=GDN_CHUNK)
    q_c, k_c, v_c, gb_c, gt_c = _gdn_prep_call(qkv_g_c, ab_c, cw, nea, dtb, cos_t[:lc], sin_t[:lc], use_rope=False,
                                                t=tpc, chunk=GDN_CHUNK)
    s0 = jnp.zeros((bsz, 2, GDN_HEADS, GDN_HEAD_DIM, GDN_HEAD_DIM), F32)
    s_ctx = _gdn_scan_call(q_c, k_c, v_c, gb_c, gt_c, s0, tb=tpc, chunk=GDN_CHUNK, with_output=False)
    o_f, o_b = _gdn_scan_call(q_l, k_l, v_l, gb_l, gt_l, s_ctx, tb=_pick_tile(n, GDN_SCAN_TOKENS), chunk=GDN_CHUNK,
                              with_output=True)

    x2 = _merge_call(x1, mod3, g_ln, b_ln, o_a, o_f, o_b, z, gab, gdn_norm_w[layer].reshape(1, GDN_HEAD_DIM),
                     w_pa[layer].astype(BF16), w_pb[layer].astype(BF16), w_o[layer].astype(BF16),
                     alpha=alpha, **dense)

    w2u, w2d = ffn2_w_in[layer].astype(BF16), ffn2_w_out[layer].astype(BF16)
    return _ffn_call(x2, mod3, g_ln, b_ln, w2u, w2d, sub=2, alpha=alpha, mod_row=None, **ffn)
```

```python
import functools
import math

import numpy as np
import jax
import jax.numpy as jnp
from jax import lax
from jax.experimental import pallas as pl
from jax.experimental.pallas import tpu as pltpu

F32 = jnp.float32
BF16 = jnp.bfloat16

GRID_W = 64
NA_HEADS = 8
NA_HEAD_DIM = 64
NA_WIN_ROWS = 8
NA_WIN_COLS = 16
GDN_HEADS = 4
GDN_HEAD_DIM = 128
GDN_CONV = 5
N_MOD = 9
ROPE_THETA = 10000.0
LN_EPS = 1e-6
NORM_EPS = 1e-6
NA_W = NA_HEADS * NA_HEAD_DIM
GDN_W = GDN_HEADS * GDN_HEAD_DIM

LANES = 128
SUBLANES = 8
VMEM_LIMIT_BYTES = 56 * 1024 * 1024
MASK_VALUE = -1e30
GDN_CHUNK = 128

DENSE_TOKENS = 512
DENSE_SUBTILES = 2
FFN_TOKENS = 1024
FFN_SUBTILES = 4
GDN_TOKENS = 2 * GDN_CHUNK
GDN_SCAN_TOKENS = 4 * GDN_CHUNK
ATTN_ROWS = 8
ATTN_ROWS_PER_ITER = 4
MOD_COLS = 9 * LANES


def _cparams(*sem):
    return pltpu.CompilerParams(dimension_semantics=sem, vmem_limit_bytes=VMEM_LIMIT_BYTES)


def _ln(x):
    mu = jnp.mean(x, axis=-1, keepdims=True)
    xc = x - mu
    var = jnp.mean(xc * xc, axis=-1, keepdims=True)
    return xc * lax.rsqrt(var + LN_EPS)


def _sigmoid(x):
    return 0.5 * jnp.tanh(0.5 * x) + 0.5


def _silu(x):
    h = 0.5 * x
    return h + h * jnp.tanh(h)


def _dot(a, b):
    return jnp.dot(a, b, preferred_element_type=F32)


def _dot_nt(a, b):
    return lax.dot_general(a, b, (((1,), (1,)), ((), ())), preferred_element_type=F32)


def _dot_tn(a, b):
    return lax.dot_general(a, b, (((0,), (0,)), ((), ())), preferred_element_type=F32)


def _dot_exact_lhs(a_bf16, x):
    hi = x.astype(BF16)
    r1 = x - hi.astype(F32)
    mid = r1.astype(BF16)
    lo = (r1 - mid.astype(F32)).astype(BF16)
    return _dot(a_bf16, hi) + _dot(a_bf16, mid) + _dot(a_bf16, lo)


def _mod_kernel(c_ref, w_ref, b_ref, o_ref):
    s = _silu(c_ref[...]).astype(BF16)
    o_ref[...] = _dot(s, w_ref[...].astype(BF16)) + b_ref[...]


def _mod_call(cc, w_ada, b_ada, *, tn):
    r, d = cc.shape
    n = w_ada.shape[1]
    return pl.pallas_call(
        _mod_kernel,
        grid=(n // tn,),
        in_specs=[
            pl.BlockSpec((r, d), lambda j: (0, 0)),
            pl.BlockSpec((d, tn), lambda j: (0, j)),
            pl.BlockSpec((1, tn), lambda j: (0, j)),
        ],
        out_specs=pl.BlockSpec((r, tn), lambda j: (0, j)),
        out_shape=jax.ShapeDtypeStruct((r, n), F32),
        compiler_params=_cparams("arbitrary"),
        name="adaln_modulation",
    )(cc, w_ada, b_ada.reshape(1, n))


def _ffn_kernel(x_ref, m_ref, g_ref, b_ref, wa_ref, wb_ref, wd_ref, o_ref, *, sub, alpha, n_sub):
    shift = m_ref[0, 3 * sub:3 * sub + 1, :]
    scale = m_ref[0, 3 * sub + 1:3 * sub + 2, :]
    gate = 0.5 * m_ref[0, 3 * sub + 2:3 * sub + 3, :]
    ts = x_ref.shape[1] // n_sub
    rows = [slice(i * ts, (i + 1) * ts) for i in range(n_sub)]

    def modulated(r):
        return (_ln(x_ref[0, r, :]) * (1.0 + scale) + shift).astype(BF16)

    def finish(r, y):
        o_ref[0, r, :] = _ln(alpha * x_ref[0, r, :] + gate * y) * g_ref[sub:sub + 1, :] + b_ref[sub:sub + 1, :]

    u = modulated(rows[0])
    pending = None
    for i, r in enumerate(rows):
        u_next = modulated(rows[i + 1]) if i + 1 < n_sub else None
        h = (_silu(_dot(u, wa_ref[...])) * _dot(u, wb_ref[...])).astype(BF16)
        y = _dot(h, wd_ref[...])
        if pending is not None:
            finish(*pending)
        pending = (r, y)
        u = u_next
    finish(*pending)


def _ffn_call(x, mod3, ln_g, ln_b, w_up, w_down, *, sub, alpha, mod_row, tm, n_sub):
    bsz, n, d = x.shape
    f = w_down.shape[0]
    if mod_row is None:
        mod_idx = lambda b, i: (b, 0, 0)
    else:
        mod_idx = lambda b, i: (mod_row, 0, 0)
    once = pl.Buffered(1)
    return pl.pallas_call(
        functools.partial(_ffn_kernel, sub=sub, alpha=alpha, n_sub=n_sub),
        grid=(bsz, n // tm),
        in_specs=[
            pl.BlockSpec((1, tm, d), lambda b, i: (b, i, 0)),
            pl.BlockSpec((1, N_MOD, d), mod_idx),
            pl.BlockSpec(ln_g.shape, lambda b, i: (0, 0)),
            pl.BlockSpec(ln_b.shape, lambda b, i: (0, 0)),
            pl.BlockSpec((d, f), lambda b, i: (0, 0), pipeline_mode=once),
            pl.BlockSpec((d, f), lambda b, i: (0, 1), pipeline_mode=once),
            pl.BlockSpec((f, d), lambda b, i: (0, 0), pipeline_mode=once),
        ],
        out_specs=pl.BlockSpec((1, tm, d), lambda b, i: (b, i, 0)),
        out_shape=jax.ShapeDtypeStruct((bsz, n, d), F32),
        compiler_params=_cparams("parallel", "parallel"),
        name=f"ffn_sublayer_{sub}",
    )(x, mod3, ln_g, ln_b, w_up, w_up, w_down)


_O_NA = 3 * NA_W
_O_GDN = _O_NA + 3 * GDN_W
_O_Z = _O_GDN + GDN_W
_O_AB = _O_Z + 4 * GDN_HEADS


def _regroup_kernel(w_ref, wa_ref, wg_ref, wz_ref, wgab_ref, wab_ref):
    w = w_ref[...]
    wa_ref[...] = w[:, :_O_NA].astype(BF16)
    wg_ref[...] = w[:, _O_NA:_O_GDN].astype(BF16)
    wz_ref[...] = w[:, _O_GDN:_O_Z].astype(BF16)
    wgab_ref[...] = w[:, _O_AB:].astype(BF16)
    lane = lax.broadcasted_iota(jnp.int32, (w.shape[0], LANES), 1)
    wab_ref[...] = jnp.where(lane < _O_AB - _O_Z, w[:, _O_Z:_O_Z + LANES], 0.0).astype(BF16)


def _regroup_w_in(w_in, *, tr=128):
    d, n_in = w_in.shape
    widths = (_O_NA, _O_GDN - _O_NA, _O_Z - _O_GDN, n_in - _O_AB, LANES)
    return pl.pallas_call(
        _regroup_kernel,
        grid=(d // tr,),
        in_specs=[pl.BlockSpec((tr, n_in), lambda i: (i, 0))],
        out_specs=[pl.BlockSpec((tr, w), lambda i: (i, 0)) for w in widths],
        out_shape=[jax.ShapeDtypeStruct((d, w), BF16) for w in widths],
        compiler_params=_cparams("parallel"),
        name="regroup_w_in",
    )(w_in)

def _proj_kernel(x_ref, m_ref, *refs, n_sub):
    n_out = len(refs) // 2
    shift = m_ref[0, 3:4, :]
    scale = m_ref[0, 4:5, :]
    ts = x_ref.shape[1] // n_sub
    rows = [slice(i * ts, (i + 1) * ts) for i in range(n_sub)]
    u = [(_ln(x_ref[0, r, :]) * (1.0 + scale) + shift).astype(BF16) for r in rows]
    for w_ref, o_ref in zip(refs[:n_out], refs[n_out:]):
        for r, ui in zip(rows, u):
            o_ref[0, r, :] = _dot(ui, w_ref[...]).astype(o_ref.dtype)


def _proj_call(x, mod3, ws, dts, *, mod_row, tm, n_sub):
    bsz, n, d = x.shape
    if mod_row is None:
        mod_idx = lambda b, i: (b, 0, 0)
    else:
        mod_idx = lambda b, i: (mod_row, 0, 0)
    return pl.pallas_call(
        functools.partial(_proj_kernel, n_sub=n_sub),
        grid=(bsz, n // tm),
        in_specs=[pl.BlockSpec((1, tm, d), lambda b, i: (b, i, 0)),
                  pl.BlockSpec((1, N_MOD, d), mod_idx)]
                 + [pl.BlockSpec(w.shape, lambda b, i: (0, 0), pipeline_mode=pl.Buffered(1)) for w in ws],
        out_specs=[pl.BlockSpec((1, tm, w.shape[1]), lambda b, i: (b, i, 0)) for w in ws],
        out_shape=[jax.ShapeDtypeStruct((bsz, n, w.shape[1]), dt) for w, dt in zip(ws, dts)],
        compiler_params=_cparams("parallel", "parallel"),
        name="mixer_in_proj",
    )(x, mod3, *ws)


def _attn_kernel(q_ref, k_ref, v_ref, kc_ref, vc_ref, bias_ref, o_ref, *, tr, rows, rpi):
    i = pl.program_id(1)
    n_pairs = NA_HEADS // 2
    pw = 2 * NA_HEAD_DIM
    win = NA_WIN_ROWS * GRID_W
    lane = lax.broadcasted_iota(jnp.int32, (GRID_W, pw), 1)
    first_head = lane < NA_HEAD_DIM
    scale = NA_HEAD_DIM ** -0.5

    def row_body(it, carry):
        chains = [(jr, p) for jr in range(rpi) for p in range(n_pairs)]
        cs = [slice(p * pw, (p + 1) * pw) for jr, p in chains]
        q0, tok0, b0 = [], [], []
        for jr in range(rpi):
            rl = it * rpi + jr
            r = i * tr + rl
            r0 = jnp.clip(r - NA_WIN_ROWS // 2, 0, rows - NA_WIN_ROWS)
            tok0 += [pl.multiple_of(r0 * GRID_W, GRID_W)] * n_pairs
            q0 += [pl.multiple_of(rl * GRID_W, GRID_W)] * n_pairs
            b0 += [r0 - r + NA_WIN_ROWS - 1] * n_pairs
        ids = range(len(chains))
        qs = []
        for c in ids:
            q2 = q_ref[0, pl.ds(q0[c], GRID_W), cs[c]] * scale
            zero = jnp.zeros_like(q2)
            qs.append(jnp.concatenate([jnp.where(first_head, q2, zero), jnp.where(first_head, zero, q2)], axis=0))
        s_loc = [_dot_nt(qs[c], k_ref[0, pl.ds(tok0[c], win), cs[c]]) for c in ids]
        nq = 2 * GRID_W
        pcs = cs[:n_pairs]

        def per_chain(stacked):
            return [stacked[p][jr * nq:(jr + 1) * nq] for jr, p in chains]

        s_ctx = per_chain([_dot_nt(jnp.concatenate([qs[jr * n_pairs + p] for jr in range(rpi)], axis=0),
                                   kc_ref[0, :, pcs[p]]) for p in range(n_pairs)])
        s_loc = [s_loc[c] + jnp.concatenate([bias_ref[b0[c] + 2 * m, chains[c][1]] for m in range(NA_WIN_ROWS // 2)],
                                            axis=1) for c in ids]
        mx = [jnp.maximum(jnp.max(s_loc[c], axis=-1, keepdims=True), jnp.max(s_ctx[c], axis=-1, keepdims=True))
              for c in ids]
        p_loc = [jnp.exp(s_loc[c] - mx[c]) for c in ids]
        p_ctx = [jnp.exp(s_ctx[c] - mx[c]) for c in ids]
        den = [jnp.sum(p_loc[c], axis=-1, keepdims=True) + jnp.sum(p_ctx[c], axis=-1, keepdims=True) for c in ids]
        o_ctx = per_chain([_dot(jnp.concatenate([p_ctx[jr * n_pairs + p] for jr in range(rpi)], axis=0).astype(BF16),
                                vc_ref[0, :, pcs[p]]) for p in range(n_pairs)])
        o = [_dot(p_loc[c].astype(BF16), v_ref[0, pl.ds(tok0[c], win), cs[c]]) + o_ctx[c] for c in ids]
        for c in ids:
            on = o[c] / den[c]
            o_ref[0, pl.ds(q0[c], GRID_W), cs[c]] = jnp.where(first_head, on[:GRID_W], on[GRID_W:]).astype(o_ref.dtype)
        return carry

    lax.fori_loop(0, tr // rpi, row_body, 0)


def _attn_call(qkv, kv_ctx, bias_tab, *, tr, rpi):
    bsz, n, _ = qkv.shape
    rows = n // GRID_W
    lc = kv_ctx.shape[1]
    assert rows >= NA_WIN_ROWS and rows % tr == 0 and tr % rpi == 0
    return pl.pallas_call(
        functools.partial(_attn_kernel, tr=tr, rows=rows, rpi=rpi),
        grid=(bsz, rows // tr),
        in_specs=[
            pl.BlockSpec((1, tr * GRID_W, NA_W), lambda b, i: (b, i, 0)),
            pl.BlockSpec((1, n, NA_W), lambda b, i: (b, 0, 1)),
            pl.BlockSpec((1, n, NA_W), lambda b, i: (b, 0, 2)),
            pl.BlockSpec((1, lc, NA_W), lambda b, i: (b, 0, 0)),
            pl.BlockSpec((1, lc, NA_W), lambda b, i: (b, 0, 1)),
            pl.BlockSpec(bias_tab.shape, lambda b, i: (0, 0, 0, 0)),
        ],
        out_specs=pl.BlockSpec((1, tr * GRID_W, NA_W), lambda b, i: (b, i, 0)),
        out_shape=jax.ShapeDtypeStruct((bsz, n, NA_W), BF16),
        compiler_params=_cparams("parallel", "arbitrary"),
        name="neighbourhood_attention",
    )(qkv, qkv, qkv, kv_ctx, kv_ctx, bias_tab)


def _na_bias_table(rpb):
    n_heads, n_rel_rows, n_rel = rpb.shape
    rows = jnp.zeros((n_rel_rows, n_heads, LANES), F32).at[:, :, :n_rel].set(rpb.astype(F32).transpose(1, 0, 2))
    return pl.pallas_call(
        _na_bias_kernel,
        grid=(n_rel_rows - 1,),
        in_specs=[pl.BlockSpec(rows.shape, lambda r: (0, 0, 0))],
        out_specs=pl.BlockSpec((1, n_heads // 2, 2 * GRID_W, 2 * GRID_W), lambda r: (r, 0, 0, 0)),
        out_shape=jax.ShapeDtypeStruct((n_rel_rows - 1, n_heads // 2, 2 * GRID_W, 2 * GRID_W), F32),
        compiler_params=_cparams("parallel"),
        name="na_bias_table",
    )(rows)


def _na_bias_kernel(rows_ref, o_ref):
    r = pl.program_id(0)
    w = lax.broadcasted_iota(jnp.int32, (GRID_W, LANES), 0)
    lane = lax.broadcasted_iota(jnp.int32, (GRID_W, LANES), 1)
    c = lane % GRID_W
    c0 = jnp.clip(w - NA_WIN_COLS // 2, 0, GRID_W - NA_WIN_COLS)
    valid = (c >= c0) & (c < c0 + NA_WIN_COLS)
    for p in range(NA_HEADS // 2):
        for h2 in range(2):
            halves = []
            for i2 in range(2):
                v = jnp.broadcast_to(rows_ref[r + i2, 2 * p + h2:2 * p + h2 + 1, :], (GRID_W, LANES))
                shift = (LANES - (NA_WIN_COLS - 1) + i2 * GRID_W) % LANES
                halves.append(pltpu.roll(v, shift, 1, stride=1, stride_axis=0))
            tile = jnp.where(lane < GRID_W, halves[0], halves[1])
            o_ref[0, p, h2 * GRID_W:(h2 + 1) * GRID_W, :] = jnp.where(valid, tile, MASK_VALUE)


def _gdn_prep_kernel(x_ref, xp_ref, xn_ref, ab_ref, cw_ref, nea_ref, dtb_ref, cos_ref, sin_ref, mats_ref,
                     q_ref, k_ref, v_ref, gb_ref, gt_ref, *, t, use_rope):
    i = pl.program_id(1)
    halo = SUBLANES
    pad = GDN_CONV // 2
    hd = GDN_HEAD_DIM
    taps = [j for j in range(GDN_CONV) if j != pad]
    n_tap = len(taps)
    shift_stack = mats_ref[0:n_tap].reshape(n_tap * t, t)
    first = i > 0
    last = i < pl.num_programs(1) - 1
    zeros = jnp.zeros((halo, hd), F32)
    lane = lax.broadcasted_iota(jnp.int32, (t, hd), 1)
    half0 = (lane % (hd // 2)) < (hd // 4)
    quarter = hd // 4
    shifted2 = None
    for cb in range(3 * GDN_HEADS):
        cs = slice(cb * hd, (cb + 1) * hd)
        xb = x_ref[0, :, cs]
        cw = cw_ref[:, cs]
        if cb % 2 == 0:
            shifted2 = _dot(shift_stack, x_ref[0, :, cb * hd:(cb + 2) * hd].astype(BF16))
        shifted = shifted2[:, (cb % 2) * hd:(cb % 2 + 1) * hd]
        acc = cw[pad:pad + 1, :] * xb
        for n, j in enumerate(taps):
            acc = acc + cw[j:j + 1, :] * shifted[n * t:(n + 1) * t]
        top = jnp.concatenate([jnp.where(first, xp_ref[0, :, cs], 0.0), zeros], axis=0)
        bot = jnp.concatenate([zeros, jnp.where(last, xn_ref[0, :, cs], 0.0)], axis=0)
        e_top = sum(cw[j:j + 1, :] * top[halo - pad + j:2 * halo - pad + j, :] for j in range(pad))
        e_bot = sum(cw[j:j + 1, :] * bot[j - pad:j - pad + halo, :] for j in range(pad + 1, GDN_CONV))
        acc = jnp.concatenate([acc[:halo] + e_top, acc[halo:t - halo], acc[t - halo:] + e_bot], axis=0)
        y = _silu(acc)
        which, h = divmod(cb, GDN_HEADS)
        if which < 2:
            y = y * lax.rsqrt(jnp.sum(y * y, axis=-1, keepdims=True) + NORM_EPS)
            if use_rope:
                partner = jnp.where(half0, pltpu.roll(y, hd - quarter, 1), pltpu.roll(y, quarter, 1))
                y = y * cos_ref[...] + partner * sin_ref[...]
        if which == 0:
            y = y * (hd ** -0.5)
        (q_ref, k_ref, v_ref)[which][0, :, h * hd:(h + 1) * hd] = y

    ab = ab_ref[0]
    nh2 = 2 * GDN_HEADS
    za = ab + dtb_ref[...]
    softplus = jnp.maximum(za, 0.0) + jnp.log(1.0 + jnp.exp(-jnp.abs(za)))
    g = nea_ref[...] * softplus
    beta = _sigmoid(ab)
    n_tap = GDN_CONV - 1
    tri_f, tri_b, same = mats_ref[n_tap], mats_ref[n_tap + 1], mats_ref[n_tap + 2]
    lane_g = lax.broadcasted_iota(jnp.int32, (t, LANES), 1)
    gcum = jnp.where(lane_g < GDN_HEADS, _dot_exact_lhs(tri_f, g), _dot_exact_lhs(tri_b, g))
    gtot = pltpu.roll(_dot_exact_lhs(same, g), 2 * nh2, 1)
    gb_ref[0] = jnp.where(lane_g < nh2, gcum,
                          jnp.where(lane_g < 2 * nh2, beta, jnp.where(lane_g < 3 * nh2, gtot, 0.0)))
    gt_ref[0] = jnp.transpose(jnp.where(lane_g < nh2, gcum, 0.0))[0:nh2, :]


def _prep_matrices(t, chunk):
    r = np.arange(t)[:, None]
    c = np.arange(t)[None, :]
    pad = GDN_CONV // 2
    shifts = [c == r + (j - pad) for j in range(GDN_CONV) if j != pad]
    same = (r // chunk) == (c // chunk)
    mats = shifts + [same & (c <= r), same & (c >= r), same]
    return jnp.asarray(np.stack(mats).astype(np.float32), dtype=BF16)


def _gdn_prep_call(qkv_g, ab, conv_w, nea_row, dtb_row, cos_t, sin_t, *, use_rope, t, chunk):
    bsz, n, c3 = qkv_g.shape
    nb8 = n // SUBLANES
    tb8 = t // SUBLANES
    f3 = jax.ShapeDtypeStruct((bsz, n, GDN_W), F32)
    mats = _prep_matrices(t, chunk)
    return pl.pallas_call(
        functools.partial(_gdn_prep_kernel, t=t, use_rope=use_rope),
        grid=(bsz, n // t),
        in_specs=[
            pl.BlockSpec((1, t, c3), lambda b, i: (b, i, 0)),
            pl.BlockSpec((1, SUBLANES, c3), lambda b, i: (b, jnp.maximum(i * tb8 - 1, 0), 0)),
            pl.BlockSpec((1, SUBLANES, c3), lambda b, i: (b, jnp.minimum((i + 1) * tb8, nb8 - 1), 0)),
            pl.BlockSpec((1, t, LANES), lambda b, i: (b, i, 0)),
            pl.BlockSpec(conv_w.shape, lambda b, i: (0, 0)),
            pl.BlockSpec((1, LANES), lambda b, i: (0, 0)),
            pl.BlockSpec((1, LANES), lambda b, i: (0, 0)),
            pl.BlockSpec((t, GDN_HEAD_DIM), lambda b, i: (i, 0)),
            pl.BlockSpec((t, GDN_HEAD_DIM), lambda b, i: (i, 0)),
            pl.BlockSpec(mats.shape, lambda b, i: (0, 0, 0)),
        ],
        out_specs=[
            pl.BlockSpec((1, t, GDN_W), lambda b, i: (b, i, 0)),
            pl.BlockSpec((1, t, GDN_W), lambda b, i: (b, i, 0)),
            pl.BlockSpec((1, t, GDN_W), lambda b, i: (b, i, 0)),
            pl.BlockSpec((1, t, LANES), lambda b, i: (b, i, 0)),
            pl.BlockSpec((1, 2 * GDN_HEADS, t), lambda b, i: (b, 0, i)),
        ],
        out_shape=[f3, f3, f3,
                   jax.ShapeDtypeStruct((bsz, n, LANES), F32),
                   jax.ShapeDtypeStruct((bsz, 2 * GDN_HEADS, n), F32)],
        compiler_params=_cparams("parallel", "parallel"),
        name="gdn_prepare",
    )(qkv_g, qkv_g, qkv_g, ab, conv_w, nea_row, dtb_row, cos_t, sin_t, mats)


TRI_BASE = 8


def _slabs(x, s, second):
    start = s if second else 0
    return jnp.concatenate([x[i:i + s] for i in range(start, x.shape[0], 2 * s)], axis=0)


def _interleave(upd, other, s, second):
    parts = []
    for p in range(other.shape[0] // (2 * s)):
        u = upd[s * p:s * (p + 1)]
        if second:
            parts += [other[2 * s * p:2 * s * p + s], u]
        else:
            parts += [u, other[2 * s * p + s:2 * s * (p + 1)]]
    return jnp.concatenate(parts, axis=0)


def _unit_tri_solve(ms, rhss, lower):
    c = ms[0].shape[0]
    n = len(ms)
    row = lax.broadcasted_iota(jnp.int32, (c, c), 0)
    col = lax.broadcasted_iota(jnp.int32, (c, c), 1)
    eye = (row == col).astype(F32)
    same = lambda s: (row // s) == (col // s)
    base = same(TRI_BASE)
    ds = [jnp.where(base, m, 0.0) for m in ms]
    ps = [eye - d for d in ds]
    dbs = [d.astype(BF16) for d in ds]
    qs = [_dot(db, db) for db in dbs]
    n_iter = int(math.log2(TRI_BASE)) - 1
    for it in range(n_iter):
        qbs = [q.astype(BF16) for q in qs]
        if it < n_iter - 1:
            yqs = [_dot(qb, jnp.concatenate([p, q], axis=1).astype(BF16)) for qb, p, q in zip(qbs, ps, qs)]
            ps = [p + yq[:, :c] for p, yq in zip(ps, yqs)]
            qs = [yq[:, c:] for yq in yqs]
        else:
            ys = [_dot(qb, p.astype(BF16)) for qb, p in zip(qbs, ps)]
            ps = [p + y for p, y in zip(ps, ys)]
    ts = ps
    zeros = jnp.zeros((c, c), F32)
    s = TRI_BASE
    while 2 * s < c:
        off_diag = same(2 * s) & jnp.logical_not(same(s))
        l_rows = [_slabs(jnp.where(off_diag, m, 0.0), s, lo).astype(BF16) for m, lo in zip(ms, lower)]
        t_rows = [_slabs(t, s, lo) for t, lo in zip(ts, lower)]
        ys = [_dot(l, t.astype(BF16)) for l, t in zip(l_rows, ts)]
        y_full = [_interleave(y, zeros, s, lo).astype(BF16) for y, lo in zip(ys, lower)]
        zs = [_dot(tr.astype(BF16), yf) for tr, yf in zip(t_rows, y_full)]
        ts = [_interleave(tr - z, t, s, lo) for tr, z, t, lo in zip(t_rows, zs, ts, lower)]
        s *= 2
    hc = c // 2
    off_diag = jnp.logical_not(same(hc))
    ind = [slice(0, hc) if lo else slice(hc, c) for lo in lower]
    dep = [slice(hc, c) if lo else slice(0, hc) for lo in lower]

    def place(x_ind, x_dep, lo):
        return jnp.concatenate([x_ind, x_dep] if lo else [x_dep, x_ind], axis=0)

    x1 = [_dot(t[i].astype(BF16), r.astype(BF16)) for t, i, r in zip(ts, ind, rhss)]
    mixed = [place(x, r[dp], lo).astype(BF16) for x, r, dp, lo in zip(x1, rhss, dep, lower)]
    ys = [_dot(jnp.where(off_diag, m, 0.0)[dp].astype(BF16), mx) for m, dp, mx in zip(ms, dep, mixed)]
    mixed2 = [place(r[i], r[dp] - y, lo).astype(BF16) for r, i, dp, y, lo in zip(rhss, ind, dep, ys, lower)]
    x2 = [_dot(t[dp].astype(BF16), mx) for t, dp, mx in zip(ts, dep, mixed2)]
    return [place(a, b, lo) for a, b, lo in zip(x1, x2, lower)]


def _gdn_scan_kernel(*refs, tb, chunk, with_output):
    (qf, kf, vf, gf, gtf, qb, kb, vb, gb, gtb, s0_ref) = refs[:11]
    if with_output:
        of_ref, ob_ref, s_ref = refs[11:]
        o_refs = (of_ref, ob_ref)
    else:
        sfin_ref, s_ref = refs[11:]
    ins = ((qf, kf, vf, gf, gtf), (qb, kb, vb, gb, gtb))
    j = pl.program_id(1)
    c = chunk
    nch = tb // c
    hd = GDN_HEAD_DIM

    @pl.when(j == 0)
    def _():
        s_ref[...] = s0_ref[0]

    row = lax.broadcasted_iota(jnp.int32, (c, c), 0)
    col = lax.broadcasted_iota(jnp.int32, (c, c), 1)
    incl = (col <= row, col >= row)
    strict = (col < row, col > row)
    nh2 = 2 * GDN_HEADS
    heads = [(d, h) for d in range(2) for h in range(GDN_HEADS)]
    units = [(ci, d, h) for ci in range(nch) for d, h in heads]
    off = lambda ci, d: (ci if d == 0 else nch - 1 - ci) * c
    assert c <= LANES

    gblk = {(ci, d): ins[d][3][0, off(ci, d):off(ci, d) + c, :] for ci in range(nch) for d in range(2)}
    gtblk = {(ci, d): ins[d][4][0, :, off(ci, d):off(ci, d) + c] for ci in range(nch) for d in range(2)}

    def lane_bcast(ci, d, h, o):
        dh = d * GDN_HEADS + h
        return jnp.broadcast_to(gblk[ci, d][:, o + dh:o + dh + 1], (c, LANES))

    g_b = [lane_bcast(ci, d, h, 0) for ci, d, h in units]
    beta = [lane_bcast(ci, d, h, nh2) for ci, d, h in units]
    g_last = [lane_bcast(ci, d, h, 2 * nh2) for ci, d, h in units]
    g_row = [gtblk[ci, d][d * GDN_HEADS + h:d * GDN_HEADS + h + 1, :] for ci, d, h in units]
    ld = lambda idx: [ins[d][idx][0, off(ci, d):off(ci, d) + c, h * hd:(h + 1) * hd] for ci, d, h in units]
    q, k, v = ld(0), ld(1), ld(2)
    decay = [jnp.exp(jnp.where(incl[d], gb_[:, :c] - gr, MASK_VALUE)) for (ci, d, h), gb_, gr in zip(units, g_b, g_row)]
    e_g = [jnp.exp(x) for x in g_b]
    k_beta = [a * b for a, b in zip(k, beta)]
    k16 = [a.astype(BF16) for a in k]
    lhs = [jnp.concatenate([a, b], axis=0) for a, b in zip(k_beta, q)] if with_output else k_beta
    kq = [_dot_nt(a.astype(BF16), b) for a, b in zip(lhs, k16)]
    m = [jnp.where(strict[d], a[:c] * b, 0.0) for (ci, d, h), a, b in zip(units, kq, decay)]
    rhs = [jnp.concatenate([a * b, kb_ * e], axis=1) for a, b, kb_, e in zip(v, beta, k_beta, e_g)]
    uw = _unit_tri_solve(m, rhs, [d == 0 for ci, d, h in units])
    k_tail = [a * jnp.exp(gl - gb_) for a, gl, gb_ in zip(k, g_last, g_b)]
    g_tot = [jnp.exp(gl[0:1, :]) for gl in g_last]

    s = [s_ref[d, h] for d, h in heads]
    nhd = len(heads)
    for ci in range(nch):
        u0 = ci * nhd
        sel = lambda xs_: xs_[u0:u0 + nhd]
        s16 = [a.astype(BF16) for a in s]
        lhs = ([jnp.concatenate([a[:, hd:], b * e], axis=0) for a, b, e in zip(sel(uw), sel(q), sel(e_g))]
               if with_output else [a[:, hd:] for a in sel(uw)])
        xs = [_dot(a.astype(BF16), b) for a, b in zip(lhs, s16)]
        v_new = [a[:, :hd] - b[:c] for a, b in zip(sel(uw), xs)]
        if with_output:
            o_local = [_dot((a[c:] * dcy).astype(BF16), b.astype(BF16))
                       for a, dcy, b in zip(sel(kq), sel(decay), v_new)]
            for (d, h), a, b in zip(heads, xs, o_local):
                o_refs[d][0, off(ci, d):off(ci, d) + c, h * hd:(h + 1) * hd] = (a[c:] + b).astype(o_refs[d].dtype)
        ktv = [_dot_tn(a.astype(BF16), b.astype(BF16)) for a, b in zip(sel(k_tail), v_new)]
        s = [a * gt_ + b for a, gt_, b in zip(s, sel(g_tot), ktv)]
    for (d, h), a in zip(heads, s):
        s_ref[d, h] = a

    if not with_output:
        @pl.when(j == pl.num_programs(1) - 1)
        def _():
            sfin_ref[0] = s_ref[...]


def _gdn_scan_call(q, k, v, gbeta, gt, s0, *, tb, chunk, with_output):
    bsz, n, _ = q.shape
    nblk = n // tb
    fwd = lambda b, j: (b, j, 0)
    bwd = lambda b, j: (b, nblk - 1 - j, 0)
    fwd_t = lambda b, j: (b, 0, j)
    bwd_t = lambda b, j: (b, 0, nblk - 1 - j)

    def specs(idx, idx_t):
        return [pl.BlockSpec((1, tb, GDN_W), idx)] * 3 + [pl.BlockSpec((1, tb, LANES), idx),
                                                          pl.BlockSpec((1, 2 * GDN_HEADS, tb), idx_t)]

    state_spec = pl.BlockSpec((1, 2, GDN_HEADS, GDN_HEAD_DIM, GDN_HEAD_DIM), lambda b, j: (b, 0, 0, 0, 0))
    if with_output:
        out_specs = [pl.BlockSpec((1, tb, GDN_W), fwd), pl.BlockSpec((1, tb, GDN_W), bwd)]
        out_shape = [jax.ShapeDtypeStruct((bsz, n, GDN_W), BF16)] * 2
    else:
        out_specs = state_spec
        out_shape = jax.ShapeDtypeStruct(s0.shape, F32)
    return pl.pallas_call(
        functools.partial(_gdn_scan_kernel, tb=tb, chunk=chunk, with_output=with_output),
        grid=(bsz, nblk),
        in_specs=specs(fwd, fwd_t) + specs(bwd, bwd_t) + [state_spec],
        out_specs=out_specs,
        out_shape=out_shape,
        scratch_shapes=[pltpu.VMEM((2, GDN_HEADS, GDN_HEAD_DIM, GDN_HEAD_DIM), F32)],
        compiler_params=_cparams("parallel", "arbitrary"),
        name="gdn_scan_latent" if with_output else "gdn_scan_context",
    )(q, k, v, gbeta, gt, q, k, v, gbeta, gt, s0)


def _merge_kernel(x_ref, m_ref, g_ref, b_ref, oa_ref, of_ref, ob_ref, z_ref, gab_ref, nw_ref,
                  wpa_ref, wpb_ref, wo_ref, o_ref, *, alpha, n_sub):
    nw = nw_ref[...]
    d = x_ref.shape[-1]
    gate = m_ref[0, 5:6, :]
    ts = x_ref.shape[1] // n_sub
    rows = [slice(i * ts, (i + 1) * ts) for i in range(n_sub)]

    def gdn_out(r):
        o = of_ref[0, r, :].astype(F32) + ob_ref[0, r, :].astype(F32)
        parts = []
        for h in range(GDN_HEADS):
            oh = o[:, h * GDN_HEAD_DIM:(h + 1) * GDN_HEAD_DIM]
            parts.append(oh * lax.rsqrt(jnp.mean(oh * oh, axis=-1, keepdims=True) + NORM_EPS) * nw)
        return (jnp.concatenate(parts, axis=1) * _silu(z_ref[0, r, :].astype(F32))).astype(BF16)

    o_b = [gdn_out(r) for r in rows]
    ya = [_dot(oa_ref[0, r, :], wpa_ref[...]) for r in rows]
    yb = [_dot(ob, wpb_ref[...]) for ob in o_b]
    y = [(_sigmoid(gab_ref[0, r, :d].astype(F32)) * a
          + _sigmoid(gab_ref[0, r, d:].astype(F32)) * b).astype(BF16) for r, a, b in zip(rows, ya, yb)]
    yo = [_dot(yi, wo_ref[...]) for yi in y]
    for r, yoi in zip(rows, yo):
        o_ref[0, r, :] = _ln(alpha * x_ref[0, r, :] + gate * yoi) * g_ref[1:2, :] + b_ref[1:2, :]


def _merge_call(x, mod3, ln_g, ln_b, o_a, o_f, o_b, z, gab, norm_w, w_pa, w_pb, w_o, *, alpha, tm, n_sub):
    bsz, n, d = x.shape
    tok = lambda w: pl.BlockSpec((1, tm, w), lambda b, i: (b, i, 0))
    full = lambda a: pl.BlockSpec(a.shape, lambda b, i: (0,) * a.ndim)
    return pl.pallas_call(
        functools.partial(_merge_kernel, alpha=alpha, n_sub=n_sub),
        grid=(bsz, n // tm),
        in_specs=[tok(d), pl.BlockSpec((1, N_MOD, d), lambda b, i: (b, 0, 0)), full(ln_g), full(ln_b),
                  tok(NA_W), tok(GDN_W), tok(GDN_W), tok(GDN_W), tok(2 * d), full(norm_w),
                  full(w_pa), full(w_pb), full(w_o)],
        out_specs=tok(d),
        out_shape=jax.ShapeDtypeStruct((bsz, n, d), F32),
        compiler_params=_cparams("parallel", "parallel"),
        name="branch_merge",
    )(x, mod3, ln_g, ln_b, o_a, o_f, o_b, z, gab, norm_w, w_pa, w_pb, w_o)


def _rope_tables(n_tok):
    n_freq = GDN_HEAD_DIM // 4
    freqs = ROPE_THETA ** (-np.arange(n_freq, dtype=np.float64) / n_freq)
    t = np.arange(n_tok)
    pos = np.stack([t // GRID_W, t % GRID_W], axis=-1).astype(np.float64)
    ang = pos[:, :, None] * freqs
    cos = np.cos(ang)
    sin = np.sin(ang)
    cos_t = np.concatenate([cos, cos], axis=-1).reshape(n_tok, GDN_HEAD_DIM)
    sin_t = np.concatenate([-sin, sin], axis=-1).reshape(n_tok, GDN_HEAD_DIM)
    return jnp.asarray(cos_t, F32), jnp.asarray(sin_t, F32)


def _pick_tile(n, pref):
    t = min(n, pref)
    assert n % t == 0
    return t


def kernel(x, c, ctx, c_ctx, w_ada, b_ada, ln_g, ln_b, ffn1_w_in, ffn1_w_out, w_in, na_rpb, gdn_conv_w,
           gdn_a_log, gdn_dt_bias, gdn_norm_w, w_pa, w_pb, w_o, ffn2_w_in, ffn2_w_out):
    depth = w_ada.shape[0]
    assert depth == 1, "context-update path of non-final layers is not implemented"
    bsz, n, d = x.shape
    lc = ctx.shape[1]
    alpha = (2 * depth) ** 0.25
    layer = 0

    n_rows = -(-(bsz + 1) // SUBLANES) * SUBLANES
    cc = jnp.zeros((n_rows, d), F32).at[:bsz].set(c).at[bsz].set(c_ctx)
    mod3 = _mod_call(cc, w_ada[layer], b_ada[layer], tn=MOD_COLS).reshape(n_rows, N_MOD, d)

    g_ln, b_ln = ln_g[layer], ln_b[layer]
    w1u, w1d = ffn1_w_in[layer].astype(BF16), ffn1_w_out[layer].astype(BF16)
    dense = dict(tm=_pick_tile(n, DENSE_TOKENS), n_sub=DENSE_SUBTILES)
    dense_ctx = dict(tm=_pick_tile(bsz * lc, DENSE_TOKENS), n_sub=DENSE_SUBTILES)
    ffn = dict(tm=_pick_tile(n, FFN_TOKENS), n_sub=FFN_SUBTILES)
    x1 = _ffn_call(x, mod3, g_ln, b_ln, w1u, w1d, sub=0, alpha=alpha, mod_row=None, **ffn)
    ctx_flat = ctx.reshape(1, bsz * lc, d)
    ctx1 = _ffn_call(ctx_flat, mod3, g_ln, b_ln, w1u, w1d, sub=0, alpha=alpha, mod_row=bsz, **dense_ctx)

    w_a, w_g, w_z, w_gab, w_ab = _regroup_w_in(w_in[layer])
    qkv_a, qkv_g, z, gab, ab = _proj_call(x1, mod3, (w_a, w_g, w_z, w_gab, w_ab), (BF16, F32, BF16, BF16, F32),
                                          mod_row=None, **ffn)
    kv_a_c, qkv_g_c, ab_c = (t.reshape(bsz, lc, t.shape[-1]) for t in _proj_call(
        ctx1, mod3, (w_a[:, NA_W:], w_g, w_ab), (BF16, F32, F32), mod_row=bsz, **dense_ctx))

    o_a = _attn_call(qkv_a, kv_a_c, _na_bias_table(na_rpb[layer]), tr=ATTN_ROWS, rpi=ATTN_ROWS_PER_ITER)

    nea = jnp.zeros((1, LANES), F32).at[0, :2 * GDN_HEADS].set(-jnp.exp(gdn_a_log[layer].astype(F32)).reshape(-1))
    dtb = jnp.zeros((1, LANES), F32).at[0, :2 * GDN_HEADS].set(gdn_dt_bias[layer].astype(F32).reshape(-1))
    cw = gdn_conv_w[layer]
    cos_t, sin_t = _rope_tables(n)
    tp = _pick_tile(n, GDN_TOKENS)
    tpc = _pick_tile(lc, GDN_TOKENS)
    q_l, k_l, v_l, gb_l, gt_l = _gdn_prep_call(qkv_g, ab, cw, nea, dtb, cos_t, sin_t, use_rope=True, t=tp,
                                                chunk=GDN_CHUNK)
    q_c, k_c, v_c, gb_c, gt_c = _gdn_prep_call(qkv_g_c, ab_c, cw, nea, dtb, cos_t[:lc], sin_t[:lc], use_rope=False,
                                                t=tpc, chunk=GDN_CHUNK)
    s0 = jnp.zeros((bsz, 2, GDN_HEADS, GDN_HEAD_DIM, GDN_HEAD_DIM), F32)
    s_ctx = _gdn_scan_call(q_c, k_c, v_c, gb_c, gt_c, s0, tb=tpc, chunk=GDN_CHUNK, with_output=False)
    o_f, o_b = _gdn_scan_call(q_l, k_l, v_l, gb_l, gt_l, s_ctx, tb=_pick_tile(n, GDN_SCAN_TOKENS), chunk=GDN_CHUNK,
                              with_output=True)

    x2 = _merge_call(x1, mod3, g_ln, b_ln, o_a, o_f, o_b, z, gab, gdn_norm_w[layer].reshape(1, GDN_HEAD_DIM),
                     w_pa[layer].astype(BF16), w_pb[layer].astype(BF16), w_o[layer].astype(BF16),
                     alpha=alpha, **dense)

    w2u, w2d = ffn2_w_in[layer].astype(BF16), ffn2_w_out[layer].astype(BF16)
    return _ffn_call(x2, mod3, g_ln, b_ln, w2u, w2d, sub=2, alpha=alpha, mod_row=None, **ffn)
```
